```python
import math
import jax
import jax.numpy as jnp
from jax import lax
import numpy as np

D_MODEL = 2048
BATCH = 8
SEQ = 2048
DEPTH = 4

GRID_W = 64
CTX_LEN = 256
NORM_EPS = 1e-6

HG_W = D_MODEL // 4
HG_HD = 128
HG_HEADS = HG_W // HG_HD
HG_CHUNK = 16
HG_COLS = 5 * HG_W

RW_W = D_MODEL // 4
RW_HD = 64
RW_HEADS = RW_W // RW_HD
RW_DECAY_LORA = 96
RW_A_LORA = 96
RW_G_LORA = 64
RW_GN_EPS = 64e-5
RW_DECAY_MAX = math.exp(-0.5)
RW_SPLITS = (RW_W, RW_W, RW_W, RW_DECAY_LORA, RW_DECAY_LORA, RW_A_LORA, RW_A_LORA, RW_G_LORA)
RW_COLS = sum(RW_SPLITS)

MLA_W = D_MODEL - HG_W - RW_W
MLA_V = 128
MLA_HEADS = MLA_W // MLA_V
MLA_NOPE = 128
MLA_ROPE = 64
MLA_Q_RANK = 512
MLA_KV_RANK = 256
MLA_COLS = MLA_Q_RANK + MLA_KV_RANK + MLA_ROPE
MLA_SCALE = (MLA_NOPE + MLA_ROPE) ** -0.5
ROPE_THETA = 10000.0
Q_BLOCK = 128

IN_COLS = HG_COLS + RW_COLS + MLA_COLS
D_FF = 5632

kernel_name = 'hybrid_hgrn2_rwkv7_mla_dit_block'


def _split_sizes(a, sizes):
    return jnp.split(a, [int(s) for s in np.cumsum(sizes)[:-1]], axis=-1)


def _heads(a, n_heads):
    return a.reshape(a.shape[:-1] + (n_heads, a.shape[-1] // n_heads))


def _merge_heads(a):
    return a.reshape(a.shape[:-2] + (a.shape[-2] * a.shape[-1],))


def _rmsnorm(x, gain):
    xf = x.astype(jnp.float32)
    y = xf * lax.rsqrt(jnp.mean(xf * xf, axis=-1, keepdims=True) + NORM_EPS)
    return (y * gain.astype(jnp.float32)).astype(x.dtype)


def _modulate(h, shift, scale):
    return h * (1.0 + scale) + shift


def _neighbours(u):
    up = jnp.pad(u, ((0, 0), (1, 1), (0, 0)))
    return up[:, :-2], up[:, 2:]


def _token_shift(p, mu):
    prev, nxt = _neighbours(p)
    return p + mu * (0.5 * (prev + nxt) - p)


def _dwconv3(u, w, b):
    prev, nxt = _neighbours(u)
    return prev * w[0] + u * w[1] + nxt * w[2] + b


def _axial_rope_tables(n_tokens):
    rows = n_tokens // GRID_W
    row = jnp.repeat(jnp.arange(rows, dtype=jnp.float32), GRID_W)
    col = jnp.tile(jnp.arange(GRID_W, dtype=jnp.float32), rows)
    axis_dim = MLA_ROPE // 2
    inv_freq = ROPE_THETA ** (-jnp.arange(0, axis_dim, 2, dtype=jnp.float32) / axis_dim)
    ang = jnp.concatenate([row[:, None] * inv_freq, col[:, None] * inv_freq], axis=-1)
    return jnp.cos(ang), jnp.sin(ang)


def _rope(x, cos, sin):
    half = x.shape[-1] // 2
    x1 = x[..., :half].astype(jnp.float32)
    x2 = x[..., half:].astype(jnp.float32)
    return jnp.concatenate([x1 * cos - x2 * sin, x2 * cos + x1 * sin], axis=-1).astype(x.dtype)


def _bidir(scan_fn, ctx_f, lat_f, ctx_b, lat_b):
    n_ctx = ctx_f[0].shape[1]
    fwd = scan_fn(*[jnp.concatenate([c, l], axis=1) for c, l in zip(ctx_f, lat_f)])
    bwd = scan_fn(*[jnp.concatenate([jnp.flip(c, 1), jnp.flip(l, 1)], axis=1) for c, l in zip(ctx_b, lat_b)])
    y_ctx = fwd[:, :n_ctx] + jnp.flip(bwd[:, :n_ctx], 1)
    y_lat = fwd[:, n_ctx:] + jnp.flip(bwd[:, n_ctx:], 1)
    return y_ctx, y_lat


def _hgrn2_chunk_scan(q, k, log_f, v):
    b, t, h, dk = q.shape
    n = t // HG_CHUNK

    def chunks(a):
        return a.reshape(b, n, HG_CHUNK, h, a.shape[-1]).transpose(1, 0, 3, 2, 4)

    q, k, log_f, v = chunks(q), chunks(k), chunks(log_f), chunks(v)
    g = jnp.cumsum(log_f, axis=3)
    causal = jnp.tril(jnp.ones((HG_CHUNK, HG_CHUNK), dtype=bool))[:, :, None]
    decay = jnp.exp(jnp.where(causal, g[..., :, None, :] - g[..., None, :, :], -jnp.inf))
    attn = jnp.einsum('nbhtd,nbhsd,nbhtsd->nbhts', q, k, decay)
    o_intra = jnp.einsum('nbhts,nbhse->nbhte', attn, v)
    g_last = g[..., -1:, :]
    q_dec = q * jnp.exp(g)
    k_dec = k * jnp.exp(g_last - g)
    f_chunk = jnp.exp(g_last[..., 0, :])

    def step(state, xs):
        qd, kd, fc, vc = xs
        o = jnp.einsum('bhtd,bhde->bhte', qd, state)
        state = fc[..., None] * state + jnp.einsum('bhsd,bhse->bhde', kd, vc)
        return state, o

    s0 = jnp.zeros((b, h, dk, v.shape[-1]), q.dtype)
    _, o_inter = lax.scan(step, s0, (q_dec, k_dec, f_chunk, v))
    return (o_intra + o_inter).transpose(1, 0, 3, 2, 4).reshape(b, t, h, -1)


def _hgrn2_mixer(p_ctx, p_lat, lb, gn_w, need_ctx):
    def prep(p):
        q, z_f, z_b, i, g = jnp.split(p.astype(jnp.float32), 5, axis=-1)
        q, i = _heads(q, HG_HEADS), _heads(i, HG_HEADS)
        dirs = []
        for d, z in enumerate((z_f, z_b)):
            f = lb[d] + (1.0 - lb[d]) * jax.nn.sigmoid(z)
            dirs.append((q, _heads(1.0 - f, HG_HEADS), _heads(jnp.log(f), HG_HEADS), i))
        return dirs, _heads(g, HG_HEADS)

    ctx_dirs, g_ctx = prep(p_ctx)
    lat_dirs, g_lat = prep(p_lat)
    o_ctx, o_lat = _bidir(_hgrn2_chunk_scan, ctx_dirs[0], lat_dirs[0], ctx_dirs[1], lat_dirs[1])

    def out(o, g):
        return _merge_heads(_rmsnorm(o, gn_w) * jax.nn.silu(g)).astype(p_lat.dtype)

    return (out(o_ctx, g_ctx) if need_ctx else None), out(o_lat, g_lat)


def _rwkv7_scan(r, w, k, v, a, b):
    def step(state, xs):
        rt, wt, kt, vt, at, bt = xs
        sa = jnp.einsum('bhvk,bhk->bhv', state, at)
        state = state * wt[:, :, None, :] + sa[..., None] * bt[:, :, None, :] + vt[..., None] * kt[:, :, None, :]
        return state, jnp.einsum('bhvk,bhk->bhv', state, rt)

    bsz, _, h, n = r.shape
    xs = tuple(jnp.moveaxis(u, 1, 0) for u in (r, w, k, v, a, b))
    _, o = lax.scan(step, jnp.zeros((bsz, h, n, n), r.dtype), xs)
    return jnp.moveaxis(o, 0, 1)


def _rwkv7_mixer(p_ctx, p_lat, mu, w0, w2, a0, a2, g2, k_k, k_a, r_k, gn_w, gn_b, need_ctx):
    def prep(p):
        s = _token_shift(p.astype(jnp.float32), mu)
        r, k, v, xw_f, xw_b, xa_f, xa_b, xg = _split_sizes(s, RW_SPLITS)
        kk = _heads(k * k_k, RW_HEADS)
        kk = kk / jnp.maximum(jnp.sqrt(jnp.sum(kk * kk, axis=-1, keepdims=True)), 1e-12)
        r, v = _heads(r, RW_HEADS), _heads(v, RW_HEADS)
        dirs = []
        for d, (xw, xa) in enumerate(((xw_f, xa_f), (xw_b, xa_b))):
            decay = jnp.exp(-RW_DECAY_MAX * jax.nn.sigmoid(w0[d] + jnp.tanh(xw) @ w2[d]))
            a = jax.nn.sigmoid(a0[d] + xa @ a2[d])
            k_d = _heads(k * (1.0 + (a - 1.0) * k_a), RW_HEADS)
            dirs.append((r, _heads(decay, RW_HEADS), k_d, v, -kk, kk * _heads(a, RW_HEADS)))
        g = jax.nn.sigmoid(xg) @ g2
        return dirs, g

    ctx_dirs, g_ctx = prep(p_ctx)
    lat_dirs, g_lat = prep(p_lat)
    o_ctx, o_lat = _bidir(_rwkv7_scan, ctx_dirs[0], lat_dirs[0], ctx_dirs[1], lat_dirs[1])

    def out(o, dirs, g):
        mean = jnp.mean(o, axis=-1, keepdims=True)
        var = jnp.mean(jnp.square(o - mean), axis=-1, keepdims=True)
        o = _merge_heads((o - mean) * lax.rsqrt(var + RW_GN_EPS)) * gn_w + gn_b
        r, k_f, v, k_b = dirs[0][0], dirs[0][2], dirs[0][3], dirs[1][2]
        bonus = _merge_heads(jnp.sum(r * (k_f + k_b) * r_k, axis=-1, keepdims=True) * v)
        return ((o + bonus) * g).astype(p_lat.dtype)

    return (out(o_ctx, ctx_dirs, g_ctx) if need_ctx else None), out(o_lat, lat_dirs, g_lat)


def _mla_q(p, q_norm, w_uq, qn_g, qr_g):
    q = _heads(_rmsnorm(p[..., :MLA_Q_RANK], q_norm) @ w_uq, MLA_HEADS)
    return _rmsnorm(q[..., :MLA_NOPE], qn_g), _rmsnorm(q[..., MLA_NOPE:], qr_g)


def _mla_kv(p, kv_norm, w_ukv, kn_g, kr_g):
    c_kv = p[..., MLA_Q_RANK:MLA_Q_RANK + MLA_KV_RANK]
    k_rope = p[..., MLA_Q_RANK + MLA_KV_RANK:]
    kv = _heads(_rmsnorm(c_kv, kv_norm) @ w_ukv, MLA_HEADS)
    return _rmsnorm(kv[..., :MLA_NOPE], kn_g), _rmsnorm(k_rope, kr_g), kv[..., MLA_NOPE:]


def _mla_attend(qn, qr, kn, kr, v):
    s = jnp.einsum('bqhd,bkhd->bhqk', qn, kn) + jnp.einsum('bqhd,bkd->bhqk', qr, kr)
    pr = jax.nn.softmax(s.astype(jnp.float32) * MLA_SCALE, axis=-1).astype(v.dtype)
    return jnp.einsum('bhqk,bkhd->bqhd', pr, v)


def _mla_mixer(p_ctx, p_lat, cos, sin, q_norm, w_uq, kv_norm, w_ukv, qn_g, qr_g, kn_g, kr_g, need_ctx):
    kn_c, kr_c, v_c = _mla_kv(p_ctx, kv_norm, w_ukv, kn_g, kr_g)
    kn_l, kr_l, v_l = _mla_kv(p_lat, kv_norm, w_ukv, kn_g, kr_g)
    kr_l = _rope(kr_l, cos, sin)
    kn = jnp.concatenate([kn_c, kn_l], axis=1)
    kr = jnp.concatenate([kr_c, kr_l], axis=1)
    v = jnp.concatenate([v_c, v_l], axis=1)
    qn_l, qr_l = _mla_q(p_lat, q_norm, w_uq, qn_g, qr_g)
    qr_l = _rope(qr_l, cos[:, None, :], sin[:, None, :])
    bsz, n_lat = p_lat.shape[:2]
    n_blk = n_lat // Q_BLOCK

    def blocks(a):
        return jnp.moveaxis(a.reshape((bsz, n_blk, Q_BLOCK) + a.shape[2:]), 1, 0)

    o_lat = lax.map(lambda qb: _mla_attend(qb[0], qb[1], kn, kr, v), (blocks(qn_l), blocks(qr_l)))
    o_lat = _merge_heads(jnp.moveaxis(o_lat, 0, 1).reshape(bsz, n_lat, MLA_HEADS, MLA_V))
    o_ctx = None
    if need_ctx:
        qn_c, qr_c = _mla_q(p_ctx, q_norm, w_uq, qn_g, qr_g)
        o_ctx = _merge_heads(_mla_attend(qn_c, qr_c, kn_c, kr_c, v_c))
    return o_ctx, o_lat


def _conv_ffn(h, w_up, dw, db, w_down):
    u = _dwconv3(h @ w_up, dw, db)
    a, b = jnp.split(u, 2, axis=-1)
    return (jax.nn.silu(a) * b) @ w_down


def setup_inputs(seed: int = 0) -> dict:
    key = jax.random.key(seed)
    keys = iter(jax.random.split(key, 40))

    def nrm(shape, scale):
        return jax.random.normal(next(keys), shape, jnp.float32) * scale

    def unif(shape, lo, hi):
        return jax.random.uniform(next(keys), shape, jnp.float32, lo, hi)

    L, D = DEPTH, D_MODEL
    return {
        'x': nrm((BATCH, SEQ, D), 1.0),
        'c': nrm((BATCH, D), 1.0),
        'ctx': nrm((BATCH, CTX_LEN, D), 1.0),
        'c_ctx': nrm((D,), 1.0),
        'norm_g': 1.0 + nrm((L, 2, D), 0.05),
        'w_mod': nrm((L, D, 6 * D), 0.5 * D ** -0.5),
        'b_mod': nrm((L, 6 * D), 0.05),
        'w_in': nrm((L, D, IN_COLS), D ** -0.5),
        'w_out': nrm((L, HG_W + RW_W + MLA_W, D), D ** -0.5),
        'hg_lb': nrm((L, 2, HG_W), 1.0),
        'hg_gn': 1.0 + nrm((L, HG_HD), 0.05),
        'rw_mu': unif((L, RW_COLS), 0.0, 1.0),
        'rw_w0': unif((L, 2, RW_W), -6.0, 1.0),
        'rw_w2': nrm((L, 2, RW_DECAY_LORA, RW_W), 0.5 * RW_DECAY_LORA ** -0.5),
        'rw_a0': nrm((L, 2, RW_W), 0.5),
        'rw_a2': nrm((L, 2, RW_A_LORA, RW_W), 0.5 * RW_A_LORA ** -0.5),
        'rw_g2': nrm((L, RW_G_LORA, RW_W), RW_G_LORA ** -0.5),
        'rw_kk': 0.85 + nrm((L, RW_W), 0.05),
        'rw_ka': 1.0 + nrm((L, RW_W), 0.05),
        'rw_rk': nrm((L, RW_HEADS, RW_HD), 0.1),
        'rw_gn_w': 1.0 + nrm((L, RW_W), 0.05),
        'rw_gn_b': nrm((L, RW_W), 0.02),
        'mla_q_norm': 1.0 + nrm((L, MLA_Q_RANK), 0.05),
        'mla_w_uq': nrm((L, MLA_Q_RANK, MLA_HEADS * (MLA_NOPE + MLA_ROPE)), MLA_Q_RANK ** -0.5),
        'mla_kv_norm': 1.0 + nrm((L, MLA_KV_RANK), 0.05),
        'mla_w_ukv': nrm((L, MLA_KV_RANK, MLA_HEADS * (MLA_NOPE + MLA_V)), MLA_KV_RANK ** -0.5),
        'mla_qn_g': 1.0 + nrm((L, MLA_NOPE), 0.05),
        'mla_qr_g': 1.0 + nrm((L, MLA_ROPE), 0.05),
        'mla_kn_g': 1.0 + nrm((L, MLA_NOPE), 0.05),
        'mla_kr_g': 1.0 + nrm((L, MLA_ROPE), 0.05),
        'ffn_up': nrm((L, D, 2 * D_FF), D ** -0.5),
        'ffn_dw': nrm((L, 3, 2 * D_FF), 0.2) + jnp.array([0.0, 1.0, 0.0], jnp.float32)[None, :, None],
        'ffn_db': nrm((L, 2 * D_FF), 0.02),
        'ffn_down': nrm((L, D_FF, D), D_FF ** -0.5),
    }


def reference(x, c, ctx, c_ctx, norm_g, w_mod, b_mod, w_in, w_out, hg_lb, hg_gn, rw_mu, rw_w0, rw_w2,
              rw_a0, rw_a2, rw_g2, rw_kk, rw_ka, rw_rk, rw_gn_w, rw_gn_b, mla_q_norm, mla_w_uq,
              mla_kv_norm, mla_w_ukv, mla_qn_g, mla_qr_g, mla_kn_g, mla_kr_g, ffn_up, ffn_dw, ffn_db,
              ffn_down):
    cos, sin = _axial_rope_tables(x.shape[1])
    lb_p = jax.nn.softmax(hg_lb.astype(jnp.float32), axis=0)
    lower_bounds = jnp.cumsum(lb_p, axis=0) - lb_p[0]
    silu_c = jax.nn.silu(c)[:, None, :]
    silu_cc = jax.nn.silu(c_ctx)[None, None, :]
    xc = ctx
    for l in range(DEPTH):
        need_ctx = l < DEPTH - 1
        mod_l = jnp.split(silu_c @ w_mod[l] + b_mod[l], 6, axis=-1)
        mod_c = jnp.split(silu_cc @ w_mod[l] + b_mod[l], 6, axis=-1)
        p_lat = _modulate(_rmsnorm(x, norm_g[l, 0]), mod_l[0], mod_l[1]) @ w_in[l]
        p_ctx = _modulate(_rmsnorm(xc, norm_g[l, 0]), mod_c[0], mod_c[1]) @ w_in[l]
        hg_c, rw_c, ml_c = _split_sizes(p_ctx, (HG_COLS, RW_COLS, MLA_COLS))
        hg_l, rw_l, ml_l = _split_sizes(p_lat, (HG_COLS, RW_COLS, MLA_COLS))
        hg_oc, hg_ol = _hgrn2_mixer(hg_c, hg_l, lower_bounds[l], hg_gn[l], need_ctx)
        rw_oc, rw_ol = _rwkv7_mixer(rw_c, rw_l, rw_mu[l], rw_w0[l], rw_w2[l], rw_a0[l], rw_a2[l], rw_g2[l],
                                    rw_kk[l], rw_ka[l], rw_rk[l], rw_gn_w[l], rw_gn_b[l], need_ctx)
        ml_oc, ml_ol = _mla_mixer(ml_c, ml_l, cos, sin, mla_q_norm[l], mla_w_uq[l], mla_kv_norm[l],
                                  mla_w_ukv[l], mla_qn_g[l], mla_qr_g[l], mla_kn_g[l], mla_kr_g[l], need_ctx)
        x = x + mod_l[2] * (jnp.concatenate([hg_ol, rw_ol, ml_ol], axis=-1) @ w_out[l])
        h2 = _modulate(_rmsnorm(x, norm_g[l, 1]), mod_l[3], mod_l[4])
        x = x + mod_l[5] * _conv_ffn(h2, ffn_up[l], ffn_dw[l], ffn_db[l], ffn_down[l])
        if need_ctx:
            xc = xc + mod_c[2] * (jnp.concatenate([hg_oc, rw_oc, ml_oc], axis=-1) @ w_out[l])
            h2c = _modulate(_rmsnorm(xc, norm_g[l, 1]), mod_c[3], mod_c[4])
            xc = xc + mod_c[5] * _conv_ffn(h2c, ffn_up[l], ffn_dw[l], ffn_db[l], ffn_down[l])
    return x
```

```python
import functools
import math

import jax
import jax.numpy as jnp
from jax import lax
from jax.experimental import pallas as pl
from jax.experimental.pallas import tpu as pltpu

F32 = jnp.float32
BF16 = jnp.bfloat16

NORM_EPS = 1e-6
GRID_W = 64
ROPE_THETA = 10000.0
HG_HD = 128
HG_CHUNK = 16
RW_HD = 64
RW_GN_EPS = 64e-5
RW_DECAY_MAX = math.exp(-0.5)
MLA_V = 128
MLA_NOPE = 128
MLA_ROPE = 64
MLA_SCALE = (MLA_NOPE + MLA_ROPE) ** -0.5
LANE = 128
SUBLANE = 8
VMEM_LIMIT = 56 * 1024 * 1024
HI = lax.Precision.HIGHEST


def _cparams(*sem):
    return pltpu.CompilerParams(dimension_semantics=sem, vmem_limit_bytes=VMEM_LIMIT)


def _bdot(a, b):
    return jnp.dot(a.astype(BF16), b.astype(BF16), preferred_element_type=F32)


def _sigmoid(x):
    return 1.0 / (1.0 + jnp.exp(-x))


def _silu(x):
    return x * _sigmoid(x)


def _pad_cols(a, width):
    return jnp.pad(a, [(0, 0)] * (a.ndim - 1) + [(0, width - a.shape[-1])])


def _mods_kernel(c_ref, w_ref, b_ref, o_ref):
    o_ref[...] = _bdot(_silu(c_ref[...]), w_ref[...]) + b_ref[...]


def _mods(cc, w_mod, b_mod):
    n_layers, d, n = w_mod.shape
    rows = cc.shape[0]
    tn = 768 if n % 768 == 0 else n
    return pl.pallas_call(
        _mods_kernel,
        out_shape=jax.ShapeDtypeStruct((n_layers, rows, n), F32),
        grid=(n_layers, n // tn),
        in_specs=[
            pl.BlockSpec((rows, d), lambda l, j: (0, 0)),
            pl.BlockSpec((None, d, tn), lambda l, j: (l, 0, j)),
            pl.BlockSpec((None, 1, tn), lambda l, j: (l, 0, j)),
        ],
        out_specs=pl.BlockSpec((None, rows, tn), lambda l, j: (l, 0, j)),
        compiler_params=_cparams("parallel", "parallel"),
    )(cc, w_mod, b_mod.reshape(n_layers, 1, n))


def _norm_mod(x, gain, shift, scale):
    y = x * lax.rsqrt(jnp.mean(x * x, axis=-1, keepdims=True) + NORM_EPS) * gain
    return y * (1.0 + scale) + shift


def _mod_spec(m, nb, d, n_extra):
    if n_extra == 0:
        return pl.BlockSpec((None, None, nb, 1, d), lambda bi, j: (m, jnp.minimum(j, 1), bi, 0, 0))
    return pl.BlockSpec((None, None, nb, 1, d), lambda bi, j, k: (m, jnp.minimum(j, 1), bi, 0, 0))


def _in_proj_kernel(x_ref, g_ref, sh_ref, sc_ref, w_ref, o_ref, h_ref):
    nb, tb, d = x_ref.shape

    @pl.when(pl.program_id(2) == 0)
    def _():
        h = _norm_mod(x_ref[...], g_ref[...], sh_ref[...], sc_ref[...])
        h_ref[...] = h.reshape(nb * tb, d).astype(BF16)

    o_ref[...] = jnp.dot(h_ref[...], w_ref[...], preferred_element_type=F32).reshape(o_ref.shape)


def _in_proj(x, gain, modtab, w, tb, nb, tn):
    b, t, d = x.shape
    n = w.shape[1]
    return pl.pallas_call(
        _in_proj_kernel,
        out_shape=jax.ShapeDtypeStruct((b, t, n), F32),
        grid=(b // nb, t // tb, n // tn),
        in_specs=[
            pl.BlockSpec((nb, tb, d), lambda bi, j, k: (bi, j, 0)),
            pl.BlockSpec((1, d), lambda bi, j, k: (0, 0)),
            _mod_spec(0, nb, d, 1),
            _mod_spec(1, nb, d, 1),
            pl.BlockSpec((d, tn), lambda bi, j, k: (0, k)),
        ],
        out_specs=pl.BlockSpec((nb, tb, tn), lambda bi, j, k: (bi, j, k)),
        scratch_shapes=[pltpu.VMEM((nb * tb, d), BF16)],
        compiler_params=_cparams("parallel", "parallel", "arbitrary"),
    )(x, gain, modtab, modtab, w)


def _out_proj_kernel(x_ref, hg_ref, rw_ref, ml_ref, w_ref, gate_ref, o_ref):
    nb, tb, tn = o_ref.shape
    n_hg, n_rw = hg_ref.shape[-1], rw_ref.shape[-1]
    rows = nb * tb
    acc = jnp.dot(hg_ref[...].reshape(rows, n_hg), w_ref[0:n_hg, :], preferred_element_type=F32)
    acc += jnp.dot(rw_ref[...].reshape(rows, n_rw), w_ref[n_hg:n_hg + n_rw, :], preferred_element_type=F32)
    acc += jnp.dot(ml_ref[...].reshape(rows, ml_ref.shape[-1]), w_ref[n_hg + n_rw:, :],
                   preferred_element_type=F32)
    o_ref[...] = x_ref[...] + gate_ref[...] * acc.reshape(nb, tb, tn)


def _out_proj(x, hg, rw, ml, w, modtab, tb, nb, tn):
    b, t, d = x.shape
    return pl.pallas_call(
        _out_proj_kernel,
        out_shape=jax.ShapeDtypeStruct((b, t, d), F32),
        grid=(b // nb, t // tb, d // tn),
        in_specs=[
            pl.BlockSpec((nb, tb, tn), lambda bi, j, k: (bi, j, k)),
            pl.BlockSpec((nb, tb, hg.shape[-1]), lambda bi, j, k: (bi, j, 0)),
            pl.BlockSpec((nb, tb, rw.shape[-1]), lambda bi, j, k: (bi, j, 0)),
            pl.BlockSpec((nb, tb, ml.shape[-1]), lambda bi, j, k: (bi, j, 0)),
            pl.BlockSpec((d, tn), lambda bi, j, k: (0, k)),
            pl.BlockSpec((None, None, nb, 1, tn), lambda bi, j, k: (2, jnp.minimum(j, 1), bi, 0, k)),
        ],
        out_specs=pl.BlockSpec((nb, tb, tn), lambda bi, j, k: (bi, j, k)),
        compiler_params=_cparams("parallel", "parallel", "arbitrary"),
    )(x, hg, rw, ml, w, modtab)


def _shift_rows(u, tb, first_rows, last_rows):
    rows = u.shape[0]
    row = lax.broadcasted_iota(jnp.int32, u.shape, 0)
    prev = pltpu.roll(u, 1, 0)
    nxt = pltpu.roll(u, rows - 1, 0)
    for n, (fr, lr) in enumerate(zip(first_rows, last_rows)):
        prev = jnp.where(row == n * tb, fr, prev)
        nxt = jnp.where(row == n * tb + tb - 1, lr, nxt)
    return prev, nxt


def _ffn_kernel(x_ref, xp_ref, xn_ref, g_ref, sh_ref, sc_ref, gate_ref, wa_ref, wb_ref, dwa_ref,
                dwb_ref, dba_ref, dbb_ref, wd_ref, o_ref, h_ref):
    nb, tb, d = x_ref.shape
    rows = nb * tb
    halo = 2 * SUBLANE
    j = pl.program_id(1)
    k = pl.program_id(2)
    n_k = pl.num_programs(2)
    n_j = pl.num_programs(1)

    @pl.when(k == 0)
    def _():
        g, sh, sc = g_ref[...], sh_ref[...], sc_ref[...]
        h_ref[0:rows, :] = _norm_mod(x_ref[...], g, sh, sc).reshape(rows, d).astype(BF16)
        hp = _norm_mod(xp_ref[...], g, sh, sc).astype(BF16)
        hn = _norm_mod(xn_ref[...], g, sh, sc).astype(BF16)
        for n in range(nb):
            h_ref[rows + n * halo:rows + n * halo + SUBLANE, :] = hp[n]
            h_ref[rows + n * halo + SUBLANE:rows + (n + 1) * halo, :] = hn[n]

    p_ok = (j >= 2).astype(F32)
    n_ok = jnp.logical_and(j >= 1, j < n_j - 1).astype(F32)

    def conv_half(w_ref, dw_ref, db_ref):
        u = jnp.dot(h_ref[...], w_ref[...], preferred_element_type=F32)
        main = u[0:rows]
        firsts = [u[rows + n * halo + SUBLANE - 1:rows + n * halo + SUBLANE] * p_ok for n in range(nb)]
        lasts = [u[rows + n * halo + SUBLANE:rows + n * halo + SUBLANE + 1] * n_ok for n in range(nb)]
        prev, nxt = _shift_rows(main, tb, firsts, lasts)
        dw = dw_ref[...]
        return prev * dw[0:1] + main * dw[1:2] + nxt * dw[2:3] + db_ref[...]

    a = conv_half(wa_ref, dwa_ref, dba_ref)
    bgate = conv_half(wb_ref, dwb_ref, dbb_ref)
    y = jnp.dot((_silu(a) * bgate).astype(BF16), wd_ref[...], preferred_element_type=F32).reshape(nb, tb, d)

    @pl.when(k == 0)
    def _():
        o_ref[...] = y

    @pl.when(k > 0)
    def _():
        o_ref[...] += y

    @pl.when(k == n_k - 1)
    def _():
        o_ref[...] = x_ref[...] + gate_ref[...] * o_ref[...]


def _ffn(x, gain, modtab, w_up, dw, db, w_down, tb, nb, tk):
    b, t, d = x.shape
    d_ff = w_down.shape[0]
    n_k = d_ff // tk
    r8 = tb // SUBLANE
    n8 = t // SUBLANE

    def mspec(m):
        return pl.BlockSpec((None, None, nb, 1, d), lambda bi, j, k: (m, jnp.minimum(j, 1), bi, 0, 0))

    return pl.pallas_call(
        _ffn_kernel,
        out_shape=jax.ShapeDtypeStruct((b, t, d), F32),
        grid=(b // nb, t // tb, n_k),
        in_specs=[
            pl.BlockSpec((nb, tb, d), lambda bi, j, k: (bi, j, 0)),
            pl.BlockSpec((nb, SUBLANE, d), lambda bi, j, k: (bi, jnp.maximum(j * r8 - 1, 0), 0)),
            pl.BlockSpec((nb, SUBLANE, d), lambda bi, j, k: (bi, jnp.minimum((j + 1) * r8, n8 - 1), 0)),
            pl.BlockSpec((1, d), lambda bi, j, k: (0, 0)),
            mspec(3), mspec(4), mspec(5),
            pl.BlockSpec((d, tk), lambda bi, j, k: (0, k)),
            pl.BlockSpec((d, tk), lambda bi, j, k: (0, n_k + k)),
            pl.BlockSpec((3, tk), lambda bi, j, k: (0, k)),
            pl.BlockSpec((3, tk), lambda bi, j, k: (0, n_k + k)),
            pl.BlockSpec((1, tk), lambda bi, j, k: (0, k)),
            pl.BlockSpec((1, tk), lambda bi, j, k: (0, n_k + k)),
            pl.BlockSpec((tk, d), lambda bi, j, k: (k, 0)),
        ],
        out_specs=pl.BlockSpec((nb, tb, d), lambda bi, j, k: (bi, j, 0)),
        scratch_shapes=[pltpu.VMEM((nb * tb + nb * 2 * SUBLANE, d), BF16)],
        compiler_params=_cparams("parallel", "parallel", "arbitrary"),
    )(x, x, x, gain, modtab, modtab, modtab, w_up, w_up, dw, dw, db, db, w_down)


def _bwd_block(j, n_j):
    return jnp.where(j == 0, 0, n_j - j)


def _hg_scan_kernel(qf_ref, zf_ref, if_ref, qb_ref, zb_ref, ib_ref, lb_ref, of_ref, ob_ref,
                    st_ref, g_ref, k_ref):
    tb, w = zf_ref.shape
    n_heads = w // HG_HD
    n_chunks = tb // HG_CHUNK
    c = HG_CHUNK

    @pl.when(pl.program_id(1) == 0)
    def _():
        st_ref[...] = jnp.zeros_like(st_ref)

    ri = lax.broadcasted_iota(jnp.int32, (tb, tb), 0)
    ci = lax.broadcasted_iota(jnp.int32, (tb, tb), 1)
    same = (ri // c) == (ci // c)
    ones = jnp.ones((HG_HD, HG_HD), BF16)
    trow = lax.broadcasted_iota(jnp.int32, (c, HG_HD), 0)

    for dr, (q_ref, z_ref, i_ref, o_ref) in enumerate(((qf_ref, zf_ref, if_ref, of_ref),
                                                       (qb_ref, zb_ref, ib_ref, ob_ref))):
        lb = lb_ref[dr:dr + 1, :]
        f = lb + (1.0 - lb) * _sigmoid(z_ref[...])
        tri = jnp.logical_and(same, (ci <= ri) if dr == 0 else (ci >= ri)).astype(F32)
        g_ref[dr] = jnp.dot(tri, jnp.log(f), precision=HI, preferred_element_type=F32)
        k_ref[dr] = 1.0 - f

        def chunk(ic, carry, dr=dr, q_ref=q_ref, i_ref=i_ref, o_ref=o_ref):
            r0 = pl.multiple_of((ic if dr == 0 else n_chunks - 1 - ic) * c, c)
            for h in range(n_heads):
                ls = slice(h * HG_HD, (h + 1) * HG_HD)
                q = q_ref[pl.ds(r0, c), ls]
                v = i_ref[pl.ds(r0, c), ls]
                g = g_ref[dr, pl.ds(r0, c), ls]
                kk = k_ref[dr, pl.ds(r0, c), ls]
                o = jnp.zeros((c, HG_HD), F32)
                for s in range(c):
                    seen = (trow >= s) if dr == 0 else (trow <= s)
                    tile = jnp.where(seen, q * kk[s:s + 1] * jnp.exp(g - g[s:s + 1]), 0.0)
                    o += jnp.dot(tile.astype(BF16), ones, preferred_element_type=F32) * v[s:s + 1]
                g_last = g[c - 1:c] if dr == 0 else g[0:1]
                st = st_ref[dr, h]
                o += lax.dot_general((q * jnp.exp(g)).astype(BF16), st.astype(BF16),
                                     (((1,), (1,)), ((), ())), preferred_element_type=F32)
                kd = kk * jnp.exp(g_last - g)
                st_ref[dr, h] = st * jnp.exp(g_last) + lax.dot_general(
                    v.astype(BF16), kd.astype(BF16), (((0,), (0,)), ((), ())), preferred_element_type=F32)
                o_ref[pl.ds(r0, c), ls] = o
            return carry

        lax.fori_loop(0, n_chunks, chunk, 0)


def _hg_scan(p_hg, lb, tb):
    b, t, _ = p_hg.shape
    w = lb.shape[-1]
    n_j = t // tb

    def fwd(col):
        return pl.BlockSpec((None, tb, w), lambda bi, j: (bi, j, col))

    def bwd(col):
        return pl.BlockSpec((None, tb, w), lambda bi, j: (bi, _bwd_block(j, n_j), col))

    out = jax.ShapeDtypeStruct((b, t, w), F32)
    return pl.pallas_call(
        _hg_scan_kernel,
        out_shape=(out, out),
        grid=(b, n_j),
        in_specs=[fwd(0), fwd(1), fwd(3), bwd(0), bwd(2), bwd(3), pl.BlockSpec((2, w), lambda bi, j: (0, 0))],
        out_specs=(fwd(0), bwd(0)),
        scratch_shapes=[pltpu.VMEM((2, w // HG_HD, HG_HD, HG_HD), F32), pltpu.VMEM((2, tb, w), F32),
                        pltpu.VMEM((2, tb, w), F32)],
        compiler_params=_cparams("parallel", "arbitrary"),
    )(p_hg, p_hg, p_hg, p_hg, p_hg, p_hg, lb)


def _hg_finish_kernel(of_ref, ob_ref, gate_ref, gn_ref, o_ref):
    w = of_ref.shape[-1]
    for h in range(w // HG_HD):
        ls = slice(h * HG_HD, (h + 1) * HG_HD)
        o = of_ref[:, ls] + ob_ref[:, ls]
        o = o * lax.rsqrt(jnp.mean(o * o, axis=-1, keepdims=True) + NORM_EPS) * gn_ref[...]
        o_ref[:, ls] = (o * _silu(gate_ref[:, ls])).astype(o_ref.dtype)


def _hg_finish(o_f, o_b, p_hg, gn, tb):
    b, t, w = o_f.shape
    spec = pl.BlockSpec((None, tb, w), lambda bi, j: (bi, j, 0))
    return pl.pallas_call(
        _hg_finish_kernel,
        out_shape=jax.ShapeDtypeStruct((b, t, w), BF16),
        grid=(b, t // tb),
        in_specs=[spec, spec, pl.BlockSpec((None, tb, w), lambda bi, j: (bi, j, 4)),
                  pl.BlockSpec((1, HG_HD), lambda bi, j: (0, 0))],
        out_specs=spec,
        compiler_params=_cparams("parallel", "parallel"),
    )(o_f, o_b, p_hg, gn)


def _head_ones(width, hd):
    r = lax.broadcasted_iota(jnp.int32, (width, width), 0) // hd
    c = lax.broadcasted_iota(jnp.int32, (width, width), 1) // hd
    return (r == c).astype(F32)


def _rw_prep_kernel(p_ref, pp_ref, pn_ref, mu_ref, w0_ref, w2_ref, a0_ref, a2_ref, g2_ref, kk_ref,
                    ka_ref, r_o, v_o, kk_o, wf_o, kf_o, bf_o, wb_o, kb_o, bb_o, g_o):
    tb = p_ref.shape[0]
    w = r_o.shape[-1]
    lw = w2_ref.shape[1]
    j = pl.program_id(1)
    n_j = pl.num_programs(1)
    p = p_ref[...]
    p_ok = (j >= 2).astype(F32)
    n_ok = jnp.logical_and(j >= 1, j < n_j - 1).astype(F32)
    prev, nxt = _shift_rows(p, tb, [pp_ref[SUBLANE - 1:SUBLANE, :] * p_ok], [pn_ref[0:1, :] * n_ok])
    s = p + mu_ref[...] * (0.5 * (prev + nxt) - p)
    r, k, v = s[:, 0:w], s[:, w:2 * w], s[:, 2 * w:3 * w]
    lora = [s[:, 3 * w + i * lw:3 * w + (i + 1) * lw] for i in range(5)]
    kk = k * kk_ref[...]
    ssq = jnp.dot(kk * kk, _head_ones(w, RW_HD), precision=HI, preferred_element_type=F32)
    kk = kk / jnp.maximum(jnp.sqrt(ssq), 1e-12)
    r_o[...] = r
    v_o[...] = v
    kk_o[...] = kk
    for dr, (w_o, k_o, b_o) in enumerate(((wf_o, kf_o, bf_o), (wb_o, kb_o, bb_o))):
        xw, xa = lora[dr], lora[2 + dr]
        w_o[...] = jnp.exp(-RW_DECAY_MAX * _sigmoid(w0_ref[dr:dr + 1, :] + _bdot(jnp.tanh(xw), w2_ref[dr])))
        a = _sigmoid(a0_ref[dr:dr + 1, :] + _bdot(xa, a2_ref[dr]))
        k_o[...] = k * (1.0 + (a - 1.0) * ka_ref[...])
        b_o[...] = kk * a
    g_o[...] = _bdot(_sigmoid(lora[4]), g2_ref[...])


def _rw_prep(p_rw, mu, w0, w2, a0, a2, g2, k_k, k_a, tb):
    b, t, n = p_rw.shape
    w = w0.shape[-1]
    r8, n8 = tb // SUBLANE, t // SUBLANE
    out = jax.ShapeDtypeStruct((b, t, w), F32)
    ospec = pl.BlockSpec((None, tb, w), lambda bi, j: (bi, j, 0))

    def const(a):
        return pl.BlockSpec(a.shape, lambda bi, j: (0,) * a.ndim)

    return pl.pallas_call(
        _rw_prep_kernel,
        out_shape=(out,) * 10,
        grid=(b, t // tb),
        in_specs=[
            pl.BlockSpec((None, tb, n), lambda bi, j: (bi, j, 0)),
            pl.BlockSpec((None, SUBLANE, n), lambda bi, j: (bi, jnp.maximum(j * r8 - 1, 0), 0)),
            pl.BlockSpec((None, SUBLANE, n), lambda bi, j: (bi, jnp.minimum((j + 1) * r8, n8 - 1), 0)),
            const(mu), const(w0), const(w2), const(a0), const(a2), const(g2), const(k_k), const(k_a),
        ],
        out_specs=(ospec,) * 10,
        compiler_params=_cparams("parallel", "parallel"),
    )(p_rw, p_rw, p_rw, mu, w0, w2, a0, a2, g2, k_k, k_a)


def _rw_scan_kernel(rf_ref, vf_ref, kkf_ref, wf_ref, kf_ref, bf_ref, rb_ref, vb_ref, kkb_ref, wb_ref,
                    kb_ref, bb_ref, of_ref, ob_ref, st_ref):
    tb, w = rf_ref.shape
    n_pairs = w // LANE
    hd = RW_HD

    @pl.when(pl.program_id(1) == 0)
    def _():
        st_ref[...] = jnp.zeros_like(st_ref)

    same_head = _head_ones(LANE, hd).astype(BF16)
    lane = lax.broadcasted_iota(jnp.int32, (hd, LANE), 1)
    sub = lax.broadcasted_iota(jnp.int32, (hd, LANE), 0)
    diag = (lane % hd) == sub
    dirs = ((rf_ref, vf_ref, kkf_ref, wf_ref, kf_ref, bf_ref, of_ref),
            (rb_ref, vb_ref, kkb_ref, wb_ref, kb_ref, bb_ref, ob_ref))

    n_chains = 2 * n_pairs
    sub8 = lax.broadcasted_iota(jnp.int32, (SUBLANE, LANE), 0)

    def group(ig, carry):
        tiles = []
        for dr, refs in enumerate(dirs):
            r0 = pl.multiple_of((ig if dr == 0 else tb // SUBLANE - 1 - ig) * SUBLANE, SUBLANE)
            for hp in range(n_pairs):
                ls = slice(hp * LANE, (hp + 1) * LANE)
                tiles.append((dr, r0, ls, [ref[pl.ds(r0, SUBLANE), ls] for ref in refs[:6]]))
        states = [st_ref[ci] for ci in range(n_chains)]
        out_tiles = [jnp.zeros((SUBLANE, LANE), F32) for _ in range(n_chains)]
        for i in range(SUBLANE):
            lhs = []
            rows = []
            for ci, (dr, r0, ls, vals) in enumerate(tiles):
                s = i if dr == 0 else SUBLANE - 1 - i
                rr, vv, kk, ww, kd, bb = [a[s:s + 1] for a in vals]
                rows.append((s, rr, ww, kd, bb))
                lhs.append(states[ci] * kk)
                lhs.append(jnp.where(diag, vv, 0.0))
            red = jnp.dot(jnp.concatenate(lhs, axis=0).astype(BF16), same_head, preferred_element_type=F32)
            outs = []
            for ci, (s, rr, ww, kd, bb) in enumerate(rows):
                sa = red[(2 * ci) * hd:(2 * ci + 1) * hd]
                vcol = red[(2 * ci + 1) * hd:(2 * ci + 2) * hd]
                states[ci] = states[ci] * ww - sa * bb + vcol * kd
                outs.append(states[ci] * rr)
            red = jnp.dot(jnp.concatenate(outs, axis=0).astype(BF16), same_head, preferred_element_type=F32)
            for ci, (s, *_rest) in enumerate(rows):
                o = jnp.sum(jnp.where(diag, red[ci * hd:(ci + 1) * hd], 0.0), axis=0, keepdims=True)
                out_tiles[ci] = jnp.where(sub8 == s, o, out_tiles[ci])
        for ci, (dr, r0, ls, _vals) in enumerate(tiles):
            st_ref[ci] = states[ci]
            dirs[dr][6][pl.ds(r0, SUBLANE), ls] = out_tiles[ci]
        return carry

    lax.fori_loop(0, tb // SUBLANE, group, 0)


def _rw_scan(r, v, kk, w_f, k_f, b_f, w_b, k_b, b_b, tb):
    b, t, w = r.shape
    n_j = t // tb
    fwd = pl.BlockSpec((None, tb, w), lambda bi, j: (bi, j, 0))
    bwd = pl.BlockSpec((None, tb, w), lambda bi, j: (bi, _bwd_block(j, n_j), 0))
    out = jax.ShapeDtypeStruct((b, t, w), F32)
    return pl.pallas_call(
        _rw_scan_kernel,
        out_shape=(out, out),
        grid=(b, n_j),
        in_specs=[fwd] * 6 + [bwd] * 6,
        out_specs=(fwd, bwd),
        scratch_shapes=[pltpu.VMEM((2 * (w // LANE), RW_HD, LANE), F32)],
        compiler_params=_cparams("parallel", "arbitrary"),
    )(r, v, kk, w_f, k_f, b_f, r, v, kk, w_b, k_b, b_b)


def _rw_finish_kernel(of_ref, ob_ref, r_ref, kf_ref, kb_ref, v_ref, g_ref, gnw_ref, gnb_ref, rk_ref, o_ref):
    w = of_ref.shape[-1]
    mean_mat = _head_ones(w, RW_HD) * (1.0 / RW_HD)
    o = of_ref[...] + ob_ref[...]
    mean = jnp.dot(o, mean_mat, precision=HI, preferred_element_type=F32)
    cen = o - mean
    var = jnp.dot(cen * cen, mean_mat, precision=HI, preferred_element_type=F32)
    o = cen * lax.rsqrt(var + RW_GN_EPS) * gnw_ref[...] + gnb_ref[...]
    dot_rk = jnp.dot(r_ref[...] * (kf_ref[...] + kb_ref[...]) * rk_ref[...], _head_ones(w, RW_HD),
                     precision=HI, preferred_element_type=F32)
    o_ref[...] = ((o + dot_rk * v_ref[...]) * g_ref[...]).astype(o_ref.dtype)


def _rw_finish(o_f, o_b, r, k_f, k_b, v, g, gn_w, gn_b, r_k, tb):
    b, t, w = o_f.shape
    spec = pl.BlockSpec((None, tb, w), lambda bi, j: (bi, j, 0))
    cspec = pl.BlockSpec((1, w), lambda bi, j: (0, 0))
    return pl.pallas_call(
        _rw_finish_kernel,
        out_shape=jax.ShapeDtypeStruct((b, t, w), BF16),
        grid=(b, t // tb),
        in_specs=[spec] * 7 + [cspec] * 3,
        out_specs=spec,
        compiler_params=_cparams("parallel", "parallel"),
    )(o_f, o_b, r, k_f, k_b, v, g, gn_w, gn_b, r_k)


def _rms(x, gain, n):
    return x * lax.rsqrt(jnp.sum(x * x, axis=-1, keepdims=True) * (1.0 / n) + NORM_EPS) * gain


def _rope128(x, cos, sin):
    half = MLA_ROPE // 2
    lane = lax.broadcasted_iota(jnp.int32, x.shape, 1)
    swapped = jnp.where(lane < half, pltpu.roll(x, LANE - half, 1), pltpu.roll(x, half, 1))
    return x * cos + swapped * sin


def _mla_prep_kernel(p_ref, cos_ref, sin_ref, qn_ref, kvn_ref, wq_ref, wkv_ref, qng_ref, qrg_ref,
                     kng_ref, krg_ref, q_o, k_o, v_o):
    q_rank = qn_ref.shape[-1]
    kv_rank = kvn_ref.shape[-1]
    n_heads = v_o.shape[-1] // MLA_V
    slot = MLA_NOPE + LANE
    cos, sin = cos_ref[...], sin_ref[...]
    p = p_ref[...]
    q = _bdot(_rms(p[:, 0:q_rank], qn_ref[...], q_rank), wq_ref[...])
    kv = _bdot(_rms(p[:, q_rank:q_rank + kv_rank], kvn_ref[...], kv_rank), wkv_ref[...])
    k_rope = _rope128(_rms(p[:, q_rank + kv_rank:], krg_ref[...], MLA_ROPE), cos, sin).astype(k_o.dtype)
    for h in range(n_heads):
        q_nope = _rms(q[:, h * slot:h * slot + MLA_NOPE], qng_ref[...], MLA_NOPE)
        q_rope = _rope128(_rms(q[:, h * slot + MLA_NOPE:(h + 1) * slot], qrg_ref[...], MLA_ROPE), cos, sin)
        q_o[:, h * slot:h * slot + MLA_NOPE] = (q_nope * MLA_SCALE).astype(q_o.dtype)
        q_o[:, h * slot + MLA_NOPE:(h + 1) * slot] = (q_rope * MLA_SCALE).astype(q_o.dtype)
        k_nope = _rms(kv[:, h * MLA_NOPE:(h + 1) * MLA_NOPE], kng_ref[...], MLA_NOPE)
        k_o[:, h * slot:h * slot + MLA_NOPE] = k_nope.astype(k_o.dtype)
        k_o[:, h * slot + MLA_NOPE:(h + 1) * slot] = k_rope
    v_o[...] = kv[:, n_heads * MLA_NOPE:].astype(v_o.dtype)


def _mla_prep(p_ml, cos, sin, q_norm, kv_norm, w_uq, w_ukv, qn_g, qr_g, kn_g, kr_g, tb):
    b, t, n = p_ml.shape
    n_heads = w_uq.shape[1] // (MLA_NOPE + LANE)

    def const(a):
        return pl.BlockSpec(a.shape, lambda bi, j: (0,) * a.ndim)

    def out(width):
        return (jax.ShapeDtypeStruct((b, t, width), BF16), pl.BlockSpec((None, tb, width), lambda bi, j: (bi, j, 0)))

    outs = [out(n_heads * (MLA_NOPE + LANE)), out(n_heads * (MLA_NOPE + LANE)), out(n_heads * MLA_V)]
    return pl.pallas_call(
        _mla_prep_kernel,
        out_shape=tuple(o[0] for o in outs),
        grid=(b, t // tb),
        in_specs=[
            pl.BlockSpec((None, tb, n), lambda bi, j: (bi, j, 0)),
            pl.BlockSpec((tb, LANE), lambda bi, j: (j, 0)),
            pl.BlockSpec((tb, LANE), lambda bi, j: (j, 0)),
            const(q_norm), const(kv_norm), const(w_uq), const(w_ukv), const(qn_g), const(qr_g),
            const(kn_g), const(kr_g),
        ],
        out_specs=tuple(o[1] for o in outs),
        compiler_params=_cparams("parallel", "parallel"),
    )(p_ml, cos, sin, q_norm, kv_norm, w_uq, w_ukv, qn_g, qr_g, kn_g, kr_g)


def _mla_attn_kernel(q_ref, k_ref, v_ref, o_ref):
    tb = q_ref.shape[0]

    def attend(k, v):
        s = lax.dot_general(q_ref[...], k, (((1,), (1,)), ((), ())), preferred_element_type=F32)
        e = jnp.exp(s - jnp.max(s, axis=-1, keepdims=True))
        o = jnp.dot(e.astype(BF16), v, preferred_element_type=F32)
        o_ref[...] = (o / jnp.sum(e, axis=-1, keepdims=True)).astype(o_ref.dtype)

    @pl.when(pl.program_id(2) == 0)
    def _():
        attend(k_ref[0:tb, :], v_ref[0:tb, :])

    @pl.when(pl.program_id(2) > 0)
    def _():
        attend(k_ref[...], v_ref[...])


def _mla_attn(q, k, v, tb):
    b, t, _ = q.shape
    n_heads = v.shape[-1] // MLA_V
    slot = MLA_NOPE + LANE
    return pl.pallas_call(
        _mla_attn_kernel,
        out_shape=jax.ShapeDtypeStruct((b, t, n_heads * MLA_V), BF16),
        grid=(b, n_heads, t // tb),
        in_specs=[
            pl.BlockSpec((None, tb, slot), lambda bi, h, j: (bi, j, h)),
            pl.BlockSpec((None, t, slot), lambda bi, h, j: (bi, 0, h)),
            pl.BlockSpec((None, t, MLA_V), lambda bi, h, j: (bi, 0, h)),
        ],
        out_specs=pl.BlockSpec((None, tb, MLA_V), lambda bi, h, j: (bi, j, h)),
        compiler_params=_cparams("parallel", "parallel", "arbitrary"),
    )(q, k, v)


def _rope_tables(n_ctx, n_lat):
    rows = n_lat // GRID_W
    row = jnp.repeat(jnp.arange(rows, dtype=F32), GRID_W)
    col = jnp.tile(jnp.arange(GRID_W, dtype=F32), rows)
    axis_dim = MLA_ROPE // 2
    inv_freq = ROPE_THETA ** (-jnp.arange(0, axis_dim, 2, dtype=F32) / axis_dim)
    ang = jnp.concatenate([row[:, None] * inv_freq, col[:, None] * inv_freq], axis=-1)
    ang = jnp.concatenate([jnp.zeros((n_ctx, axis_dim), F32), ang], axis=0)
    cos, sin = jnp.cos(ang), jnp.sin(ang)
    return (_pad_cols(jnp.concatenate([cos, cos], axis=-1), LANE),
            _pad_cols(jnp.concatenate([-sin, sin], axis=-1), LANE))


def _pick(n, prefs):
    for p in prefs:
        if n % p == 0:
            return p
    return n


def kernel(x, c, ctx, c_ctx, norm_g, w_mod, b_mod, w_in, w_out, hg_lb, hg_gn, rw_mu, rw_w0, rw_w2, rw_a0, rw_a2, rw_g2, rw_kk, rw_ka, rw_rk, rw_gn_w, rw_gn_b, mla_q_norm, mla_w_uq, mla_kv_norm, mla_w_ukv, mla_qn_g, mla_qr_g, mla_kn_g, mla_kr_g, ffn_up, ffn_dw, ffn_db, ffn_down):
    bsz, n_lat, d = x.shape
    n_ctx = ctx.shape[1]
    depth = w_in.shape[0]
    hg_w = hg_lb.shape[-1]
    rw_w = rw_w0.shape[-1]
    q_rank = mla_q_norm.shape[-1]
    kv_rank = mla_kv_norm.shape[-1]
    n_heads = mla_w_ukv.shape[-1] // (MLA_NOPE + MLA_V)
    d_ff = ffn_down.shape[1]
    lora = (rw_w2.shape[2], rw_w2.shape[2], rw_a2.shape[2], rw_a2.shape[2], rw_g2.shape[1])
    tb = n_ctx
    nb = 2 if bsz % 2 == 0 else 1
    assert n_lat % tb == 0 and tb % HG_CHUNK == 0 and max(lora) <= LANE

    cc = jnp.concatenate([c, c_ctx[None, :]], axis=0)
    cc = jnp.pad(cc, ((0, -(bsz + 1) % SUBLANE), (0, 0)))
    mods = _mods(cc, w_mod, b_mod)
    mod_lat = mods[:, :bsz].reshape(depth, bsz, 6, d).transpose(0, 2, 1, 3)
    mod_ctx = jnp.broadcast_to(mods[:, bsz].reshape(depth, 6, 1, d), (depth, 6, bsz, d))
    modtab = jnp.stack([mod_ctx, mod_lat], axis=2)[:, :, :, :, None, :]

    lb_p = jax.nn.softmax(hg_lb.astype(F32), axis=0)
    lower = jnp.cumsum(lb_p, axis=0) - lb_p[0]

    hg_cols = 5 * hg_w
    rw_cols = 3 * rw_w + sum(lora)
    w_hg = w_in[:, :, :hg_cols].astype(BF16)
    w_rw_raw = w_in[:, :, hg_cols:hg_cols + rw_cols]
    w_ml = w_in[:, :, hg_cols + rw_cols:]
    offs = [3 * rw_w]
    for n in lora:
        offs.append(offs[-1] + n)

    def pad_lora(a):
        parts = [a[..., :3 * rw_w]] + [_pad_cols(a[..., offs[i]:offs[i + 1]], LANE) for i in range(5)]
        return jnp.concatenate(parts, axis=-1)

    w_rw = pad_lora(w_rw_raw).astype(BF16)
    mu = pad_lora(rw_mu)[:, None, :]
    w_ml = _pad_cols(w_ml, q_rank + kv_rank + LANE).astype(BF16)

    def pad_rows(a):
        return jnp.pad(a, ((0, 0),) * (a.ndim - 2) + ((0, LANE - a.shape[-2]), (0, 0)))

    w2 = pad_rows(rw_w2).astype(BF16)
    a2 = pad_rows(rw_a2).astype(BF16)
    g2 = pad_rows(rw_g2).astype(BF16)

    wq = mla_w_uq.reshape(depth, q_rank, n_heads, MLA_NOPE + MLA_ROPE)
    wq = _pad_cols(wq, MLA_NOPE + LANE).reshape(depth, q_rank, n_heads * (MLA_NOPE + LANE)).astype(BF16)
    wkv = mla_w_ukv.reshape(depth, kv_rank, n_heads, MLA_NOPE + MLA_V)
    wkv = jnp.concatenate([wkv[..., :MLA_NOPE].reshape(depth, kv_rank, -1),
                           wkv[..., MLA_NOPE:].reshape(depth, kv_rank, -1)], axis=-1).astype(BF16)
    cos, sin = _rope_tables(n_ctx, n_lat)

    w_out_b = w_out.astype(BF16)
    up_b = ffn_up.astype(BF16)
    down_b = ffn_down.astype(BF16)

    tn_hg = _pick(hg_cols, (512, 256, 128))
    tn_d = _pick(d, (512, 256, 128))
    tk = _pick(d_ff, (512, 256, 128))

    xs = jnp.concatenate([ctx, x], axis=1)
    for l in range(depth):
        gain0, gain1 = norm_g[l, 0:1], norm_g[l, 1:2]
        mt = modtab[l]
        p_hg = _in_proj(xs, gain0, mt, w_hg[l], tb, nb, tn_hg)
        p_rw = _in_proj(xs, gain0, mt, w_rw[l], tb, nb, w_rw.shape[-1])
        p_ml = _in_proj(xs, gain0, mt, w_ml[l], tb, nb, w_ml.shape[-1])

        hg_f, hg_b = _hg_scan(p_hg, lower[l], tb)
        hg_o = _hg_finish(hg_f, hg_b, p_hg, hg_gn[l][None, :], tb)

        r, v, kk, w_f, k_f, b_f, w_b, k_b, b_b, g = _rw_prep(
            p_rw, mu[l], rw_w0[l], w2[l], rw_a0[l], a2[l], g2[l], rw_kk[l][None, :], rw_ka[l][None, :], tb)
        rw_f, rw_b = _rw_scan(r, v, kk, w_f, k_f, b_f, w_b, k_b, b_b, tb)
        rw_o = _rw_finish(rw_f, rw_b, r, k_f, k_b, v, g, rw_gn_w[l][None, :], rw_gn_b[l][None, :],
                          rw_rk[l].reshape(1, rw_w), tb)

        q, k, vv = _mla_prep(p_ml, cos, sin, mla_q_norm[l][None, :], mla_kv_norm[l][None, :], wq[l], wkv[l],
                             mla_qn_g[l][None, :], _pad_cols(mla_qr_g[l][None, :], LANE),
                             mla_kn_g[l][None, :], _pad_cols(mla_kr_g[l][None, :], LANE), tb)
        ml_o = _mla_attn(q, k, vv, tb)

        xs = _out_proj(xs, hg_o, rw_o, ml_o, w_out_b[l], mt, tb, nb, tn_d)
        xs = _ffn(xs, gain1, mt, up_b[l], ffn_dw[l], ffn_db[l][None, :], down_b[l], tb, nb, tk)
    return xs[:, n_ctx:]
```

```python
import functools
import math

import jax
import jax.numpy as jnp
from jax import lax
from jax.experimental import pallas as pl
from jax.experimental.pallas import tpu as pltpu

F32 = jnp.float32
BF16 = jnp.bfloat16

NORM_EPS = 1e-6
GRID_W = 64
ROPE_THETA = 10000.0
HG_HD = 128
HG_CHUNK = 16
RW_HD = 64
RW_GN_EPS = 64e-5
RW_DECAY_MAX = math.exp(-0.5)
MLA_V = 128
MLA_NOPE = 128
MLA_ROPE = 64
MLA_SCALE = (MLA_NOPE + MLA_ROPE) ** -0.5
LANE = 128
SUBLANE = 8
RW_GROUP = 2
NORM_ROW_TILE = 16
FFN_ROW_TILE = 32
VMEM_LIMIT = 56 * 1024 * 1024
HI = lax.Precision.HIGHEST


def _cparams(*sem):
    return pltpu.CompilerParams(dimension_semantics=sem, vmem_limit_bytes=VMEM_LIMIT)


def _bdot(a, b):
    return jnp.dot(a.astype(BF16), b.astype(BF16), preferred_element_type=F32)


def _sigmoid(x):
    return 1.0 / (1.0 + jnp.exp(-x))


def _silu(x):
    return x * _sigmoid(x)


def _pad_cols(a, width):
    return jnp.pad(a, [(0, 0)] * (a.ndim - 1) + [(0, width - a.shape[-1])])


def _mods_kernel(c_ref, w_ref, b_ref, o_ref):
    o_ref[...] = _bdot(_silu(c_ref[...]), w_ref[...]) + b_ref[...]


def _mods(cc, w_mod, b_mod):
    n_layers, d, n = w_mod.shape
    rows = cc.shape[0]
    tn = 768 if n % 768 == 0 else n
    return pl.pallas_call(
        _mods_kernel,
        out_shape=jax.ShapeDtypeStruct((n_layers, rows, n), F32),
        grid=(n_layers, n // tn),
        in_specs=[
            pl.BlockSpec((rows, d), lambda l, j: (0, 0)),
            pl.BlockSpec((None, d, tn), lambda l, j: (l, 0, j)),
            pl.BlockSpec((None, 1, tn), lambda l, j: (l, 0, j)),
        ],
        out_specs=pl.BlockSpec((None, rows, tn), lambda l, j: (l, 0, j)),
        compiler_params=_cparams("parallel", "parallel"),
    )(cc, w_mod, b_mod.reshape(n_layers, 1, n))


def _norm_mod(x, gain, shift, scale):
    y = x * lax.rsqrt(jnp.mean(x * x, axis=-1, keepdims=True) + NORM_EPS) * gain
    return y * (1.0 + scale) + shift


def _norm_mod_rows(x_ref, h_ref, g_ref, sh_ref, sc_ref):
    nb, tb, _ = x_ref.shape
    rt = NORM_ROW_TILE
    for n in range(nb):
        mult = g_ref[...] * (1.0 + sc_ref[n])
        shift = sh_ref[n]

        def tile(i, carry, n=n, mult=mult, shift=shift):
            r = pl.multiple_of(i * rt, rt)
            x = x_ref[n, pl.ds(r, rt), :]
            rs = lax.rsqrt(jnp.mean(x * x, axis=-1, keepdims=True) + NORM_EPS)
            h = (x * rs * mult + shift).astype(h_ref.dtype)
            if len(h_ref.shape) == 3:
                h_ref[n, pl.ds(r, rt), :] = h
            else:
                h_ref[pl.ds(pl.multiple_of(n * tb + r, rt), rt), :] = h
            return carry

        lax.fori_loop(0, tb // rt, tile, 0, unroll=4)


def _mod_spec(m, nb, d, n_extra):
    if n_extra == 0:
        return pl.BlockSpec((None, None, nb, 1, d), lambda bi, j: (m, jnp.minimum(j, 1), bi, 0, 0))
    return pl.BlockSpec((None, None, nb, 1, d), lambda bi, j, k: (m, jnp.minimum(j, 1), bi, 0, 0))


def _norm_kernel(x_ref, g_ref, sh_ref, sc_ref, h_ref):
    _norm_mod_rows(x_ref, h_ref, g_ref, sh_ref, sc_ref)


def _norm(x, gain, modtab, tb, nb):
    b, t, d = x.shape
    return pl.pallas_call(
        _norm_kernel,
        out_shape=jax.ShapeDtypeStruct((b, t, d), BF16),
        grid=(b // nb, t // tb),
        in_specs=[
            pl.BlockSpec((nb, tb, d), lambda bi, j: (bi, j, 0)),
            pl.BlockSpec((1, d), lambda bi, j: (0, 0)),
            _mod_spec(0, nb, d, 0),
            _mod_spec(1, nb, d, 0),
        ],
        out_specs=pl.BlockSpec((nb, tb, d), lambda bi, j: (bi, j, 0)),
        compiler_params=_cparams("parallel", "parallel"),
    )(x, gain, modtab, modtab)


def _in_proj_kernel(h_ref, w_ref, o_ref):
    nb, tb, d = h_ref.shape
    o_ref[...] = jnp.dot(h_ref[...].reshape(nb * tb, d), w_ref[...],
                         preferred_element_type=F32).reshape(o_ref.shape)


def _in_proj(h, w, tb, nb, tn):
    b, t, d = h.shape
    n = w.shape[1]
    return pl.pallas_call(
        _in_proj_kernel,
        out_shape=jax.ShapeDtypeStruct((b, t, n), F32),
        grid=(b // nb, t // tb, n // tn),
        in_specs=[
            pl.BlockSpec((nb, tb, d), lambda bi, j, k: (bi, j, 0)),
            pl.BlockSpec((d, tn), lambda bi, j, k: (0, k)),
        ],
        out_specs=pl.BlockSpec((nb, tb, tn), lambda bi, j, k: (bi, j, k)),
        compiler_params=_cparams("parallel", "parallel", "arbitrary"),
    )(h, w)


def _out_proj_kernel(x_ref, hg_ref, rw_ref, ml_ref, w_ref, gate_ref, o_ref):
    nb, tb, tn = o_ref.shape
    n_hg, n_rw = hg_ref.shape[-1], rw_ref.shape[-1]
    rows = nb * tb
    acc = jnp.dot(hg_ref[...].reshape(rows, n_hg), w_ref[0:n_hg, :], preferred_element_type=F32)
    acc += jnp.dot(rw_ref[...].reshape(rows, n_rw), w_ref[n_hg:n_hg + n_rw, :], preferred_element_type=F32)
    acc += jnp.dot(ml_ref[...].reshape(rows, ml_ref.shape[-1]), w_ref[n_hg + n_rw:, :],
                   preferred_element_type=F32)
    o_ref[...] = x_ref[...] + gate_ref[...] * acc.reshape(nb, tb, tn)


def _out_proj(x, hg, rw, ml, w, modtab, tb, nb, tn):
    b, t, d = x.shape
    return pl.pallas_call(
        _out_proj_kernel,
        out_shape=jax.ShapeDtypeStruct((b, t, d), F32),
        grid=(b // nb, t // tb, d // tn),
        in_specs=[
            pl.BlockSpec((nb, tb, tn), lambda bi, j, k: (bi, j, k)),
            pl.BlockSpec((nb, tb, hg.shape[-1]), lambda bi, j, k: (bi, j, 0)),
            pl.BlockSpec((nb, tb, rw.shape[-1]), lambda bi, j, k: (bi, j, 0)),
            pl.BlockSpec((nb, tb, ml.shape[-1]), lambda bi, j, k: (bi, j, 0)),
            pl.BlockSpec((d, tn), lambda bi, j, k: (0, k)),
            pl.BlockSpec((None, None, nb, 1, tn), lambda bi, j, k: (2, jnp.minimum(j, 1), bi, 0, k)),
        ],
        out_specs=pl.BlockSpec((nb, tb, tn), lambda bi, j, k: (bi, j, k)),
        compiler_params=_cparams("parallel", "parallel", "arbitrary"),
    )(x, hg, rw, ml, w, modtab)


def _shift_rows(u, tb, first_rows, last_rows):
    rows = u.shape[0]
    row = lax.broadcasted_iota(jnp.int32, u.shape, 0)
    prev = pltpu.roll(u, 1, 0)
    nxt = pltpu.roll(u, rows - 1, 0)
    for n, (fr, lr) in enumerate(zip(first_rows, last_rows)):
        prev = jnp.where(row == n * tb, fr, prev)
        nxt = jnp.where(row == n * tb + tb - 1, lr, nxt)
    return prev, nxt


def _ffn_kernel(x_ref, xp_ref, xn_ref, g_ref, sh_ref, sc_ref, gate_ref, wa_ref, wb_ref, dwa_ref,
                dwb_ref, dba_ref, dbb_ref, wd_ref, o_ref, h_ref, ua_ref, ub_ref, act_ref):
    nb, tb, d = x_ref.shape
    rows = nb * tb
    halo = 2 * SUBLANE
    rt = FFN_ROW_TILE
    j = pl.program_id(1)
    k = pl.program_id(2)
    n_k = pl.num_programs(2)
    n_j = pl.num_programs(1)

    @pl.when(k == 0)
    def _():
        p_ok = (j >= 2).astype(F32)
        n_ok = jnp.logical_and(j >= 1, j < n_j - 1).astype(F32)
        g, sh, sc = g_ref[...], sh_ref[...], sc_ref[...]
        _norm_mod_rows(x_ref, h_ref, g_ref, sh_ref, sc_ref)
        hp = (_norm_mod(xp_ref[...], g, sh, sc) * p_ok).astype(BF16)
        hn = (_norm_mod(xn_ref[...], g, sh, sc) * n_ok).astype(BF16)
        for n in range(nb):
            h_ref[rows + n * halo:rows + n * halo + SUBLANE, :] = hp[n]
            h_ref[rows + n * halo + SUBLANE:rows + (n + 1) * halo, :] = hn[n]
        o_ref[...] = jnp.zeros_like(o_ref)

    ua_ref[...] = jnp.dot(h_ref[...], wa_ref[...], preferred_element_type=F32)
    ub_ref[...] = jnp.dot(h_ref[...], wb_ref[...], preferred_element_type=F32)

    trow = lax.broadcasted_iota(jnp.int32, (rt, ua_ref.shape[1]), 0)

    def conv(u_ref, dw, db, n, r):
        base = n * tb + r
        mid = u_ref[base:base + rt, :]
        if r == 0:
            edge = u_ref[rows + n * halo + SUBLANE - 1:rows + n * halo + SUBLANE, :]
            prev = jnp.where(trow == 0, edge, pltpu.roll(mid, 1, 0))
        else:
            prev = u_ref[base - 1:base - 1 + rt, :]
        if r == tb - rt:
            edge = u_ref[rows + n * halo + SUBLANE:rows + n * halo + SUBLANE + 1, :]
            nxt = jnp.where(trow == rt - 1, edge, pltpu.roll(mid, rt - 1, 0))
        else:
            nxt = u_ref[base + 1:base + 1 + rt, :]
        return prev * dw[0:1] + mid * dw[1:2] + nxt * dw[2:3] + db

    dwa, dwb, dba, dbb = dwa_ref[...], dwb_ref[...], dba_ref[...], dbb_ref[...]
    for n in range(nb):
        for r in range(0, tb, rt):
            a = conv(ua_ref, dwa, dba, n, r)
            b = conv(ub_ref, dwb, dbb, n, r)
            act_ref[n * tb + r:n * tb + r + rt, :] = (_silu(a) * b).astype(BF16)
    o_ref[...] += jnp.dot(act_ref[...], wd_ref[...], preferred_element_type=F32).reshape(nb, tb, d)

    @pl.when(k == n_k - 1)
    def _():
        o_ref[...] = x_ref[...] + gate_ref[...] * o_ref[...]


def _ffn(x, gain, modtab, w_up, dw, db, w_down, tb, nb, tk):
    b, t, d = x.shape
    d_ff = w_down.shape[0]
    n_k = d_ff // tk
    r8 = tb // SUBLANE
    n8 = t // SUBLANE

    def mspec(m):
        return pl.BlockSpec((None, None, nb, 1, d), lambda bi, j, k: (m, jnp.minimum(j, 1), bi, 0, 0))

    return pl.pallas_call(
        _ffn_kernel,
        out_shape=jax.ShapeDtypeStruct((b, t, d), F32),
        grid=(b // nb, t // tb, n_k),
        in_specs=[
            pl.BlockSpec((nb, tb, d), lambda bi, j, k: (bi, j, 0)),
            pl.BlockSpec((nb, SUBLANE, d), lambda bi, j, k: (bi, jnp.maximum(j * r8 - 1, 0), 0)),
            pl.BlockSpec((nb, SUBLANE, d), lambda bi, j, k: (bi, jnp.minimum((j + 1) * r8, n8 - 1), 0)),
            pl.BlockSpec((1, d), lambda bi, j, k: (0, 0)),
            mspec(3), mspec(4), mspec(5),
            pl.BlockSpec((d, tk), lambda bi, j, k: (0, k)),
            pl.BlockSpec((d, tk), lambda bi, j, k: (0, n_k + k)),
            pl.BlockSpec((3, tk), lambda bi, j, k: (0, k)),
            pl.BlockSpec((3, tk), lambda bi, j, k: (0, n_k + k)),
            pl.BlockSpec((1, tk), lambda bi, j, k: (0, k)),
            pl.BlockSpec((1, tk), lambda bi, j, k: (0, n_k + k)),
            pl.BlockSpec((tk, d), lambda bi, j, k: (k, 0)),
        ],
        out_specs=pl.BlockSpec((nb, tb, d), lambda bi, j, k: (bi, j, 0)),
        scratch_shapes=[pltpu.VMEM((nb * tb + nb * 2 * SUBLANE, d), BF16),
                        pltpu.VMEM((nb * tb + nb * 2 * SUBLANE, tk), F32),
                        pltpu.VMEM((nb * tb + nb * 2 * SUBLANE, tk), F32),
                        pltpu.VMEM((nb * tb, tk), BF16)],
        compiler_params=_cparams("parallel", "parallel", "arbitrary"),
    )(x, x, x, gain, modtab, modtab, modtab, w_up, w_up, dw, dw, db, db, w_down)


def _bwd_block(j, n_j):
    return jnp.where(j == 0, 0, n_j - j)


def _hg_scan_kernel(qf_ref, zf_ref, if_ref, qb_ref, zb_ref, ib_ref, lb_ref, of_ref, ob_ref,
                    st_ref, g_ref, k_ref):
    tb, w = zf_ref.shape
    n_heads = w // HG_HD
    n_chunks = tb // HG_CHUNK
    c = HG_CHUNK

    @pl.when(pl.program_id(1) == 0)
    def _():
        st_ref[...] = jnp.zeros_like(st_ref)

    ri = lax.broadcasted_iota(jnp.int32, (tb, tb), 0)
    ci = lax.broadcasted_iota(jnp.int32, (tb, tb), 1)
    same = (ri // c) == (ci // c)
    ones = jnp.ones((HG_HD, HG_HD), BF16)
    trow = lax.broadcasted_iota(jnp.int32, (c, HG_HD), 0)

    dirs = ((qf_ref, zf_ref, if_ref, of_ref), (qb_ref, zb_ref, ib_ref, ob_ref))
    for dr, (q_ref, z_ref, i_ref, o_ref) in enumerate(dirs):
        lb = lb_ref[dr:dr + 1, :]
        f = lb + (1.0 - lb) * _sigmoid(z_ref[...])
        tri = jnp.logical_and(same, (ci <= ri) if dr == 0 else (ci >= ri)).astype(F32)
        g_ref[dr] = jnp.dot(tri, jnp.log(f), precision=HI, preferred_element_type=F32)
        k_ref[dr] = 1.0 - f

    def chunk(ic, carry):
        for dr, (q_ref, z_ref, i_ref, o_ref) in enumerate(dirs):
            r0 = pl.multiple_of((ic if dr == 0 else n_chunks - 1 - ic) * c, c)
            heads = []
            tiles = []
            for h in range(n_heads):
                ls = slice(h * HG_HD, (h + 1) * HG_HD)
                q = q_ref[pl.ds(r0, c), ls]
                v = i_ref[pl.ds(r0, c), ls]
                g = g_ref[dr, pl.ds(r0, c), ls]
                kk = k_ref[dr, pl.ds(r0, c), ls]
                heads.append((ls, q, v, g, kk))
                for s in range(c):
                    seen = (trow >= s) if dr == 0 else (trow <= s)
                    tiles.append(jnp.where(seen, q * kk[s:s + 1] * jnp.exp(g - g[s:s + 1]), 0.0).astype(BF16))
            attn = jnp.dot(jnp.concatenate(tiles, axis=0), ones, preferred_element_type=F32)
            for h, (ls, q, v, g, kk) in enumerate(heads):
                o = jnp.zeros((c, HG_HD), F32)
                for s in range(c):
                    o += attn[(h * c + s) * c:(h * c + s + 1) * c] * v[s:s + 1]
                g_last = g[c - 1:c] if dr == 0 else g[0:1]
                st = st_ref[dr, h]
                o += lax.dot_general((q * jnp.exp(g)).astype(BF16), st.astype(BF16),
                                     (((1,), (1,)), ((), ())), preferred_element_type=F32)
                kd = kk * jnp.exp(g_last - g)
                st_ref[dr, h] = st * jnp.exp(g_last) + lax.dot_general(
                    v.astype(BF16), kd.astype(BF16), (((0,), (0,)), ((), ())), preferred_element_type=F32)
                o_ref[pl.ds(r0, c), ls] = o
        return carry

    lax.fori_loop(0, n_chunks, chunk, 0)


def _hg_scan(p_hg, lb, tb):
    b, t, _ = p_hg.shape
    w = lb.shape[-1]
    n_j = t // tb

    def fwd(col):
        return pl.BlockSpec((None, tb, w), lambda bi, j: (bi, j, col))

    def bwd(col):
        return pl.BlockSpec((None, tb, w), lambda bi, j: (bi, _bwd_block(j, n_j), col))

    out = jax.ShapeDtypeStruct((b, t, w), F32)
    return pl.pallas_call(
        _hg_scan_kernel,
        out_shape=(out, out),
        grid=(b, n_j),
        in_specs=[fwd(0), fwd(1), fwd(3), bwd(0), bwd(2), bwd(3), pl.BlockSpec((2, w), lambda bi, j: (0, 0))],
        out_specs=(fwd(0), bwd(0)),
        scratch_shapes=[pltpu.VMEM((2, w // HG_HD, HG_HD, HG_HD), F32), pltpu.VMEM((2, tb, w), F32),
                        pltpu.VMEM((2, tb, w), F32)],
        compiler_params=_cparams("parallel", "arbitrary"),
    )(p_hg, p_hg, p_hg, p_hg, p_hg, p_hg, lb)


def _hg_finish_kernel(of_ref, ob_ref, gate_ref, gn_ref, o_ref):
    w = of_ref.shape[-1]
    for h in range(w // HG_HD):
        ls = slice(h * HG_HD, (h + 1) * HG_HD)
        o = of_ref[:, ls] + ob_ref[:, ls]
        o = o * lax.rsqrt(jnp.mean(o * o, axis=-1, keepdims=True) + NORM_EPS) * gn_ref[...]
        o_ref[:, ls] = (o * _silu(gate_ref[:, ls])).astype(o_ref.dtype)


def _hg_finish(o_f, o_b, p_hg, gn, tb):
    b, t, w = o_f.shape
    spec = pl.BlockSpec((None, tb, w), lambda bi, j: (bi, j, 0))
    return pl.pallas_call(
        _hg_finish_kernel,
        out_shape=jax.ShapeDtypeStruct((b, t, w), BF16),
        grid=(b, t // tb),
        in_specs=[spec, spec, pl.BlockSpec((None, tb, w), lambda bi, j: (bi, j, 4)),
                  pl.BlockSpec((1, HG_HD), lambda bi, j: (0, 0))],
        out_specs=spec,
        compiler_params=_cparams("parallel", "parallel"),
    )(o_f, o_b, p_hg, gn)


def _head_ones(width, hd):
    r = lax.broadcasted_iota(jnp.int32, (width, width), 0) // hd
    c = lax.broadcasted_iota(jnp.int32, (width, width), 1) // hd
    return (r == c).astype(F32)


def _head_sum(x, hd):
    same_head = _head_ones(LANE, hd).astype(BF16)
    out = []
    for i in range(x.shape[-1] // LANE):
        xs = x[:, i * LANE:(i + 1) * LANE]
        hi = xs.astype(BF16)
        lo = (xs - hi.astype(F32)).astype(BF16)
        out.append(jnp.dot(hi, same_head, preferred_element_type=F32)
                   + jnp.dot(lo, same_head, preferred_element_type=F32))
    return jnp.concatenate(out, axis=-1)


def _rw_prep_kernel(p_ref, pp_ref, pn_ref, mu_ref, w0_ref, w2_ref, a0_ref, a2_ref, g2_ref, kk_ref,
                    ka_ref, r_o, v_o, kk_o, wf_o, kf_o, bf_o, wb_o, kb_o, bb_o, g_o):
    tb = p_ref.shape[0]
    w = r_o.shape[-1]
    lw = w2_ref.shape[1]
    j = pl.program_id(1)
    n_j = pl.num_programs(1)
    p = p_ref[...]
    p_ok = (j >= 2).astype(F32)
    n_ok = jnp.logical_and(j >= 1, j < n_j - 1).astype(F32)
    prev, nxt = _shift_rows(p, tb, [pp_ref[SUBLANE - 1:SUBLANE, :] * p_ok], [pn_ref[0:1, :] * n_ok])
    s = p + mu_ref[...] * (0.5 * (prev + nxt) - p)
    r, k, v = s[:, 0:w], s[:, w:2 * w], s[:, 2 * w:3 * w]
    lora = [s[:, 3 * w + i * lw:3 * w + (i + 1) * lw] for i in range(5)]
    kk = k * kk_ref[...]
    ssq = _head_sum(kk * kk, RW_HD)
    kk = kk / jnp.maximum(jnp.sqrt(ssq), 1e-12)
    r_o[...] = r
    v_o[...] = v
    kk_o[...] = kk
    for dr, (w_o, k_o, b_o) in enumerate(((wf_o, kf_o, bf_o), (wb_o, kb_o, bb_o))):
        xw, xa = lora[dr], lora[2 + dr]
        w_o[...] = jnp.exp(-RW_DECAY_MAX * _sigmoid(w0_ref[dr:dr + 1, :] + _bdot(jnp.tanh(xw), w2_ref[dr])))
        a = _sigmoid(a0_ref[dr:dr + 1, :] + _bdot(xa, a2_ref[dr]))
        k_o[...] = k * (1.0 + (a - 1.0) * ka_ref[...])
        b_o[...] = kk * a
    g_o[...] = _bdot(_sigmoid(lora[4]), g2_ref[...])


def _rw_prep(p_rw, mu, w0, w2, a0, a2, g2, k_k, k_a, tb):
    b, t, n = p_rw.shape
    w = w0.shape[-1]
    r8, n8 = tb // SUBLANE, t // SUBLANE
    out = jax.ShapeDtypeStruct((b, t, w), F32)
    ospec = pl.BlockSpec((None, tb, w), lambda bi, j: (bi, j, 0))

    def const(a):
        return pl.BlockSpec(a.shape, lambda bi, j: (0,) * a.ndim)

    return pl.pallas_call(
        _rw_prep_kernel,
        out_shape=(out,) * 10,
        grid=(b, t // tb),
        in_specs=[
            pl.BlockSpec((None, tb, n), lambda bi, j: (bi, j, 0)),
            pl.BlockSpec((None, SUBLANE, n), lambda bi, j: (bi, jnp.maximum(j * r8 - 1, 0), 0)),
            pl.BlockSpec((None, SUBLANE, n), lambda bi, j: (bi, jnp.minimum((j + 1) * r8, n8 - 1), 0)),
            const(mu), const(w0), const(w2), const(a0), const(a2), const(g2), const(k_k), const(k_a),
        ],
        out_specs=(ospec,) * 10,
        compiler_params=_cparams("parallel", "parallel"),
    )(p_rw, p_rw, p_rw, mu, w0, w2, a0, a2, g2, k_k, k_a)


def _rw_scan_kernel(rf_ref, vf_ref, kkf_ref, wf_ref, kf_ref, bf_ref, rb_ref, vb_ref, kkb_ref, wb_ref,
                    kb_ref, bb_ref, of_ref, ob_ref, st_ref):
    tb, w = rf_ref.shape
    n_pairs = w // LANE
    hd = RW_HD

    @pl.when(pl.program_id(1) == 0)
    def _():
        st_ref[...] = jnp.zeros_like(st_ref)

    same_head = _head_ones(LANE, hd).astype(BF16)
    gs = min(RW_GROUP, n_pairs)
    groups = [(dr, list(range(dr * n_pairs + g0, dr * n_pairs + min(g0 + gs, n_pairs))))
              for dr in range(2) for g0 in range(0, n_pairs, gs)]
    lane = lax.broadcasted_iota(jnp.int32, (hd, LANE), 1)
    sub = lax.broadcasted_iota(jnp.int32, (hd, LANE), 0)
    diag = (lane % hd) == sub
    dirs = ((rf_ref, vf_ref, kkf_ref, wf_ref, kf_ref, bf_ref, of_ref),
            (rb_ref, vb_ref, kkb_ref, wb_ref, kb_ref, bb_ref, ob_ref))

    n_chains = 2 * n_pairs
    sub8 = lax.broadcasted_iota(jnp.int32, (SUBLANE, LANE), 0)

    def group(ig, carry):
        tiles = []
        for dr, refs in enumerate(dirs):
            r0 = pl.multiple_of((ig if dr == 0 else tb // SUBLANE - 1 - ig) * SUBLANE, SUBLANE)
            for hp in range(n_pairs):
                ls = slice(hp * LANE, (hp + 1) * LANE)
                tiles.append((dr, r0, ls, [ref[pl.ds(r0, SUBLANE), ls] for ref in refs[:6]]))
        states = [st_ref[ci] for ci in range(n_chains)]
        out_tiles = [jnp.zeros((SUBLANE, LANE), F32) for _ in range(n_chains)]
        readout = [None] * n_chains

        def put_out(ci, s, red_o):
            o = jnp.sum(jnp.where(diag, red_o, 0.0), axis=0, keepdims=True)
            out_tiles[ci] = jnp.where(sub8 == s, o, out_tiles[ci])

        def row_of(dr, i):
            return i if dr == 0 else SUBLANE - 1 - i

        def head_sums(parts):
            pack = same_head.shape[0] // LANE
            n_part = len(parts[0])
            lhs = [jnp.concatenate([parts[c + u][k] for u in range(pack)], axis=1) if pack > 1 else parts[c][k]
                   for c in range(0, len(parts), pack) for k in range(n_part)]
            red = jnp.dot(jnp.concatenate(lhs, axis=0).astype(BF16), same_head, preferred_element_type=F32)
            return [[red[((c // pack) * n_part + k) * hd:((c // pack) * n_part + k + 1) * hd,
                         (c % pack) * LANE:(c % pack + 1) * LANE] for k in range(n_part)]
                    for c in range(len(parts))]

        for i in range(SUBLANE):
            for dr, group in groups:
                s = row_of(dr, i)
                parts = []
                for ci in group:
                    vals = tiles[ci][3]
                    parts.append([states[ci] * vals[2][s:s + 1], jnp.where(diag, vals[1][s:s + 1], 0.0)]
                                 + ([readout[ci]] if i > 0 else []))
                for ci, red in zip(group, head_sums(parts)):
                    rr, _vv, _kk, ww, kd, bb = [a[s:s + 1] for a in tiles[ci][3]]
                    if i > 0:
                        put_out(ci, row_of(dr, i - 1), red[2])
                    states[ci] = states[ci] * ww - red[0] * bb + red[1] * kd
                    readout[ci] = states[ci] * rr
        for dr, group in groups:
            for ci, red in zip(group, head_sums([[readout[ci]] for ci in group])):
                put_out(ci, row_of(dr, SUBLANE - 1), red[0])
        for ci, (dr, r0, ls, _vals) in enumerate(tiles):
            st_ref[ci] = states[ci]
            dirs[dr][6][pl.ds(r0, SUBLANE), ls] = out_tiles[ci]
        return carry

    lax.fori_loop(0, tb // SUBLANE, group, 0)


def _rw_scan(r, v, kk, w_f, k_f, b_f, w_b, k_b, b_b, tb):
    b, t, w = r.shape
    n_j = t // tb
    fwd = pl.BlockSpec((None, tb, w), lambda bi, j: (bi, j, 0))
    bwd = pl.BlockSpec((None, tb, w), lambda bi, j: (bi, _bwd_block(j, n_j), 0))
    out = jax.ShapeDtypeStruct((b, t, w), F32)
    return pl.pallas_call(
        _rw_scan_kernel,
        out_shape=(out, out),
        grid=(b, n_j),
        in_specs=[fwd] * 6 + [bwd] * 6,
        out_specs=(fwd, bwd),
        scratch_shapes=[pltpu.VMEM((2 * (w // LANE), RW_HD, LANE), F32)],
        compiler_params=_cparams("parallel", "arbitrary"),
    )(r, v, kk, w_f, k_f, b_f, r, v, kk, w_b, k_b, b_b)


def _rw_finish_kernel(of_ref, ob_ref, r_ref, kf_ref, kb_ref, v_ref, g_ref, gnw_ref, gnb_ref, rk_ref, o_ref):
    w = of_ref.shape[-1]
    o = of_ref[...] + ob_ref[...]
    cen = o - _head_sum(o, RW_HD) * (1.0 / RW_HD)
    var = _head_sum(cen * cen, RW_HD) * (1.0 / RW_HD)
    o = cen * lax.rsqrt(var + RW_GN_EPS) * gnw_ref[...] + gnb_ref[...]
    dot_rk = _head_sum(r_ref[...] * (kf_ref[...] + kb_ref[...]) * rk_ref[...], RW_HD)
    o_ref[...] = ((o + dot_rk * v_ref[...]) * g_ref[...]).astype(o_ref.dtype)


def _rw_finish(o_f, o_b, r, k_f, k_b, v, g, gn_w, gn_b, r_k, tb):
    b, t, w = o_f.shape
    spec = pl.BlockSpec((None, tb, w), lambda bi, j: (bi, j, 0))
    cspec = pl.BlockSpec((1, w), lambda bi, j: (0, 0))
    return pl.pallas_call(
        _rw_finish_kernel,
        out_shape=jax.ShapeDtypeStruct((b, t, w), BF16),
        grid=(b, t // tb),
        in_specs=[spec] * 7 + [cspec] * 3,
        out_specs=spec,
        compiler_params=_cparams("parallel", "parallel"),
    )(o_f, o_b, r, k_f, k_b, v, g, gn_w, gn_b, r_k)


def _rms(x, gain, n):
    return x * lax.rsqrt(jnp.sum(x * x, axis=-1, keepdims=True) * (1.0 / n) + NORM_EPS) * gain


def _rope128(x, cos, sin):
    half = MLA_ROPE // 2
    lane = lax.broadcasted_iota(jnp.int32, x.shape, 1)
    swapped = jnp.where(lane < half, pltpu.roll(x, LANE - half, 1), pltpu.roll(x, half, 1))
    return x * cos + swapped * sin


def _mla_prep_kernel(p_ref, cos_ref, sin_ref, qn_ref, kvn_ref, wq_ref, wkv_ref, qng_ref, qrg_ref,
                     kng_ref, krg_ref, q_o, k_o, v_o):
    q_rank = qn_ref.shape[-1]
    kv_rank = kvn_ref.shape[-1]
    n_heads = v_o.shape[-1] // MLA_V
    slot = MLA_NOPE + LANE
    cos, sin = cos_ref[...], sin_ref[...]
    p = p_ref[...]
    q = _bdot(_rms(p[:, 0:q_rank], qn_ref[...], q_rank), wq_ref[...])
    kv = _bdot(_rms(p[:, q_rank:q_rank + kv_rank], kvn_ref[...], kv_rank), wkv_ref[...])
    k_rope = _rope128(_rms(p[:, q_rank + kv_rank:], krg_ref[...], MLA_ROPE), cos, sin).astype(k_o.dtype)
    for h in range(n_heads):
        q_nope = _rms(q[:, h * slot:h * slot + MLA_NOPE], qng_ref[...], MLA_NOPE)
        q_rope = _rope128(_rms(q[:, h * slot + MLA_NOPE:(h + 1) * slot], qrg_ref[...], MLA_ROPE), cos, sin)
        q_o[:, h * slot:h * slot + MLA_NOPE] = (q_nope * MLA_SCALE).astype(q_o.dtype)
        q_o[:, h * slot + MLA_NOPE:(h + 1) * slot] = (q_rope * MLA_SCALE).astype(q_o.dtype)
        k_nope = _rms(kv[:, h * MLA_NOPE:(h + 1) * MLA_NOPE], kng_ref[...], MLA_NOPE)
        k_o[:, h * slot:h * slot + MLA_NOPE] = k_nope.astype(k_o.dtype)
        k_o[:, h * slot + MLA_NOPE:(h + 1) * slot] = k_rope
    v_o[...] = kv[:, n_heads * MLA_NOPE:].astype(v_o.dtype)


def _mla_prep(p_ml, cos, sin, q_norm, kv_norm, w_uq, w_ukv, qn_g, qr_g, kn_g, kr_g, tb):
    b, t, n = p_ml.shape
    n_heads = w_uq.shape[1] // (MLA_NOPE + LANE)

    def const(a):
        return pl.BlockSpec(a.shape, lambda bi, j: (0,) * a.ndim)

    def out(width):
        return (jax.ShapeDtypeStruct((b, t, width), BF16), pl.BlockSpec((None, tb, width), lambda bi, j: (bi, j, 0)))

    outs = [out(n_heads * (MLA_NOPE + LANE)), out(n_heads * (MLA_NOPE + LANE)), out(n_heads * MLA_V)]
    return pl.pallas_call(
        _mla_prep_kernel,
        out_shape=tuple(o[0] for o in outs),
        grid=(b, t // tb),
        in_specs=[
            pl.BlockSpec((None, tb, n), lambda bi, j: (bi, j, 0)),
            pl.BlockSpec((tb, LANE), lambda bi, j: (j, 0)),
            pl.BlockSpec((tb, LANE), lambda bi, j: (j, 0)),
            const(q_norm), const(kv_norm), const(w_uq), const(w_ukv), const(qn_g), const(qr_g),
            const(kn_g), const(kr_g),
        ],
        out_specs=tuple(o[1] for o in outs),
        compiler_params=_cparams("parallel", "parallel"),
    )(p_ml, cos, sin, q_norm, kv_norm, w_uq, w_ukv, qn_g, qr_g, kn_g, kr_g)


def _mla_attn_kernel(q_ref, k_ref, v_ref, o_ref):
    tb = q_ref.shape[0]

    def attend(k, v):
        s = lax.dot_general(q_ref[...], k, (((1,), (1,)), ((), ())), preferred_element_type=F32)
        e = jnp.exp(s - jnp.max(s, axis=-1, keepdims=True))
        o = jnp.dot(e.astype(BF16), v, preferred_element_type=F32)
        o_ref[...] = (o / jnp.sum(e, axis=-1, keepdims=True)).astype(o_ref.dtype)

    @pl.when(pl.program_id(2) == 0)
    def _():
        attend(k_ref[0:tb, :], v_ref[0:tb, :])

    @pl.when(pl.program_id(2) > 0)
    def _():
        attend(k_ref[...], v_ref[...])


def _mla_attn(q, k, v, tb):
    b, t, _ = q.shape
    n_heads = v.shape[-1] // MLA_V
    slot = MLA_NOPE + LANE
    return pl.pallas_call(
        _mla_attn_kernel,
        out_shape=jax.ShapeDtypeStruct((b, t, n_heads * MLA_V), BF16),
        grid=(b, n_heads, t // tb),
        in_specs=[
            pl.BlockSpec((None, tb, slot), lambda bi, h, j: (bi, j, h)),
            pl.BlockSpec((None, t, slot), lambda bi, h, j: (bi, 0, h)),
            pl.BlockSpec((None, t, MLA_V), lambda bi, h, j: (bi, 0, h)),
        ],
        out_specs=pl.BlockSpec((None, tb, MLA_V), lambda bi, h, j: (bi, j, h)),
        compiler_params=_cparams("parallel", "parallel", "arbitrary"),
    )(q, k, v)


def _rope_tables(n_ctx, n_lat):
    rows = n_lat // GRID_W
    row = jnp.repeat(jnp.arange(rows, dtype=F32), GRID_W)
    col = jnp.tile(jnp.arange(GRID_W, dtype=F32), rows)
    axis_dim = MLA_ROPE // 2
    inv_freq = ROPE_THETA ** (-jnp.arange(0, axis_dim, 2, dtype=F32) / axis_dim)
    ang = jnp.concatenate([row[:, None] * inv_freq, col[:, None] * inv_freq], axis=-1)
    ang = jnp.concatenate([jnp.zeros((n_ctx, axis_dim), F32), ang], axis=0)
    cos, sin = jnp.cos(ang), jnp.sin(ang)
    return (_pad_cols(jnp.concatenate([cos, cos], axis=-1), LANE),
            _pad_cols(jnp.concatenate([-sin, sin], axis=-1), LANE))


def _pick(n, prefs):
    for p in prefs:
        if n % p == 0:
            return p
    return n


def kernel(x, c, ctx, c_ctx, norm_g, w_mod, b_mod, w_in, w_out, hg_lb, hg_gn, rw_mu, rw_w0, rw_w2, rw_a0, rw_a2, rw_g2, rw_kk, rw_ka, rw_rk, rw_gn_w, rw_gn_b, mla_q_norm, mla_w_uq, mla_kv_norm, mla_w_ukv, mla_qn_g, mla_qr_g, mla_kn_g, mla_kr_g, ffn_up, ffn_dw, ffn_db, ffn_down):
    bsz, n_lat, d = x.shape
    n_ctx = ctx.shape[1]
    depth = w_in.shape[0]
    hg_w = hg_lb.shape[-1]
    rw_w = rw_w0.shape[-1]
    q_rank = mla_q_norm.shape[-1]
    kv_rank = mla_kv_norm.shape[-1]
    n_heads = mla_w_ukv.shape[-1] // (MLA_NOPE + MLA_V)
    d_ff = ffn_down.shape[1]
    lora = (rw_w2.shape[2], rw_w2.shape[2], rw_a2.shape[2], rw_a2.shape[2], rw_g2.shape[1])
    tb = n_ctx
    nb = 2 if bsz % 2 == 0 else 1
    nb_mm = 4 if bsz % 4 == 0 else nb
    assert n_lat % tb == 0 and tb % HG_CHUNK == 0 and max(lora) <= LANE

    cc = jnp.concatenate([c, c_ctx[None, :]], axis=0)
    cc = jnp.pad(cc, ((0, -(bsz + 1) % SUBLANE), (0, 0)))
    mods = _mods(cc, w_mod, b_mod)
    mod_lat = mods[:, :bsz].reshape(depth, bsz, 6, d).transpose(0, 2, 1, 3)
    mod_ctx = jnp.broadcast_to(mods[:, bsz].reshape(depth, 6, 1, d), (depth, 6, bsz, d))
    modtab = jnp.stack([mod_ctx, mod_lat], axis=2)[:, :, :, :, None, :]

    lb_p = jax.nn.softmax(hg_lb.astype(F32), axis=0)
    lower = jnp.cumsum(lb_p, axis=0) - lb_p[0]

    hg_cols = 5 * hg_w
    rw_cols = 3 * rw_w + sum(lora)
    w_hg = w_in[:, :, :hg_cols].astype(BF16)
    w_rw_raw = w_in[:, :, hg_cols:hg_cols + rw_cols]
    w_ml = w_in[:, :, hg_cols + rw_cols:]
    offs = [3 * rw_w]
    for n in lora:
        offs.append(offs[-1] + n)

    def pad_lora(a):
        parts = [a[..., :3 * rw_w]] + [_pad_cols(a[..., offs[i]:offs[i + 1]], LANE) for i in range(5)]
        return jnp.concatenate(parts, axis=-1)

    w_rw = pad_lora(w_rw_raw).astype(BF16)
    mu = pad_lora(rw_mu)[:, None, :]
    w_ml = _pad_cols(w_ml, q_rank + kv_rank + LANE).astype(BF16)

    def pad_rows(a):
        return jnp.pad(a, ((0, 0),) * (a.ndim - 2) + ((0, LANE - a.shape[-2]), (0, 0)))

    w2 = pad_rows(rw_w2).astype(BF16)
    a2 = pad_rows(rw_a2).astype(BF16)
    g2 = pad_rows(rw_g2).astype(BF16)

    wq = mla_w_uq.reshape(depth, q_rank, n_heads, MLA_NOPE + MLA_ROPE)
    wq = _pad_cols(wq, MLA_NOPE + LANE).reshape(depth, q_rank, n_heads * (MLA_NOPE + LANE)).astype(BF16)
    wkv = mla_w_ukv.reshape(depth, kv_rank, n_heads, MLA_NOPE + MLA_V)
    wkv = jnp.concatenate([wkv[..., :MLA_NOPE].reshape(depth, kv_rank, -1),
                           wkv[..., MLA_NOPE:].reshape(depth, kv_rank, -1)], axis=-1).astype(BF16)
    cos, sin = _rope_tables(n_ctx, n_lat)

    w_out_b = w_out.astype(BF16)
    up_b = ffn_up.astype(BF16)
    down_b = ffn_down.astype(BF16)

    tn_hg = _pick(hg_cols, (512, 256, 128))
    tn_d = _pick(d, (512, 256, 128))
    tk = _pick(d_ff, (512, 256, 128))

    xs = jnp.concatenate([ctx, x], axis=1)
    for l in range(depth):
        gain0, gain1 = norm_g[l, 0:1], norm_g[l, 1:2]
        mt = modtab[l]
        h = _norm(xs, gain0, mt, tb, nb)
        p_hg = _in_proj(h, w_hg[l], tb, nb_mm, tn_hg)
        p_rw = _in_proj(h, w_rw[l], tb, nb, w_rw.shape[-1])
        p_ml = _in_proj(h, w_ml[l], tb, nb_mm, w_ml.shape[-1])

        hg_f, hg_b = _hg_scan(p_hg, lower[l], tb)
        hg_o = _hg_finish(hg_f, hg_b, p_hg, hg_gn[l][None, :], tb)

        r, v, kk, w_f, k_f, b_f, w_b, k_b, b_b, g = _rw_prep(
            p_rw, mu[l], rw_w0[l], w2[l], rw_a0[l], a2[l], g2[l], rw_kk[l][None, :], rw_ka[l][None, :], tb)
        rw_f, rw_b = _rw_scan(r, v, kk, w_f, k_f, b_f, w_b, k_b, b_b, tb)
        rw_o = _rw_finish(rw_f, rw_b, r, k_f, k_b, v, g, rw_gn_w[l][None, :], rw_gn_b[l][None, :],
                          rw_rk[l].reshape(1, rw_w), tb)

        q, k, vv = _mla_prep(p_ml, cos, sin, mla_q_norm[l][None, :], mla_kv_norm[l][None, :], wq[l], wkv[l],
                             mla_qn_g[l][None, :], _pad_cols(mla_qr_g[l][None, :], LANE),
                             mla_kn_g[l][None, :], _pad_cols(mla_kr_g[l][None, :], LANE), tb)
        ml_o = _mla_attn(q, k, vv, tb)

        xs = _out_proj(xs, hg_o, rw_o, ml_o, w_out_b[l], mt, tb, nb, tn_d)
        xs = _ffn(xs, gain1, mt, up_b[l], ffn_dw[l], ffn_db[l][None, :], down_b[l], tb, nb, tk)
    return xs[:, n_ctx:]
```

```python
import functools
import math

import jax
import jax.numpy as jnp
from jax import lax
from jax.experimental import pallas as pl
from jax.experimental.pallas import tpu as pltpu

F32 = jnp.float32
BF16 = jnp.bfloat16

NORM_EPS = 1e-6
GRID_W = 64
ROPE_THETA = 10000.0
HG_HD = 128
HG_CHUNK = 16
RW_HD = 64
RW_GN_EPS = 64e-5
RW_DECAY_MAX = math.exp(-0.5)
MLA_V = 128
MLA_NOPE = 128
MLA_ROPE = 64
MLA_SCALE = (MLA_NOPE + MLA_ROPE) ** -0.5
LANE = 128
SUBLANE = 8
RW_CHUNK = 64
NORM_ROW_TILE = 16
FFN_ROW_TILE = 32
VMEM_LIMIT = 56 * 1024 * 1024
HI = lax.Precision.HIGHEST


def _cparams(*sem):
    return pltpu.CompilerParams(dimension_semantics=sem, vmem_limit_bytes=VMEM_LIMIT)


def _bdot(a, b):
    return jnp.dot(a.astype(BF16), b.astype(BF16), preferred_element_type=F32)


def _sigmoid(x):
    return 1.0 / (1.0 + jnp.exp(-x))


def _silu(x):
    return x * _sigmoid(x)


def _pad_cols(a, width):
    return jnp.pad(a, [(0, 0)] * (a.ndim - 1) + [(0, width - a.shape[-1])])


def _mods_kernel(c_ref, w_ref, b_ref, o_ref):
    o_ref[...] = _bdot(_silu(c_ref[...]), w_ref[...]) + b_ref[...]


def _mods(cc, w_mod, b_mod):
    n_layers, d, n = w_mod.shape
    rows = cc.shape[0]
    tn = 768 if n % 768 == 0 else n
    return pl.pallas_call(
        _mods_kernel,
        out_shape=jax.ShapeDtypeStruct((n_layers, rows, n), F32),
        grid=(n_layers, n // tn),
        in_specs=[
            pl.BlockSpec((rows, d), lambda l, j: (0, 0)),
            pl.BlockSpec((None, d, tn), lambda l, j: (l, 0, j)),
            pl.BlockSpec((None, 1, tn), lambda l, j: (l, 0, j)),
        ],
        out_specs=pl.BlockSpec((None, rows, tn), lambda l, j: (l, 0, j)),
        compiler_params=_cparams("parallel", "parallel"),
    )(cc, w_mod, b_mod.reshape(n_layers, 1, n))


def _norm_mod(x, gain, shift, scale):
    y = x * lax.rsqrt(jnp.mean(x * x, axis=-1, keepdims=True) + NORM_EPS) * gain
    return y * (1.0 + scale) + shift


def _norm_mod_rows(x_ref, h_ref, g_ref, sh_ref, sc_ref):
    nb, tb, _ = x_ref.shape
    rt = NORM_ROW_TILE
    for n in range(nb):
        mult = g_ref[...] * (1.0 + sc_ref[n])
        shift = sh_ref[n]

        def tile(i, carry, n=n, mult=mult, shift=shift):
            r = pl.multiple_of(i * rt, rt)
            x = x_ref[n, pl.ds(r, rt), :]
            rs = lax.rsqrt(jnp.mean(x * x, axis=-1, keepdims=True) + NORM_EPS)
            h = (x * rs * mult + shift).astype(h_ref.dtype)
            if len(h_ref.shape) == 3:
                h_ref[n, pl.ds(r, rt), :] = h
            else:
                h_ref[pl.ds(pl.multiple_of(n * tb + r, rt), rt), :] = h
            return carry

        lax.fori_loop(0, tb // rt, tile, 0, unroll=4)


def _mod_spec(m, nb, d):
    return pl.BlockSpec((None, None, nb, 1, d), lambda bi, j: (m, jnp.minimum(j, 1), bi, 0, 0))


def _norm_kernel(x_ref, g_ref, sh_ref, sc_ref, h_ref):
    _norm_mod_rows(x_ref, h_ref, g_ref, sh_ref, sc_ref)


def _norm(x, gain, modtab, tb, nb):
    b, t, d = x.shape
    return pl.pallas_call(
        _norm_kernel,
        out_shape=jax.ShapeDtypeStruct((b, t, d), BF16),
        grid=(b // nb, t // tb),
        in_specs=[
            pl.BlockSpec((nb, tb, d), lambda bi, j: (bi, j, 0)),
            pl.BlockSpec((1, d), lambda bi, j: (0, 0)),
            _mod_spec(0, nb, d),
            _mod_spec(1, nb, d),
        ],
        out_specs=pl.BlockSpec((nb, tb, d), lambda bi, j: (bi, j, 0)),
        compiler_params=_cparams("parallel", "parallel"),
    )(x, gain, modtab, modtab)


def _in_proj_kernel(h_ref, w_ref, o_ref):
    nb, tb, d = h_ref.shape
    o_ref[...] = jnp.dot(h_ref[...].reshape(nb * tb, d), w_ref[...],
                         preferred_element_type=F32).reshape(o_ref.shape)


def _in_proj(h, w, tb, nb, tn):
    b, t, d = h.shape
    n = w.shape[1]
    return pl.pallas_call(
        _in_proj_kernel,
        out_shape=jax.ShapeDtypeStruct((b, t, n), F32),
        grid=(b // nb, t // tb, n // tn),
        in_specs=[
            pl.BlockSpec((nb, tb, d), lambda bi, j, k: (bi, j, 0)),
            pl.BlockSpec((d, tn), lambda bi, j, k: (0, k)),
        ],
        out_specs=pl.BlockSpec((nb, tb, tn), lambda bi, j, k: (bi, j, k)),
        compiler_params=_cparams("parallel", "parallel", "arbitrary"),
    )(h, w)


def _out_proj_kernel(x_ref, hg_ref, rw_ref, ml_ref, w_ref, gate_ref, o_ref):
    nb, tb, tn = o_ref.shape
    n_hg, n_rw = hg_ref.shape[-1], rw_ref.shape[-1]
    rows = nb * tb
    acc = jnp.dot(hg_ref[...].reshape(rows, n_hg), w_ref[0:n_hg, :], preferred_element_type=F32)
    acc += jnp.dot(rw_ref[...].reshape(rows, n_rw), w_ref[n_hg:n_hg + n_rw, :], preferred_element_type=F32)
    acc += jnp.dot(ml_ref[...].reshape(rows, ml_ref.shape[-1]), w_ref[n_hg + n_rw:, :],
                   preferred_element_type=F32)
    o_ref[...] = x_ref[...] + gate_ref[...] * acc.reshape(nb, tb, tn)


def _out_proj(x, hg, rw, ml, w, modtab, tb, nb, tn):
    b, t, d = x.shape
    return pl.pallas_call(
        _out_proj_kernel,
        out_shape=jax.ShapeDtypeStruct((b, t, d), F32),
        grid=(b // nb, t // tb, d // tn),
        in_specs=[
            pl.BlockSpec((nb, tb, tn), lambda bi, j, k: (bi, j, k)),
            pl.BlockSpec((nb, tb, hg.shape[-1]), lambda bi, j, k: (bi, j, 0)),
            pl.BlockSpec((nb, tb, rw.shape[-1]), lambda bi, j, k: (bi, j, 0)),
            pl.BlockSpec((nb, tb, ml.shape[-1]), lambda bi, j, k: (bi, j, 0)),
            pl.BlockSpec((d, tn), lambda bi, j, k: (0, k)),
            pl.BlockSpec((None, None, nb, 1, tn), lambda bi, j, k: (2, jnp.minimum(j, 1), bi, 0, k)),
        ],
        out_specs=pl.BlockSpec((nb, tb, tn), lambda bi, j, k: (bi, j, k)),
        compiler_params=_cparams("parallel", "parallel", "arbitrary"),
    )(x, hg, rw, ml, w, modtab)


def _shift_rows(u, tb, first_rows, last_rows):
    rows = u.shape[0]
    row = lax.broadcasted_iota(jnp.int32, u.shape, 0)
    prev = pltpu.roll(u, 1, 0)
    nxt = pltpu.roll(u, rows - 1, 0)
    for n, (fr, lr) in enumerate(zip(first_rows, last_rows)):
        prev = jnp.where(row == n * tb, fr, prev)
        nxt = jnp.where(row == n * tb + tb - 1, lr, nxt)
    return prev, nxt


def _ffn_kernel(x_ref, xp_ref, xn_ref, g_ref, sh_ref, sc_ref, gate_ref, wa_ref, wb_ref, dwa_ref,
                dwb_ref, dba_ref, dbb_ref, wd_ref, o_ref, h_ref, ua_ref, ub_ref, act_ref):
    nb, tb, d = x_ref.shape
    rows = nb * tb
    halo = 2 * SUBLANE
    rt = FFN_ROW_TILE
    j = pl.program_id(1)
    k = pl.program_id(2)
    n_k = pl.num_programs(2)
    n_j = pl.num_programs(1)

    @pl.when(k == 0)
    def _():
        p_ok = (j >= 2).astype(F32)
        n_ok = jnp.logical_and(j >= 1, j < n_j - 1).astype(F32)
        g, sh, sc = g_ref[...], sh_ref[...], sc_ref[...]
        _norm_mod_rows(x_ref, h_ref, g_ref, sh_ref, sc_ref)
        hp = (_norm_mod(xp_ref[...], g, sh, sc) * p_ok).astype(BF16)
        hn = (_norm_mod(xn_ref[...], g, sh, sc) * n_ok).astype(BF16)
        for n in range(nb):
            h_ref[rows + n * halo:rows + n * halo + SUBLANE, :] = hp[n]
            h_ref[rows + n * halo + SUBLANE:rows + (n + 1) * halo, :] = hn[n]
        o_ref[...] = jnp.zeros_like(o_ref)

    ua_ref[...] = jnp.dot(h_ref[...], wa_ref[...], preferred_element_type=F32)
    ub_ref[...] = jnp.dot(h_ref[...], wb_ref[...], preferred_element_type=F32)

    trow = lax.broadcasted_iota(jnp.int32, (rt, ua_ref.shape[1]), 0)

    def conv(u_ref, dw, db, n, r):
        base = n * tb + r
        mid = u_ref[base:base + rt, :]
        if r == 0:
            edge = u_ref[rows + n * halo + SUBLANE - 1:rows + n * halo + SUBLANE, :]
            prev = jnp.where(trow == 0, edge, pltpu.roll(mid, 1, 0))
        else:
            prev = u_ref[base - 1:base - 1 + rt, :]
        if r == tb - rt:
            edge = u_ref[rows + n * halo + SUBLANE:rows + n * halo + SUBLANE + 1, :]
            nxt = jnp.where(trow == rt - 1, edge, pltpu.roll(mid, rt - 1, 0))
        else:
            nxt = u_ref[base + 1:base + 1 + rt, :]
        return prev * dw[0:1] + mid * dw[1:2] + nxt * dw[2:3] + db

    dwa, dwb, dba, dbb = dwa_ref[...], dwb_ref[...], dba_ref[...], dbb_ref[...]
    for n in range(nb):
        for r in range(0, tb, rt):
            a = conv(ua_ref, dwa, dba, n, r)
            b = conv(ub_ref, dwb, dbb, n, r)
            act_ref[n * tb + r:n * tb + r + rt, :] = (_silu(a) * b).astype(BF16)
    o_ref[...] += jnp.dot(act_ref[...], wd_ref[...], preferred_element_type=F32).reshape(nb, tb, d)

    @pl.when(k == n_k - 1)
    def _():
        o_ref[...] = x_ref[...] + gate_ref[...] * o_ref[...]


def _ffn(x, gain, modtab, w_up, dw, db, w_down, tb, nb, tk):
    b, t, d = x.shape
    d_ff = w_down.shape[0]
    n_k = d_ff // tk
    r8 = tb // SUBLANE
    n8 = t // SUBLANE

    def mspec(m):
        return pl.BlockSpec((None, None, nb, 1, d), lambda bi, j, k: (m, jnp.minimum(j, 1), bi, 0, 0))

    return pl.pallas_call(
        _ffn_kernel,
        out_shape=jax.ShapeDtypeStruct((b, t, d), F32),
        grid=(b // nb, t // tb, n_k),
        in_specs=[
            pl.BlockSpec((nb, tb, d), lambda bi, j, k: (bi, j, 0)),
            pl.BlockSpec((nb, SUBLANE, d), lambda bi, j, k: (bi, jnp.maximum(j * r8 - 1, 0), 0)),
            pl.BlockSpec((nb, SUBLANE, d), lambda bi, j, k: (bi, jnp.minimum((j + 1) * r8, n8 - 1), 0)),
            pl.BlockSpec((1, d), lambda bi, j, k: (0, 0)),
            mspec(3), mspec(4), mspec(5),
            pl.BlockSpec((d, tk), lambda bi, j, k: (0, k)),
            pl.BlockSpec((d, tk), lambda bi, j, k: (0, n_k + k)),
            pl.BlockSpec((3, tk), lambda bi, j, k: (0, k)),
            pl.BlockSpec((3, tk), lambda bi, j, k: (0, n_k + k)),
            pl.BlockSpec((1, tk), lambda bi, j, k: (0, k)),
            pl.BlockSpec((1, tk), lambda bi, j, k: (0, n_k + k)),
            pl.BlockSpec((tk, d), lambda bi, j, k: (k, 0)),
        ],
        out_specs=pl.BlockSpec((nb, tb, d), lambda bi, j, k: (bi, j, 0)),
        scratch_shapes=[pltpu.VMEM((nb * tb + nb * 2 * SUBLANE, d), BF16),
                        pltpu.VMEM((nb * tb + nb * 2 * SUBLANE, tk), F32),
                        pltpu.VMEM((nb * tb + nb * 2 * SUBLANE, tk), F32),
                        pltpu.VMEM((nb * tb, tk), BF16)],
        compiler_params=_cparams("parallel", "parallel", "arbitrary"),
    )(x, x, x, gain, modtab, modtab, modtab, w_up, w_up, dw, dw, db, db, w_down)


def _chunk_cumsum(x, c, reverse):
    rows = x.shape[0]
    ri = lax.broadcasted_iota(jnp.int32, (rows, rows), 0)
    ci = lax.broadcasted_iota(jnp.int32, (rows, rows), 1)
    tri = jnp.logical_and((ri // c) == (ci // c), (ci >= ri) if reverse else (ci <= ri)).astype(BF16)
    hi = x.astype(BF16)
    rest = x - hi.astype(F32)
    mid = rest.astype(BF16)
    lo = (rest - mid.astype(F32)).astype(BF16)
    return (jnp.dot(tri, hi, preferred_element_type=F32) + jnp.dot(tri, mid, preferred_element_type=F32)
            + jnp.dot(tri, lo, preferred_element_type=F32))


def _bwd_block(j, n_j):
    return jnp.where(j == 0, 0, n_j - j)


def _hg_scan_kernel(qf_ref, zf_ref, if_ref, qb_ref, zb_ref, ib_ref, lb_ref, of_ref, ob_ref,
                    st_ref, g_ref, k_ref):
    tb, w = zf_ref.shape
    n_heads = w // HG_HD
    n_chunks = tb // HG_CHUNK
    c = HG_CHUNK

    @pl.when(pl.program_id(1) == 0)
    def _():
        st_ref[...] = jnp.zeros_like(st_ref)

    ones = jnp.ones((HG_HD, HG_HD), BF16)
    trow = lax.broadcasted_iota(jnp.int32, (c, HG_HD), 0)

    dirs = ((qf_ref, zf_ref, if_ref, of_ref), (qb_ref, zb_ref, ib_ref, ob_ref))
    for dr, (q_ref, z_ref, i_ref, o_ref) in enumerate(dirs):
        lb = lb_ref[dr:dr + 1, :]
        f = lb + (1.0 - lb) * _sigmoid(z_ref[...])
        g_ref[dr] = _chunk_cumsum(jnp.log(f), c, dr == 1)
        k_ref[dr] = 1.0 - f

    def dir_chunk(dr, r0):
        q_ref, _z_ref, i_ref, o_ref = dirs[dr]
        heads = []
        tiles = []
        for h in range(n_heads):
            ls = slice(h * HG_HD, (h + 1) * HG_HD)
            q = q_ref[pl.ds(r0, c), ls]
            v = i_ref[pl.ds(r0, c), ls]
            g = g_ref[dr, pl.ds(r0, c), ls]
            kk = k_ref[dr, pl.ds(r0, c), ls]
            heads.append((ls, q, v, g, kk))
            for s in range(c):
                seen = (trow >= s) if dr == 0 else (trow <= s)
                tiles.append(jnp.where(seen, q * kk[s:s + 1] * jnp.exp(g - g[s:s + 1]), 0.0).astype(BF16))
        attn = jnp.dot(jnp.concatenate(tiles, axis=0), ones, preferred_element_type=F32)
        inter = []
        for h, (ls, q, v, g, kk) in enumerate(heads):
            g_last = g[c - 1:c] if dr == 0 else g[0:1]
            st = st_ref[dr, h]
            inter.append(lax.dot_general((q * jnp.exp(g)).astype(BF16), st.astype(BF16),
                                         (((1,), (1,)), ((), ())), preferred_element_type=F32))
            kd = kk * jnp.exp(g_last - g)
            st_ref[dr, h] = st * jnp.exp(g_last) + lax.dot_general(
                v.astype(BF16), kd.astype(BF16), (((0,), (0,)), ((), ())), preferred_element_type=F32)
        yield
        for h, (ls, q, v, g, kk) in enumerate(heads):
            o = inter[h]
            for s in range(c):
                o += attn[(h * c + s) * c:(h * c + s + 1) * c] * v[s:s + 1]
            o_ref[pl.ds(r0, c), ls] = o

    def chunk(ic, carry):
        live = [dir_chunk(dr, pl.multiple_of((ic if dr == 0 else n_chunks - 1 - ic) * c, c)) for dr in range(2)]
        while live:
            live = [gen for gen in live if next(gen, True) is None]
        return carry

    lax.fori_loop(0, n_chunks, chunk, 0)


def _hg_scan(p_hg, lb, tb):
    b, t, _ = p_hg.shape
    w = lb.shape[-1]
    n_j = t // tb

    def fwd(col):
        return pl.BlockSpec((None, tb, w), lambda bi, j: (bi, j, col))

    def bwd(col):
        return pl.BlockSpec((None, tb, w), lambda bi, j: (bi, _bwd_block(j, n_j), col))

    out = jax.ShapeDtypeStruct((b, t, w), F32)
    return pl.pallas_call(
        _hg_scan_kernel,
        out_shape=(out, out),
        grid=(b, n_j),
        in_specs=[fwd(0), fwd(1), fwd(3), bwd(0), bwd(2), bwd(3), pl.BlockSpec((2, w), lambda bi, j: (0, 0))],
        out_specs=(fwd(0), bwd(0)),
        scratch_shapes=[pltpu.VMEM((2, w // HG_HD, HG_HD, HG_HD), F32), pltpu.VMEM((2, tb, w), F32),
                        pltpu.VMEM((2, tb, w), F32)],
        compiler_params=_cparams("parallel", "arbitrary"),
    )(p_hg, p_hg, p_hg, p_hg, p_hg, p_hg, lb)


def _hg_finish_kernel(of_ref, ob_ref, gate_ref, gn_ref, o_ref):
    w = of_ref.shape[-1]
    for h in range(w // HG_HD):
        ls = slice(h * HG_HD, (h + 1) * HG_HD)
        o = of_ref[:, ls] + ob_ref[:, ls]
        o = o * lax.rsqrt(jnp.mean(o * o, axis=-1, keepdims=True) + NORM_EPS) * gn_ref[...]
        o_ref[:, ls] = (o * _silu(gate_ref[:, ls])).astype(o_ref.dtype)


def _hg_finish(o_f, o_b, p_hg, gn, tb):
    b, t, w = o_f.shape
    spec = pl.BlockSpec((None, tb, w), lambda bi, j: (bi, j, 0))
    return pl.pallas_call(
        _hg_finish_kernel,
        out_shape=jax.ShapeDtypeStruct((b, t, w), BF16),
        grid=(b, t // tb),
        in_specs=[spec, spec, pl.BlockSpec((None, tb, w), lambda bi, j: (bi, j, 4)),
                  pl.BlockSpec((1, HG_HD), lambda bi, j: (0, 0))],
        out_specs=spec,
        compiler_params=_cparams("parallel", "parallel"),
    )(o_f, o_b, p_hg, gn)


def _head_ones(width, hd):
    r = lax.broadcasted_iota(jnp.int32, (width, width), 0) // hd
    c = lax.broadcasted_iota(jnp.int32, (width, width), 1) // hd
    return (r == c).astype(F32)


def _head_sum(x, hd):
    same_head = _head_ones(LANE, hd).astype(BF16)
    out = []
    for i in range(x.shape[-1] // LANE):
        xs = x[:, i * LANE:(i + 1) * LANE]
        hi = xs.astype(BF16)
        lo = (xs - hi.astype(F32)).astype(BF16)
        out.append(jnp.dot(hi, same_head, preferred_element_type=F32)
                   + jnp.dot(lo, same_head, preferred_element_type=F32))
    return jnp.concatenate(out, axis=-1)


def _rw_prep_kernel(p_ref, pp_ref, pn_ref, mu_ref, w0_ref, w2_ref, a0_ref, a2_ref, g2_ref, kk_ref,
                    ka_ref, r_o, v_o, kk_o, wf_o, kf_o, bf_o, wb_o, kb_o, bb_o, g_o):
    tb = p_ref.shape[0]
    w = r_o.shape[-1]
    lw = w2_ref.shape[1]
    j = pl.program_id(1)
    n_j = pl.num_programs(1)
    p = p_ref[...]
    p_ok = (j >= 2).astype(F32)
    n_ok = jnp.logical_and(j >= 1, j < n_j - 1).astype(F32)
    prev, nxt = _shift_rows(p, tb, [pp_ref[SUBLANE - 1:SUBLANE, :] * p_ok], [pn_ref[0:1, :] * n_ok])
    s = p + mu_ref[...] * (0.5 * (prev + nxt) - p)
    r, k, v = s[:, 0:w], s[:, w:2 * w], s[:, 2 * w:3 * w]
    lora = [s[:, 3 * w + i * lw:3 * w + (i + 1) * lw] for i in range(5)]
    kk = k * kk_ref[...]
    ssq = _head_sum(kk * kk, RW_HD)
    kk = kk / jnp.maximum(jnp.sqrt(ssq), 1e-12)
    r_o[...] = r
    v_o[...] = v
    kk_o[...] = kk
    for dr, (w_o, k_o, b_o) in enumerate(((wf_o, kf_o, bf_o), (wb_o, kb_o, bb_o))):
        xw, xa = lora[dr], lora[2 + dr]
        w_o[...] = -RW_DECAY_MAX * _sigmoid(w0_ref[dr:dr + 1, :] + _bdot(jnp.tanh(xw), w2_ref[dr]))
        a = _sigmoid(a0_ref[dr:dr + 1, :] + _bdot(xa, a2_ref[dr]))
        k_o[...] = k * (1.0 + (a - 1.0) * ka_ref[...])
        b_o[...] = kk * a
    g_o[...] = _bdot(_sigmoid(lora[4]), g2_ref[...])


def _rw_prep(p_rw, mu, w0, w2, a0, a2, g2, k_k, k_a, tb):
    b, t, n = p_rw.shape
    w = w0.shape[-1]
    r8, n8 = tb // SUBLANE, t // SUBLANE
    out = jax.ShapeDtypeStruct((b, t, w), F32)
    ospec = pl.BlockSpec((None, tb, w), lambda bi, j: (bi, j, 0))

    def const(a):
        return pl.BlockSpec(a.shape, lambda bi, j: (0,) * a.ndim)

    return pl.pallas_call(
        _rw_prep_kernel,
        out_shape=(out,) * 10,
        grid=(b, t // tb),
        in_specs=[
            pl.BlockSpec((None, tb, n), lambda bi, j: (bi, j, 0)),
            pl.BlockSpec((None, SUBLANE, n), lambda bi, j: (bi, jnp.maximum(j * r8 - 1, 0), 0)),
            pl.BlockSpec((None, SUBLANE, n), lambda bi, j: (bi, jnp.minimum((j + 1) * r8, n8 - 1), 0)),
            const(mu), const(w0), const(w2), const(a0), const(a2), const(g2), const(k_k), const(k_a),
        ],
        out_specs=(ospec,) * 10,
        compiler_params=_cparams("parallel", "parallel"),
    )(p_rw, p_rw, p_rw, mu, w0, w2, a0, a2, g2, k_k, k_a)


def _rw_scan_kernel(rf_ref, vf_ref, kkf_ref, wf_ref, kf_ref, bf_ref, rb_ref, vb_ref, kkb_ref, wb_ref,
                    kb_ref, bb_ref, of_ref, ob_ref, st_ref, g_ref):
    tb, w = rf_ref.shape
    n_pairs = w // LANE
    hd = RW_HD
    c = RW_CHUNK
    n_chunks = tb // c
    assert c == hd and tb % c == 0

    @pl.when(pl.program_id(1) == 0)
    def _():
        st_ref[...] = jnp.zeros_like(st_ref)

    dirs = ((rf_ref, vf_ref, kkf_ref, wf_ref, kf_ref, bf_ref, of_ref),
            (rb_ref, vb_ref, kkb_ref, wb_ref, kb_ref, bb_ref, ob_ref))
    for dr, refs in enumerate(dirs):
        g_ref[dr] = _chunk_cumsum(refs[3][...], c, dr == 1)

    row = lax.broadcasted_iota(jnp.int32, (c, LANE), 0)
    col = lax.broadcasted_iota(jnp.int32, (c, LANE), 1) % hd
    head_a = lax.broadcasted_iota(jnp.int32, (c, LANE), 1) < hd
    head_a2 = (lax.broadcasted_iota(jnp.int32, (c, 2 * LANE), 1) % LANE) < hd
    r2 = lax.broadcasted_iota(jnp.int32, (LANE, LANE), 0)
    c2 = lax.broadcasted_iota(jnp.int32, (LANE, LANE), 1)
    same_head = (r2 // hd) == (c2 // hd)
    eye = r2 == c2

    def stack2(y):
        m = head_a if y.shape[1] == LANE else head_a2
        return jnp.concatenate([jnp.where(m, y, 0.0), jnp.where(m, 0.0, y)], axis=0).astype(BF16)

    def tn(x, y):
        return lax.dot_general(x.astype(BF16), y.astype(BF16), (((0,), (0,)), ((), ())),
                               preferred_element_type=F32)

    def pair_chunk(dr, hp, r0):
        refs = dirs[dr]
        before = (col < row) if dr == 0 else (col > row)
        upto = (col <= row) if dr == 0 else (col >= row)
        ls = slice(hp * LANE, (hp + 1) * LANE)
        rr, vv, kk, lw, kd, bb = [ref[pl.ds(r0, c), ls] for ref in refs[:6]]
        g = g_ref[dr, pl.ds(r0, c), ls]
        g_end = g[c - 1:c] if dr == 0 else g[0:1]
        at = -kk * jnp.exp(g - lw)
        rt = rr * jnp.exp(g)
        e_inv = jnp.exp(-g)
        e_out = jnp.exp(g_end - g)
        bh_kh = jnp.concatenate([bb * e_out, kd * e_out], axis=0)
        abk = lax.dot_general(jnp.concatenate([at, rt], axis=0).astype(BF16),
                              jnp.concatenate([stack2(bb * e_inv), stack2(kd * e_inv)], axis=0),
                              (((1,), (1,)), ((), ())), preferred_element_type=F32)
        yield
        x = jnp.where(before, abk[0:c, 0:LANE], 0.0)
        a_ak = jnp.where(before, abk[0:c, LANE:], 0.0)
        a_r = jnp.concatenate([jnp.where(upto, abk[c:, 0:LANE], 0.0), jnp.where(upto, abk[c:, LANE:], 0.0)], axis=1)
        u0 = jnp.dot(a_ak.astype(BF16), stack2(vv), preferred_element_type=F32)
        yield
        y = jnp.concatenate([at, u0], axis=1)
        span = 1
        while span < c:
            span *= 2
            rhs = stack2(y) if span >= c else jnp.concatenate([stack2(y), stack2(x)], axis=1)
            z = jnp.dot(x.astype(BF16), rhs, preferred_element_type=F32)
            yield
            y = y + z[:, 0:2 * LANE]
            if span < c:
                x = z[:, 2 * LANE:]
        zero = jnp.zeros((c, LANE), F32)
        v_pad = jnp.concatenate([zero, vv], axis=1)
        wu = jnp.dot(a_r.astype(BF16), jnp.concatenate([stack2(y), stack2(v_pad)], axis=0),
                     preferred_element_type=F32)
        mn = tn(bh_kh, jnp.concatenate([y, v_pad], axis=0))
        yield
        q_eff = rt + wu[:, 0:LANE]
        m_bd = jnp.where(same_head, mn[:, 0:LANE], 0.0) + jnp.where(eye, jnp.exp(g_end), 0.0)
        n_bd = jnp.where(same_head, mn[:, LANE:], 0.0)
        st = st_ref[dr * n_pairs + hp].astype(BF16)
        so = jnp.dot(jnp.concatenate([q_eff, m_bd], axis=0).astype(BF16), st, preferred_element_type=F32)
        yield
        refs[6][pl.ds(r0, c), ls] = so[0:c] + wu[:, LANE:]
        st_ref[dr * n_pairs + hp] = so[c:] + n_bd

    def chunk(ic, carry):
        live = [pair_chunk(dr, hp, pl.multiple_of((ic if dr == 0 else n_chunks - 1 - ic) * c, c))
                for dr in range(2) for hp in range(n_pairs)]
        while live:
            live = [gen for gen in live if next(gen, True) is None]
        return carry

    lax.fori_loop(0, n_chunks, chunk, 0)


def _rw_scan(r, v, kk, w_f, k_f, b_f, w_b, k_b, b_b, tb):
    b, t, w = r.shape
    n_j = t // tb
    fwd = pl.BlockSpec((None, tb, w), lambda bi, j: (bi, j, 0))
    bwd = pl.BlockSpec((None, tb, w), lambda bi, j: (bi, _bwd_block(j, n_j), 0))
    out = jax.ShapeDtypeStruct((b, t, w), F32)
    return pl.pallas_call(
        _rw_scan_kernel,
        out_shape=(out, out),
        grid=(b, n_j),
        in_specs=[fwd] * 6 + [bwd] * 6,
        out_specs=(fwd, bwd),
        scratch_shapes=[pltpu.VMEM((2 * (w // LANE), LANE, LANE), F32), pltpu.VMEM((2, tb, w), F32)],
        compiler_params=_cparams("parallel", "arbitrary"),
    )(r, v, kk, w_f, k_f, b_f, r, v, kk, w_b, k_b, b_b)


def _rw_finish_kernel(of_ref, ob_ref, r_ref, kf_ref, kb_ref, v_ref, g_ref, gnw_ref, gnb_ref, rk_ref, o_ref):
    o = of_ref[...] + ob_ref[...]
    cen = o - _head_sum(o, RW_HD) * (1.0 / RW_HD)
    var = _head_sum(cen * cen, RW_HD) * (1.0 / RW_HD)
    o = cen * lax.rsqrt(var + RW_GN_EPS) * gnw_ref[...] + gnb_ref[...]
    dot_rk = _head_sum(r_ref[...] * (kf_ref[...] + kb_ref[...]) * rk_ref[...], RW_HD)
    o_ref[...] = ((o + dot_rk * v_ref[...]) * g_ref[...]).astype(o_ref.dtype)


def _rw_finish(o_f, o_b, r, k_f, k_b, v, g, gn_w, gn_b, r_k, tb):
    b, t, w = o_f.shape
    spec = pl.BlockSpec((None, tb, w), lambda bi, j: (bi, j, 0))
    cspec = pl.BlockSpec((1, w), lambda bi, j: (0, 0))
    return pl.pallas_call(
        _rw_finish_kernel,
        out_shape=jax.ShapeDtypeStruct((b, t, w), BF16),
        grid=(b, t // tb),
        in_specs=[spec] * 7 + [cspec] * 3,
        out_specs=spec,
        compiler_params=_cparams("parallel", "parallel"),
    )(o_f, o_b, r, k_f, k_b, v, g, gn_w, gn_b, r_k)


def _rms(x, gain, n):
    return x * lax.rsqrt(jnp.sum(x * x, axis=-1, keepdims=True) * (1.0 / n) + NORM_EPS) * gain


def _rope128(x, cos, sin):
    half = MLA_ROPE // 2
    lane = lax.broadcasted_iota(jnp.int32, x.shape, 1)
    swapped = jnp.where(lane < half, pltpu.roll(x, LANE - half, 1), pltpu.roll(x, half, 1))
    return x * cos + swapped * sin


def _mla_prep_kernel(p_ref, cos_ref, sin_ref, qn_ref, kvn_ref, wq_ref, wkv_ref, qng_ref, qrg_ref,
                     kng_ref, krg_ref, q_o, k_o, v_o):
    q_rank = qn_ref.shape[-1]
    kv_rank = kvn_ref.shape[-1]
    n_heads = v_o.shape[-1] // MLA_V
    slot = MLA_NOPE + LANE
    cos, sin = cos_ref[...], sin_ref[...]
    p = p_ref[...]
    q = _bdot(_rms(p[:, 0:q_rank], qn_ref[...], q_rank), wq_ref[...])
    kv = _bdot(_rms(p[:, q_rank:q_rank + kv_rank], kvn_ref[...], kv_rank), wkv_ref[...])
    k_rope = _rope128(_rms(p[:, q_rank + kv_rank:], krg_ref[...], MLA_ROPE), cos, sin).astype(k_o.dtype)
    for h in range(n_heads):
        q_nope = _rms(q[:, h * slot:h * slot + MLA_NOPE], qng_ref[...], MLA_NOPE)
        q_rope = _rope128(_rms(q[:, h * slot + MLA_NOPE:(h + 1) * slot], qrg_ref[...], MLA_ROPE), cos, sin)
        q_o[:, h * slot:h * slot + MLA_NOPE] = (q_nope * MLA_SCALE).astype(q_o.dtype)
        q_o[:, h * slot + MLA_NOPE:(h + 1) * slot] = (q_rope * MLA_SCALE).astype(q_o.dtype)
        k_nope = _rms(kv[:, h * MLA_NOPE:(h + 1) * MLA_NOPE], kng_ref[...], MLA_NOPE)
        k_o[:, h * slot:h * slot + MLA_NOPE] = k_nope.astype(k_o.dtype)
        k_o[:, h * slot + MLA_NOPE:(h + 1) * slot] = k_rope
    v_o[...] = kv[:, n_heads * MLA_NOPE:].astype(v_o.dtype)


def _mla_prep(p_ml, cos, sin, q_norm, kv_norm, w_uq, w_ukv, qn_g, qr_g, kn_g, kr_g, tb):
    b, t, n = p_ml.shape
    n_heads = w_uq.shape[1] // (MLA_NOPE + LANE)

    def const(a):
        return pl.BlockSpec(a.shape, lambda bi, j: (0,) * a.ndim)

    def out(width):
        return (jax.ShapeDtypeStruct((b, t, width), BF16), pl.BlockSpec((None, tb, width), lambda bi, j: (bi, j, 0)))

    outs = [out(n_heads * (MLA_NOPE + LANE)), out(n_heads * (MLA_NOPE + LANE)), out(n_heads * MLA_V)]
    return pl.pallas_call(
        _mla_prep_kernel,
        out_shape=tuple(o[0] for o in outs),
        grid=(b, t // tb),
        in_specs=[
            pl.BlockSpec((None, tb, n), lambda bi, j: (bi, j, 0)),
            pl.BlockSpec((tb, LANE), lambda bi, j: (j, 0)),
            pl.BlockSpec((tb, LANE), lambda bi, j: (j, 0)),
            const(q_norm), const(kv_norm), const(w_uq), const(w_ukv), const(qn_g), const(qr_g),
            const(kn_g), const(kr_g),
        ],
        out_specs=tuple(o[1] for o in outs),
        compiler_params=_cparams("parallel", "parallel"),
    )(p_ml, cos, sin, q_norm, kv_norm, w_uq, w_ukv, qn_g, qr_g, kn_g, kr_g)


def _mla_attn_kernel(q_ref, k_ref, v_ref, o_ref):
    tb = q_ref.shape[0]

    def attend(k, v):
        half = tb // 2
        parts = [slice(0, half), slice(half, tb)]
        scores = [lax.dot_general(q_ref[p, :], k, (((1,), (1,)), ((), ())), preferred_element_type=F32)
                  for p in parts]
        for p, s in zip(parts, scores):
            e = jnp.exp(s - jnp.max(s, axis=-1, keepdims=True))
            o = jnp.dot(e.astype(BF16), v, preferred_element_type=F32)
            o_ref[p, :] = (o / jnp.sum(e, axis=-1, keepdims=True)).astype(o_ref.dtype)

    @pl.when(pl.program_id(2) == 0)
    def _():
        attend(k_ref[0:tb, :], v_ref[0:tb, :])

    @pl.when(pl.program_id(2) > 0)
    def _():
        attend(k_ref[...], v_ref[...])


def _mla_attn(q, k, v, tb):
    b, t, _ = q.shape
    n_heads = v.shape[-1] // MLA_V
    slot = MLA_NOPE + LANE
    return pl.pallas_call(
        _mla_attn_kernel,
        out_shape=jax.ShapeDtypeStruct((b, t, n_heads * MLA_V), BF16),
        grid=(b, n_heads, t // tb),
        in_specs=[
            pl.BlockSpec((None, tb, slot), lambda bi, h, j: (bi, j, h)),
            pl.BlockSpec((None, t, slot), lambda bi, h, j: (bi, 0, h)),
            pl.BlockSpec((None, t, MLA_V), lambda bi, h, j: (bi, 0, h)),
        ],
        out_specs=pl.BlockSpec((None, tb, MLA_V), lambda bi, h, j: (bi, j, h)),
        compiler_params=_cparams("parallel", "parallel", "arbitrary"),
    )(q, k, v)


def _rope_tables(n_ctx, n_lat):
    rows = n_lat // GRID_W
    row = jnp.repeat(jnp.arange(rows, dtype=F32), GRID_W)
    col = jnp.tile(jnp.arange(GRID_W, dtype=F32), rows)
    axis_dim = MLA_ROPE // 2
    inv_freq = ROPE_THETA ** (-jnp.arange(0, axis_dim, 2, dtype=F32) / axis_dim)
    ang = jnp.concatenate([row[:, None] * inv_freq, col[:, None] * inv_freq], axis=-1)
    ang = jnp.concatenate([jnp.zeros((n_ctx, axis_dim), F32), ang], axis=0)
    cos, sin = jnp.cos(ang), jnp.sin(ang)
    return (_pad_cols(jnp.concatenate([cos, cos], axis=-1), LANE),
            _pad_cols(jnp.concatenate([-sin, sin], axis=-1), LANE))


def _pick(n, prefs):
    for p in prefs:
        if n % p == 0:
            return p
    return n


def kernel(x, c, ctx, c_ctx, norm_g, w_mod, b_mod, w_in, w_out, hg_lb, hg_gn, rw_mu, rw_w0, rw_w2, rw_a0, rw_a2, rw_g2, rw_kk, rw_ka, rw_rk, rw_gn_w, rw_gn_b, mla_q_norm, mla_w_uq, mla_kv_norm, mla_w_ukv, mla_qn_g, mla_qr_g, mla_kn_g, mla_kr_g, ffn_up, ffn_dw, ffn_db, ffn_down):
    bsz, n_lat, d = x.shape
    n_ctx = ctx.shape[1]
    depth = w_in.shape[0]
    hg_w = hg_lb.shape[-1]
    rw_w = rw_w0.shape[-1]
    q_rank = mla_q_norm.shape[-1]
    kv_rank = mla_kv_norm.shape[-1]
    n_heads = mla_w_ukv.shape[-1] // (MLA_NOPE + MLA_V)
    d_ff = ffn_down.shape[1]
    lora = (rw_w2.shape[2], rw_w2.shape[2], rw_a2.shape[2], rw_a2.shape[2], rw_g2.shape[1])
    tb = n_ctx
    nb = 2 if bsz % 2 == 0 else 1
    nb_mm = 4 if bsz % 4 == 0 else nb
    assert n_lat % tb == 0 and tb % HG_CHUNK == 0 and max(lora) <= LANE

    cc = jnp.concatenate([c, c_ctx[None, :]], axis=0)
    cc = jnp.pad(cc, ((0, -(bsz + 1) % SUBLANE), (0, 0)))
    mods = _mods(cc, w_mod, b_mod)
    mod_lat = mods[:, :bsz].reshape(depth, bsz, 6, d).transpose(0, 2, 1, 3)
    mod_ctx = jnp.broadcast_to(mods[:, bsz].reshape(depth, 6, 1, d), (depth, 6, bsz, d))
    modtab = jnp.stack([mod_ctx, mod_lat], axis=2)[:, :, :, :, None, :]

    lb_p = jax.nn.softmax(hg_lb.astype(F32), axis=0)
    lower = jnp.cumsum(lb_p, axis=0) - lb_p[0]

    hg_cols = 5 * hg_w
    rw_cols = 3 * rw_w + sum(lora)
    w_hg = w_in[:, :, :hg_cols].astype(BF16)
    w_rw_raw = w_in[:, :, hg_cols:hg_cols + rw_cols]
    w_ml = w_in[:, :, hg_cols + rw_cols:]
    offs = [3 * rw_w]
    for n in lora:
        offs.append(offs[-1] + n)

    def pad_lora(a):
        parts = [a[..., :3 * rw_w]] + [_pad_cols(a[..., offs[i]:offs[i + 1]], LANE) for i in range(5)]
        return jnp.concatenate(parts, axis=-1)

    w_rw = pad_lora(w_rw_raw).astype(BF16)
    mu = pad_lora(rw_mu)[:, None, :]
    w_ml = _pad_cols(w_ml, q_rank + kv_rank + LANE).astype(BF16)

    def pad_rows(a):
        return jnp.pad(a, ((0, 0),) * (a.ndim - 2) + ((0, LANE - a.shape[-2]), (0, 0)))

    w2 = pad_rows(rw_w2).astype(BF16)
    a2 = pad_rows(rw_a2).astype(BF16)
    g2 = pad_rows(rw_g2).astype(BF16)

    wq = mla_w_uq.reshape(depth, q_rank, n_heads, MLA_NOPE + MLA_ROPE)
    wq = _pad_cols(wq, MLA_NOPE + LANE).reshape(depth, q_rank, n_heads * (MLA_NOPE + LANE)).astype(BF16)
    wkv = mla_w_ukv.reshape(depth, kv_rank, n_heads, MLA_NOPE + MLA_V)
    wkv = jnp.concatenate([wkv[..., :MLA_NOPE].reshape(depth, kv_rank, -1),
                           wkv[..., MLA_NOPE:].reshape(depth, kv_rank, -1)], axis=-1).astype(BF16)
    cos, sin = _rope_tables(n_ctx, n_lat)

    w_out_b = w_out.astype(BF16)
    up_b = ffn_up.astype(BF16)
    down_b = ffn_down.astype(BF16)

    tn_hg = _pick(hg_cols, (512, 256, 128))
    tn_d = _pick(d, (512, 256, 128))
    tk = _pick(d_ff, (512, 256, 128))

    xs = jnp.concatenate([ctx, x], axis=1)
    for l in range(depth):
        gain0, gain1 = norm_g[l, 0:1], norm_g[l, 1:2]
        mt = modtab[l]
        h = _norm(xs, gain0, mt, tb, nb)
        p_hg = _in_proj(h, w_hg[l], tb, nb_mm, tn_hg)
        p_rw = _in_proj(h, w_rw[l], tb, nb, w_rw.shape[-1])
        p_ml = _in_proj(h, w_ml[l], tb, nb_mm, w_ml.shape[-1])

        hg_f, hg_b = _hg_scan(p_hg, lower[l], tb)
        hg_o = _hg_finish(hg_f, hg_b, p_hg, hg_gn[l][None, :], tb)

        r, v, kk, w_f, k_f, b_f, w_b, k_b, b_b, g = _rw_prep(
            p_rw, mu[l], rw_w0[l], w2[l], rw_a0[l], a2[l], g2[l], rw_kk[l][None, :], rw_ka[l][None, :], tb)
        rw_f, rw_b = _rw_scan(r, v, kk, w_f, k_f, b_f, w_b, k_b, b_b, tb)
        rw_o = _rw_finish(rw_f, rw_b, r, k_f, k_b, v, g, rw_gn_w[l][None, :], rw_gn_b[l][None, :],
                          rw_rk[l].reshape(1, rw_w), tb)

        q, k, vv = _mla_prep(p_ml, cos, sin, mla_q_norm[l][None, :], mla_kv_norm[l][None, :], wq[l], wkv[l],
                             mla_qn_g[l][None, :], _pad_cols(mla_qr_g[l][None, :], LANE),
                             mla_kn_g[l][None, :], _pad_cols(mla_kr_g[l][None, :], LANE), tb)
        ml_o = _mla_attn(q, k, vv, tb)

        xs = _out_proj(xs, hg_o, rw_o, ml_o, w_out_b[l], mt, tb, nb, tn_d)
        xs = _ffn(xs, gain1, mt, up_b[l], ffn_dw[l], ffn_db[l][None, :], down_b[l], tb, nb, tk)
    return xs[:, n_ctx:]
```

```python
import functools
import math

import jax
import jax.numpy as jnp
from jax import lax
from jax.experimental import pallas as pl
from jax.experimental.pallas import tpu as pltpu

F32 = jnp.float32
BF16 = jnp.bfloat16

NORM_EPS = 1e-6
GRID_W = 64
ROPE_THETA = 10000.0
HG_HD = 128
HG_CHUNK = 16
RW_HD = 64
RW_GN_EPS = 64e-5
RW_DECAY_MAX = math.exp(-0.5)
MLA_V = 128
MLA_NOPE = 128
MLA_ROPE = 64
MLA_SCALE = (MLA_NOPE + MLA_ROPE) ** -0.5
LANE = 128
SUBLANE = 8
RW_CHUNK = 64
NORM_ROW_TILE = 16
ATTN_HEADS = 2
FFN_COL_GROUPS = 2
FFN_ROW_TILE = 32
VMEM_LIMIT = 56 * 1024 * 1024
HI = lax.Precision.HIGHEST


def _cparams(*sem):
    return pltpu.CompilerParams(dimension_semantics=sem, vmem_limit_bytes=VMEM_LIMIT)


def _bdot(a, b):
    return jnp.dot(a.astype(BF16), b.astype(BF16), preferred_element_type=F32)


def _sigmoid(x):
    return 1.0 / (1.0 + jnp.exp(-x))


def _silu(x):
    return x * _sigmoid(x)


def _pad_cols(a, width):
    return jnp.pad(a, [(0, 0)] * (a.ndim - 1) + [(0, width - a.shape[-1])])


def _mods_kernel(c_ref, w_ref, b_ref, o_ref):
    o_ref[...] = _bdot(_silu(c_ref[...]), w_ref[...]) + b_ref[...]


def _mods(cc, w_mod, b_mod):
    n_layers, d, n = w_mod.shape
    rows = cc.shape[0]
    tn = 768 if n % 768 == 0 else n
    return pl.pallas_call(
        _mods_kernel,
        out_shape=jax.ShapeDtypeStruct((n_layers, rows, n), F32),
        grid=(n_layers, n // tn),
        in_specs=[
            pl.BlockSpec((rows, d), lambda l, j: (0, 0)),
            pl.BlockSpec((None, d, tn), lambda l, j: (l, 0, j)),
            pl.BlockSpec((None, 1, tn), lambda l, j: (l, 0, j)),
        ],
        out_specs=pl.BlockSpec((None, rows, tn), lambda l, j: (l, 0, j)),
        compiler_params=_cparams("parallel", "parallel"),
    )(cc, w_mod, b_mod.reshape(n_layers, 1, n))


def _norm_mod(x, gain, shift, scale):
    y = x * lax.rsqrt(jnp.mean(x * x, axis=-1, keepdims=True) + NORM_EPS) * gain
    return y * (1.0 + scale) + shift


def _norm_mod_rows(x_ref, h_ref, g_ref, sh_ref, sc_ref):
    nb, tb, _ = x_ref.shape
    rt = NORM_ROW_TILE
    for n in range(nb):
        mult = g_ref[...] * (1.0 + sc_ref[n])
        shift = sh_ref[n]

        def tile(i, carry, n=n, mult=mult, shift=shift):
            r = pl.multiple_of(i * rt, rt)
            x = x_ref[n, pl.ds(r, rt), :]
            rs = lax.rsqrt(jnp.mean(x * x, axis=-1, keepdims=True) + NORM_EPS)
            h = (x * rs * mult + shift).astype(h_ref.dtype)
            if len(h_ref.shape) == 3:
                h_ref[n, pl.ds(r, rt), :] = h
            else:
                h_ref[pl.ds(pl.multiple_of(n * tb + r, rt), rt), :] = h
            return carry

        lax.fori_loop(0, tb // rt, tile, 0, unroll=4)


def _mod_spec(m, nb, d):
    return pl.BlockSpec((None, None, nb, 1, d), lambda bi, j: (m, jnp.minimum(j, 1), bi, 0, 0))


def _norm_kernel(x_ref, g_ref, sh_ref, sc_ref, h_ref):
    _norm_mod_rows(x_ref, h_ref, g_ref, sh_ref, sc_ref)


def _norm(x, gain, modtab, tb, nb):
    b, t, d = x.shape
    return pl.pallas_call(
        _norm_kernel,
        out_shape=jax.ShapeDtypeStruct((b, t, d), BF16),
        grid=(b // nb, t // tb),
        in_specs=[
            pl.BlockSpec((nb, tb, d), lambda bi, j: (bi, j, 0)),
            pl.BlockSpec((1, d), lambda bi, j: (0, 0)),
            _mod_spec(0, nb, d),
            _mod_spec(1, nb, d),
        ],
        out_specs=pl.BlockSpec((nb, tb, d), lambda bi, j: (bi, j, 0)),
        compiler_params=_cparams("parallel", "parallel"),
    )(x, gain, modtab, modtab)


def _in_proj_kernel(h_ref, w_ref, o_ref):
    nb, tb, d = h_ref.shape
    o_ref[...] = jnp.dot(h_ref[...].reshape(nb * tb, d), w_ref[...],
                         preferred_element_type=F32).reshape(o_ref.shape)


def _in_proj(h, w, tb, nb, tn):
    b, t, d = h.shape
    n = w.shape[1]
    return pl.pallas_call(
        _in_proj_kernel,
        out_shape=jax.ShapeDtypeStruct((b, t, n), F32),
        grid=(b // nb, t // tb, n // tn),
        in_specs=[
            pl.BlockSpec((nb, tb, d), lambda bi, j, k: (bi, j, 0)),
            pl.BlockSpec((d, tn), lambda bi, j, k: (0, k)),
        ],
        out_specs=pl.BlockSpec((nb, tb, tn), lambda bi, j, k: (bi, j, k)),
        compiler_params=_cparams("parallel", "parallel", "arbitrary"),
    )(h, w)


def _out_proj_kernel(x_ref, hg_ref, rw_ref, ml_ref, w_ref, gate_ref, o_ref):
    nb, tb, tn = o_ref.shape
    mixed = jnp.concatenate([hg_ref[...], rw_ref[...], ml_ref[...]], axis=-1)
    acc = jnp.dot(mixed.reshape(nb * tb, mixed.shape[-1]), w_ref[...], preferred_element_type=F32)
    o_ref[...] = x_ref[...] + gate_ref[...] * acc.reshape(nb, tb, tn)


def _out_proj(x, hg, rw, ml, w, modtab, tb, nb, tn):
    b, t, d = x.shape
    return pl.pallas_call(
        _out_proj_kernel,
        out_shape=jax.ShapeDtypeStruct((b, t, d), F32),
        grid=(b // nb, t // tb, d // tn),
        in_specs=[
            pl.BlockSpec((nb, tb, tn), lambda bi, j, k: (bi, j, k)),
            pl.BlockSpec((nb, tb, hg.shape[-1]), lambda bi, j, k: (bi, j, 0)),
            pl.BlockSpec((nb, tb, rw.shape[-1]), lambda bi, j, k: (bi, j, 0)),
            pl.BlockSpec((nb, tb, ml.shape[-1]), lambda bi, j, k: (bi, j, 0)),
            pl.BlockSpec((d, tn), lambda bi, j, k: (0, k)),
            pl.BlockSpec((None, None, nb, 1, tn), lambda bi, j, k: (2, jnp.minimum(j, 1), bi, 0, k)),
        ],
        out_specs=pl.BlockSpec((nb, tb, tn), lambda bi, j, k: (bi, j, k)),
        compiler_params=_cparams("parallel", "parallel", "arbitrary"),
    )(x, hg, rw, ml, w, modtab)


def _shift_rows(u, tb, first_rows, last_rows):
    rows = u.shape[0]
    row = lax.broadcasted_iota(jnp.int32, u.shape, 0)
    prev = pltpu.roll(u, 1, 0)
    nxt = pltpu.roll(u, rows - 1, 0)
    for n, (fr, lr) in enumerate(zip(first_rows, last_rows)):
        prev = jnp.where(row == n * tb, fr, prev)
        nxt = jnp.where(row == n * tb + tb - 1, lr, nxt)
    return prev, nxt


def _ffn_kernel(x_ref, xp_ref, xn_ref, g_ref, sh_ref, sc_ref, gate_ref, wa_ref, wb_ref, dwa_ref,
                dwb_ref, dba_ref, dbb_ref, wd_ref, o_ref, h_ref, ua_ref, ub_ref, act_ref):
    nb, tb, d = x_ref.shape
    rows = nb * tb
    halo = 2 * SUBLANE
    rt = FFN_ROW_TILE
    j = pl.program_id(1)
    k = pl.program_id(2)
    n_k = pl.num_programs(2)
    n_j = pl.num_programs(1)

    @pl.when(k == 0)
    def _():
        p_ok = (j >= 2).astype(F32)
        n_ok = jnp.logical_and(j >= 1, j < n_j - 1).astype(F32)
        g, sh, sc = g_ref[...], sh_ref[...], sc_ref[...]
        _norm_mod_rows(x_ref, h_ref, g_ref, sh_ref, sc_ref)
        hp = (_norm_mod(xp_ref[...], g, sh, sc) * p_ok).astype(BF16)
        hn = (_norm_mod(xn_ref[...], g, sh, sc) * n_ok).astype(BF16)
        for n in range(nb):
            h_ref[rows + n * halo:rows + n * halo + SUBLANE, :] = hp[n]
            h_ref[rows + n * halo + SUBLANE:rows + (n + 1) * halo, :] = hn[n]
        o_ref[...] = jnp.zeros_like(o_ref)

    tk = ua_ref.shape[1]
    tc = tk // FFN_COL_GROUPS
    groups = [slice(i * tc, (i + 1) * tc) for i in range(FFN_COL_GROUPS)]
    for cs in groups:
        ua_ref[:, cs] = jnp.dot(h_ref[...], wa_ref[:, cs], preferred_element_type=F32)
        ub_ref[:, cs] = jnp.dot(h_ref[...], wb_ref[:, cs], preferred_element_type=F32)

    trow = lax.broadcasted_iota(jnp.int32, (rt, tc), 0)

    def conv(u_ref, dw, db, cs, n, r):
        base = n * tb + r
        mid = u_ref[base:base + rt, cs]
        if r == 0:
            edge = u_ref[rows + n * halo + SUBLANE - 1:rows + n * halo + SUBLANE, cs]
            prev = jnp.where(trow == 0, edge, pltpu.roll(mid, 1, 0))
        else:
            prev = u_ref[base - 1:base - 1 + rt, cs]
        if r == tb - rt:
            edge = u_ref[rows + n * halo + SUBLANE:rows + n * halo + SUBLANE + 1, cs]
            nxt = jnp.where(trow == rt - 1, edge, pltpu.roll(mid, rt - 1, 0))
        else:
            nxt = u_ref[base + 1:base + 1 + rt, cs]
        return prev * dw[0:1, cs] + mid * dw[1:2, cs] + nxt * dw[2:3, cs] + db[:, cs]

    dwa, dwb, dba, dbb = dwa_ref[...], dwb_ref[...], dba_ref[...], dbb_ref[...]
    for cs in groups:
        for n in range(nb):
            for r in range(0, tb, rt):
                a = conv(ua_ref, dwa, dba, cs, n, r)
                b = conv(ub_ref, dwb, dbb, cs, n, r)
                act_ref[n * tb + r:n * tb + r + rt, cs] = (_silu(a) * b).astype(BF16)
        o_ref[...] += jnp.dot(act_ref[:, cs], wd_ref[cs, :], preferred_element_type=F32).reshape(nb, tb, d)

    @pl.when(k == n_k - 1)
    def _():
        o_ref[...] = x_ref[...] + gate_ref[...] * o_ref[...]


def _ffn(x, gain, modtab, w_up, dw, db, w_down, tb, nb, tk):
    b, t, d = x.shape
    d_ff = w_down.shape[0]
    n_k = d_ff // tk
    r8 = tb // SUBLANE
    n8 = t // SUBLANE

    def mspec(m):
        return pl.BlockSpec((None, None, nb, 1, d), lambda bi, j, k: (m, jnp.minimum(j, 1), bi, 0, 0))

    return pl.pallas_call(
        _ffn_kernel,
        out_shape=jax.ShapeDtypeStruct((b, t, d), F32),
        grid=(b // nb, t // tb, n_k),
        in_specs=[
            pl.BlockSpec((nb, tb, d), lambda bi, j, k: (bi, j, 0)),
            pl.BlockSpec((nb, SUBLANE, d), lambda bi, j, k: (bi, jnp.maximum(j * r8 - 1, 0), 0)),
            pl.BlockSpec((nb, SUBLANE, d), lambda bi, j, k: (bi, jnp.minimum((j + 1) * r8, n8 - 1), 0)),
            pl.BlockSpec((1, d), lambda bi, j, k: (0, 0)),
            mspec(3), mspec(4), mspec(5),
            pl.BlockSpec((d, tk), lambda bi, j, k: (0, k)),
            pl.BlockSpec((d, tk), lambda bi, j, k: (0, n_k + k)),
            pl.BlockSpec((3, tk), lambda bi, j, k: (0, k)),
            pl.BlockSpec((3, tk), lambda bi, j, k: (0, n_k + k)),
            pl.BlockSpec((1, tk), lambda bi, j, k: (0, k)),
            pl.BlockSpec((1, tk), lambda bi, j, k: (0, n_k + k)),
            pl.BlockSpec((tk, d), lambda bi, j, k: (k, 0)),
        ],
        out_specs=pl.BlockSpec((nb, tb, d), lambda bi, j, k: (bi, j, 0)),
        scratch_shapes=[pltpu.VMEM((nb * tb + nb * 2 * SUBLANE, d), BF16),
                        pltpu.VMEM((nb * tb + nb * 2 * SUBLANE, tk), F32),
                        pltpu.VMEM((nb * tb + nb * 2 * SUBLANE, tk), F32),
                        pltpu.VMEM((nb * tb, tk), BF16)],
        compiler_params=_cparams("parallel", "parallel", "arbitrary"),
    )(x, x, x, gain, modtab, modtab, modtab, w_up, w_up, dw, dw, db, db, w_down)


def _chunk_cumsum(x, c, reverse):
    rows = x.shape[0]
    ri = lax.broadcasted_iota(jnp.int32, (rows, rows), 0)
    ci = lax.broadcasted_iota(jnp.int32, (rows, rows), 1)
    tri = jnp.logical_and((ri // c) == (ci // c), (ci >= ri) if reverse else (ci <= ri)).astype(BF16)
    hi = x.astype(BF16)
    rest = x - hi.astype(F32)
    mid = rest.astype(BF16)
    lo = (rest - mid.astype(F32)).astype(BF16)
    return (jnp.dot(tri, hi, preferred_element_type=F32) + jnp.dot(tri, mid, preferred_element_type=F32)
            + jnp.dot(tri, lo, preferred_element_type=F32))


def _bwd_block(j, n_j):
    return jnp.where(j == 0, 0, n_j - j)


def _hg_scan_kernel(qf_ref, zf_ref, if_ref, qb_ref, zb_ref, ib_ref, lb_ref, of_ref, ob_ref,
                    st_ref, g_ref, k_ref):
    tb, w = zf_ref.shape
    n_heads = w // HG_HD
    n_chunks = tb // HG_CHUNK
    c = HG_CHUNK

    @pl.when(pl.program_id(1) == 0)
    def _():
        st_ref[...] = jnp.zeros_like(st_ref)

    ones = jnp.ones((HG_HD, HG_HD), BF16)
    trow = lax.broadcasted_iota(jnp.int32, (c, HG_HD), 0)

    dirs = ((qf_ref, zf_ref, if_ref, of_ref), (qb_ref, zb_ref, ib_ref, ob_ref))
    for dr, (q_ref, z_ref, i_ref, o_ref) in enumerate(dirs):
        lb = lb_ref[dr:dr + 1, :]
        f = lb + (1.0 - lb) * _sigmoid(z_ref[...])
        g_ref[dr] = _chunk_cumsum(jnp.log(f), c, dr == 1)
        k_ref[dr] = 1.0 - f

    def dir_chunk(dr, r0):
        q_ref, _z_ref, i_ref, o_ref = dirs[dr]
        heads = []
        tiles = []
        for h in range(n_heads):
            ls = slice(h * HG_HD, (h + 1) * HG_HD)
            q = q_ref[pl.ds(r0, c), ls]
            v = i_ref[pl.ds(r0, c), ls]
            g = g_ref[dr, pl.ds(r0, c), ls]
            kk = k_ref[dr, pl.ds(r0, c), ls]
            heads.append((ls, q, v, g, kk))
            for s in range(c):
                seen = (trow >= s) if dr == 0 else (trow <= s)
                tiles.append(jnp.where(seen, q * kk[s:s + 1] * jnp.exp(g - g[s:s + 1]), 0.0).astype(BF16))
        attn = jnp.dot(jnp.concatenate(tiles, axis=0), ones, preferred_element_type=F32)
        inter = []
        for h, (ls, q, v, g, kk) in enumerate(heads):
            g_last = g[c - 1:c] if dr == 0 else g[0:1]
            st = st_ref[dr, h]
            inter.append(lax.dot_general((q * jnp.exp(g)).astype(BF16), st.astype(BF16),
                                         (((1,), (1,)), ((), ())), preferred_element_type=F32))
            kd = kk * jnp.exp(g_last - g)
            st_ref[dr, h] = st * jnp.exp(g_last) + lax.dot_general(
                v.astype(BF16), kd.astype(BF16), (((0,), (0,)), ((), ())), preferred_element_type=F32)
        yield
        for h, (ls, q, v, g, kk) in enumerate(heads):
            o = inter[h]
            for s in range(c):
                o += attn[(h * c + s) * c:(h * c + s + 1) * c] * v[s:s + 1]
            o_ref[pl.ds(r0, c), ls] = o

    def chunk(ic, carry):
        live = [dir_chunk(dr, pl.multiple_of((ic if dr == 0 else n_chunks - 1 - ic) * c, c)) for dr in range(2)]
        while live:
            live = [gen for gen in live if next(gen, True) is None]
        return carry

    lax.fori_loop(0, n_chunks, chunk, 0)


def _hg_scan(p_hg, lb, tb):
    b, t, _ = p_hg.shape
    w = lb.shape[-1]
    n_j = t // tb

    def fwd(col):
        return pl.BlockSpec((None, tb, w), lambda bi, j: (bi, j, col))

    def bwd(col):
        return pl.BlockSpec((None, tb, w), lambda bi, j: (bi, _bwd_block(j, n_j), col))

    out = jax.ShapeDtypeStruct((b, t, w), F32)
    return pl.pallas_call(
        _hg_scan_kernel,
        out_shape=(out, out),
        grid=(b, n_j),
        in_specs=[fwd(0), fwd(1), fwd(3), bwd(0), bwd(2), bwd(3), pl.BlockSpec((2, w), lambda bi, j: (0, 0))],
        out_specs=(fwd(0), bwd(0)),
        scratch_shapes=[pltpu.VMEM((2, w // HG_HD, HG_HD, HG_HD), F32), pltpu.VMEM((2, tb, w), F32),
                        pltpu.VMEM((2, tb, w), F32)],
        compiler_params=_cparams("parallel", "arbitrary"),
    )(p_hg, p_hg, p_hg, p_hg, p_hg, p_hg, lb)


def _hg_finish_kernel(of_ref, ob_ref, gate_ref, gn_ref, o_ref):
    w = of_ref.shape[-1]
    for h in range(w // HG_HD):
        ls = slice(h * HG_HD, (h + 1) * HG_HD)
        o = of_ref[:, ls] + ob_ref[:, ls]
        o = o * lax.rsqrt(jnp.mean(o * o, axis=-1, keepdims=True) + NORM_EPS) * gn_ref[...]
        o_ref[:, ls] = (o * _silu(gate_ref[:, ls])).astype(o_ref.dtype)


def _hg_finish(o_f, o_b, p_hg, gn, tb):
    b, t, w = o_f.shape
    spec = pl.BlockSpec((None, tb, w), lambda bi, j: (bi, j, 0))
    return pl.pallas_call(
        _hg_finish_kernel,
        out_shape=jax.ShapeDtypeStruct((b, t, w), BF16),
        grid=(b, t // tb),
        in_specs=[spec, spec, pl.BlockSpec((None, tb, w), lambda bi, j: (bi, j, 4)),
                  pl.BlockSpec((1, HG_HD), lambda bi, j: (0, 0))],
        out_specs=spec,
        compiler_params=_cparams("parallel", "parallel"),
    )(o_f, o_b, p_hg, gn)


def _head_ones(width, hd):
    r = lax.broadcasted_iota(jnp.int32, (width, width), 0) // hd
    c = lax.broadcasted_iota(jnp.int32, (width, width), 1) // hd
    return (r == c).astype(F32)


def _head_sum(x, hd):
    same_head = _head_ones(LANE, hd).astype(BF16)
    out = []
    for i in range(x.shape[-1] // LANE):
        xs = x[:, i * LANE:(i + 1) * LANE]
        hi = xs.astype(BF16)
        lo = (xs - hi.astype(F32)).astype(BF16)
        out.append(jnp.dot(hi, same_head, preferred_element_type=F32)
                   + jnp.dot(lo, same_head, preferred_element_type=F32))
    return jnp.concatenate(out, axis=-1)


def _rw_prep_kernel(p_ref, pp_ref, pn_ref, mu_ref, w0_ref, w2_ref, a0_ref, a2_ref, g2_ref, kk_ref,
                    ka_ref, r_o, v_o, kk_o, wf_o, kf_o, bf_o, wb_o, kb_o, bb_o, g_o):
    tb = p_ref.shape[0]
    w = r_o.shape[-1]
    lw = w2_ref.shape[1]
    j = pl.program_id(1)
    n_j = pl.num_programs(1)
    p = p_ref[...]
    p_ok = (j >= 2).astype(F32)
    n_ok = jnp.logical_and(j >= 1, j < n_j - 1).astype(F32)
    prev, nxt = _shift_rows(p, tb, [pp_ref[SUBLANE - 1:SUBLANE, :] * p_ok], [pn_ref[0:1, :] * n_ok])
    s = p + mu_ref[...] * (0.5 * (prev + nxt) - p)
    r, k, v = s[:, 0:w], s[:, w:2 * w], s[:, 2 * w:3 * w]
    lora = [s[:, 3 * w + i * lw:3 * w + (i + 1) * lw] for i in range(5)]
    kk = k * kk_ref[...]
    ssq = _head_sum(kk * kk, RW_HD)
    kk = kk / jnp.maximum(jnp.sqrt(ssq), 1e-12)
    r_o[...] = r
    v_o[...] = v
    kk_o[...] = kk
    for dr, (w_o, k_o, b_o) in enumerate(((wf_o, kf_o, bf_o), (wb_o, kb_o, bb_o))):
        xw, xa = lora[dr], lora[2 + dr]
        w_o[...] = -RW_DECAY_MAX * _sigmoid(w0_ref[dr:dr + 1, :] + _bdot(jnp.tanh(xw), w2_ref[dr]))
        a = _sigmoid(a0_ref[dr:dr + 1, :] + _bdot(xa, a2_ref[dr]))
        k_o[...] = k * (1.0 + (a - 1.0) * ka_ref[...])
        b_o[...] = kk * a
    g_o[...] = _bdot(_sigmoid(lora[4]), g2_ref[...])


def _rw_prep(p_rw, mu, w0, w2, a0, a2, g2, k_k, k_a, tb):
    b, t, n = p_rw.shape
    w = w0.shape[-1]
    r8, n8 = tb // SUBLANE, t // SUBLANE
    out = jax.ShapeDtypeStruct((b, t, w), F32)
    ospec = pl.BlockSpec((None, tb, w), lambda bi, j: (bi, j, 0))

    def const(a):
        return pl.BlockSpec(a.shape, lambda bi, j: (0,) * a.ndim)

    return pl.pallas_call(
        _rw_prep_kernel,
        out_shape=(out,) * 10,
        grid=(b, t // tb),
        in_specs=[
            pl.BlockSpec((None, tb, n), lambda bi, j: (bi, j, 0)),
            pl.BlockSpec((None, SUBLANE, n), lambda bi, j: (bi, jnp.maximum(j * r8 - 1, 0), 0)),
            pl.BlockSpec((None, SUBLANE, n), lambda bi, j: (bi, jnp.minimum((j + 1) * r8, n8 - 1), 0)),
            const(mu), const(w0), const(w2), const(a0), const(a2), const(g2), const(k_k), const(k_a),
        ],
        out_specs=(ospec,) * 10,
        compiler_params=_cparams("parallel", "parallel"),
    )(p_rw, p_rw, p_rw, mu, w0, w2, a0, a2, g2, k_k, k_a)


def _rw_scan_kernel(rf_ref, vf_ref, kkf_ref, wf_ref, kf_ref, bf_ref, rb_ref, vb_ref, kkb_ref, wb_ref,
                    kb_ref, bb_ref, of_ref, ob_ref, st_ref, g_ref):
    tb, w = rf_ref.shape
    n_pairs = w // LANE
    hd = RW_HD
    c = RW_CHUNK
    n_chunks = tb // c
    assert c == hd and tb % c == 0

    @pl.when(pl.program_id(1) == 0)
    def _():
        st_ref[...] = jnp.zeros_like(st_ref)

    dirs = ((rf_ref, vf_ref, kkf_ref, wf_ref, kf_ref, bf_ref, of_ref),
            (rb_ref, vb_ref, kkb_ref, wb_ref, kb_ref, bb_ref, ob_ref))
    for dr, refs in enumerate(dirs):
        g_ref[dr] = _chunk_cumsum(refs[3][...], c, dr == 1)

    row = lax.broadcasted_iota(jnp.int32, (c, LANE), 0)
    col = lax.broadcasted_iota(jnp.int32, (c, LANE), 1) % hd
    head_a = lax.broadcasted_iota(jnp.int32, (c, LANE), 1) < hd
    head_a2 = (lax.broadcasted_iota(jnp.int32, (c, 2 * LANE), 1) % LANE) < hd
    r2 = lax.broadcasted_iota(jnp.int32, (LANE, LANE), 0)
    c2 = lax.broadcasted_iota(jnp.int32, (LANE, LANE), 1)
    same_head = (r2 // hd) == (c2 // hd)
    eye = r2 == c2

    def stack2(y):
        m = head_a if y.shape[1] == LANE else head_a2
        return jnp.concatenate([jnp.where(m, y, 0.0), jnp.where(m, 0.0, y)], axis=0).astype(BF16)

    def tn(x, y):
        return lax.dot_general(x.astype(BF16), y.astype(BF16), (((0,), (0,)), ((), ())),
                               preferred_element_type=F32)

    def pair_chunk(dr, hp, r0):
        refs = dirs[dr]
        before = (col < row) if dr == 0 else (col > row)
        upto = (col <= row) if dr == 0 else (col >= row)
        ls = slice(hp * LANE, (hp + 1) * LANE)
        rr, vv, kk, lw, kd, bb = [ref[pl.ds(r0, c), ls] for ref in refs[:6]]
        g = g_ref[dr, pl.ds(r0, c), ls]
        g_end = g[c - 1:c] if dr == 0 else g[0:1]
        at = -kk * jnp.exp(g - lw)
        rt = rr * jnp.exp(g)
        e_inv = jnp.exp(-g)
        e_out = jnp.exp(g_end - g)
        bh_kh = jnp.concatenate([bb * e_out, kd * e_out], axis=0)
        abk = lax.dot_general(jnp.concatenate([at, rt], axis=0).astype(BF16),
                              jnp.concatenate([stack2(bb * e_inv), stack2(kd * e_inv)], axis=0),
                              (((1,), (1,)), ((), ())), preferred_element_type=F32)
        yield
        x = jnp.where(before, abk[0:c, 0:LANE], 0.0)
        a_ak = jnp.where(before, abk[0:c, LANE:], 0.0)
        a_r = jnp.concatenate([jnp.where(upto, abk[c:, 0:LANE], 0.0), jnp.where(upto, abk[c:, LANE:], 0.0)], axis=1)
        u0 = jnp.dot(a_ak.astype(BF16), stack2(vv), preferred_element_type=F32)
        yield
        y = jnp.concatenate([at, u0], axis=1)
        span = 1
        while span < c:
            span *= 2
            rhs = stack2(y) if span >= c else jnp.concatenate([stack2(y), stack2(x)], axis=1)
            z = jnp.dot(x.astype(BF16), rhs, preferred_element_type=F32)
            yield
            y = y + z[:, 0:2 * LANE]
            if span < c:
                x = z[:, 2 * LANE:]
        zero = jnp.zeros((c, LANE), F32)
        v_pad = jnp.concatenate([zero, vv], axis=1)
        wu = jnp.dot(a_r.astype(BF16), jnp.concatenate([stack2(y), stack2(v_pad)], axis=0),
                     preferred_element_type=F32)
        mn = tn(bh_kh, jnp.concatenate([y, v_pad], axis=0))
        yield
        q_eff = rt + wu[:, 0:LANE]
        m_bd = jnp.where(same_head, mn[:, 0:LANE], 0.0) + jnp.where(eye, jnp.exp(g_end), 0.0)
        n_bd = jnp.where(same_head, mn[:, LANE:], 0.0)
        st = st_ref[dr * n_pairs + hp].astype(BF16)
        so = jnp.dot(jnp.concatenate([q_eff, m_bd], axis=0).astype(BF16), st, preferred_element_type=F32)
        yield
        refs[6][pl.ds(r0, c), ls] = so[0:c] + wu[:, LANE:]
        st_ref[dr * n_pairs + hp] = so[c:] + n_bd

    def chunk(ic, carry):
        live = [pair_chunk(dr, hp, pl.multiple_of((ic if dr == 0 else n_chunks - 1 - ic) * c, c))
                for dr in range(2) for hp in range(n_pairs)]
        while live:
            live = [gen for gen in live if next(gen, True) is None]
        return carry

    lax.fori_loop(0, n_chunks, chunk, 0)


def _rw_scan(r, v, kk, w_f, k_f, b_f, w_b, k_b, b_b, tb):
    b, t, w = r.shape
    n_j = t // tb
    fwd = pl.BlockSpec((None, tb, w), lambda bi, j: (bi, j, 0))
    bwd = pl.BlockSpec((None, tb, w), lambda bi, j: (bi, _bwd_block(j, n_j), 0))
    out = jax.ShapeDtypeStruct((b, t, w), F32)
    return pl.pallas_call(
        _rw_scan_kernel,
        out_shape=(out, out),
        grid=(b, n_j),
        in_specs=[fwd] * 6 + [bwd] * 6,
        out_specs=(fwd, bwd),
        scratch_shapes=[pltpu.VMEM((2 * (w // LANE), LANE, LANE), F32), pltpu.VMEM((2, tb, w), F32)],
        compiler_params=_cparams("parallel", "arbitrary"),
    )(r, v, kk, w_f, k_f, b_f, r, v, kk, w_b, k_b, b_b)


def _rw_finish_kernel(of_ref, ob_ref, r_ref, kf_ref, kb_ref, v_ref, g_ref, gnw_ref, gnb_ref, rk_ref, o_ref):
    o = of_ref[...] + ob_ref[...]
    cen = o - _head_sum(o, RW_HD) * (1.0 / RW_HD)
    var = _head_sum(cen * cen, RW_HD) * (1.0 / RW_HD)
    o = cen * lax.rsqrt(var + RW_GN_EPS) * gnw_ref[...] + gnb_ref[...]
    dot_rk = _head_sum(r_ref[...] * (kf_ref[...] + kb_ref[...]) * rk_ref[...], RW_HD)
    o_ref[...] = ((o + dot_rk * v_ref[...]) * g_ref[...]).astype(o_ref.dtype)


def _rw_finish(o_f, o_b, r, k_f, k_b, v, g, gn_w, gn_b, r_k, tb):
    b, t, w = o_f.shape
    spec = pl.BlockSpec((None, tb, w), lambda bi, j: (bi, j, 0))
    cspec = pl.BlockSpec((1, w), lambda bi, j: (0, 0))
    return pl.pallas_call(
        _rw_finish_kernel,
        out_shape=jax.ShapeDtypeStruct((b, t, w), BF16),
        grid=(b, t // tb),
        in_specs=[spec] * 7 + [cspec] * 3,
        out_specs=spec,
        compiler_params=_cparams("parallel", "parallel"),
    )(o_f, o_b, r, k_f, k_b, v, g, gn_w, gn_b, r_k)


def _rms(x, gain, n):
    return x * lax.rsqrt(jnp.sum(x * x, axis=-1, keepdims=True) * (1.0 / n) + NORM_EPS) * gain


def _rope128(x, cos, sin):
    half = MLA_ROPE // 2
    lane = lax.broadcasted_iota(jnp.int32, x.shape, 1)
    swapped = jnp.where(lane < half, pltpu.roll(x, LANE - half, 1), pltpu.roll(x, half, 1))
    return x * cos + swapped * sin


def _mla_prep_kernel(p_ref, cos_ref, sin_ref, qn_ref, kvn_ref, wq_ref, wkv_ref, qng_ref, qrg_ref,
                     kng_ref, krg_ref, q_o, k_o, v_o):
    q_rank = qn_ref.shape[-1]
    kv_rank = kvn_ref.shape[-1]
    n_heads = v_o.shape[-1] // MLA_V
    slot = MLA_NOPE + LANE
    cos, sin = cos_ref[...], sin_ref[...]
    p = p_ref[...]
    q = _bdot(_rms(p[:, 0:q_rank], qn_ref[...], q_rank), wq_ref[...])
    kv = _bdot(_rms(p[:, q_rank:q_rank + kv_rank], kvn_ref[...], kv_rank), wkv_ref[...])
    k_rope = _rope128(_rms(p[:, q_rank + kv_rank:], krg_ref[...], MLA_ROPE), cos, sin).astype(k_o.dtype)
    for h in range(n_heads):
        q_nope = _rms(q[:, h * slot:h * slot + MLA_NOPE], qng_ref[...], MLA_NOPE)
        q_rope = _rope128(_rms(q[:, h * slot + MLA_NOPE:(h + 1) * slot], qrg_ref[...], MLA_ROPE), cos, sin)
        q_o[:, h * slot:h * slot + MLA_NOPE] = (q_nope * MLA_SCALE).astype(q_o.dtype)
        q_o[:, h * slot + MLA_NOPE:(h + 1) * slot] = (q_rope * MLA_SCALE).astype(q_o.dtype)
        k_nope = _rms(kv[:, h * MLA_NOPE:(h + 1) * MLA_NOPE], kng_ref[...], MLA_NOPE)
        k_o[:, h * slot:h * slot + MLA_NOPE] = k_nope.astype(k_o.dtype)
        k_o[:, h * slot + MLA_NOPE:(h + 1) * slot] = k_rope
    v_o[...] = kv[:, n_heads * MLA_NOPE:].astype(v_o.dtype)


def _mla_prep(p_ml, cos, sin, q_norm, kv_norm, w_uq, w_ukv, qn_g, qr_g, kn_g, kr_g, tb):
    b, t, n = p_ml.shape
    n_heads = w_uq.shape[1] // (MLA_NOPE + LANE)

    def const(a):
        return pl.BlockSpec(a.shape, lambda bi, j: (0,) * a.ndim)

    def out(width):
        return (jax.ShapeDtypeStruct((b, t, width), BF16), pl.BlockSpec((None, tb, width), lambda bi, j: (bi, j, 0)))

    outs = [out(n_heads * (MLA_NOPE + LANE)), out(n_heads * (MLA_NOPE + LANE)), out(n_heads * MLA_V)]
    return pl.pallas_call(
        _mla_prep_kernel,
        out_shape=tuple(o[0] for o in outs),
        grid=(b, t // tb),
        in_specs=[
            pl.BlockSpec((None, tb, n), lambda bi, j: (bi, j, 0)),
            pl.BlockSpec((tb, LANE), lambda bi, j: (j, 0)),
            pl.BlockSpec((tb, LANE), lambda bi, j: (j, 0)),
            const(q_norm), const(kv_norm), const(w_uq), const(w_ukv), const(qn_g), const(qr_g),
            const(kn_g), const(kr_g),
        ],
        out_specs=tuple(o[1] for o in outs),
        compiler_params=_cparams("parallel", "parallel"),
    )(p_ml, cos, sin, q_norm, kv_norm, w_uq, w_ukv, qn_g, qr_g, kn_g, kr_g)


def _mla_attn_kernel(q_ref, k_ref, v_ref, o_ref):
    tb = q_ref.shape[0]
    slot = MLA_NOPE + LANE
    n_heads = o_ref.shape[1] // MLA_V

    def head(h, n_keys):
        qs = slice(h * slot, (h + 1) * slot)
        vs = slice(h * MLA_V, (h + 1) * MLA_V)
        s = lax.dot_general(q_ref[:, qs], k_ref[0:n_keys, qs], (((1,), (1,)), ((), ())),
                            preferred_element_type=F32)
        yield
        e = jnp.exp(s - jnp.max(s, axis=-1, keepdims=True))
        o = jnp.dot(e.astype(BF16), v_ref[0:n_keys, vs], preferred_element_type=F32)
        yield
        o_ref[:, vs] = (o / jnp.sum(e, axis=-1, keepdims=True)).astype(o_ref.dtype)

    def attend(n_keys):
        live = [head(h, n_keys) for h in range(n_heads)]
        while live:
            live = [gen for gen in live if next(gen, True) is None]

    @pl.when(pl.program_id(2) == 0)
    def _():
        attend(tb)

    @pl.when(pl.program_id(2) > 0)
    def _():
        attend(k_ref.shape[0])


def _mla_attn(q, k, v, tb):
    b, t, _ = q.shape
    n_heads = v.shape[-1] // MLA_V
    slot = MLA_NOPE + LANE
    hg = ATTN_HEADS if n_heads % ATTN_HEADS == 0 else 1
    return pl.pallas_call(
        _mla_attn_kernel,
        out_shape=jax.ShapeDtypeStruct((b, t, n_heads * MLA_V), BF16),
        grid=(b, n_heads // hg, t // tb),
        in_specs=[
            pl.BlockSpec((None, tb, hg * slot), lambda bi, h, j: (bi, j, h)),
            pl.BlockSpec((None, t, hg * slot), lambda bi, h, j: (bi, 0, h)),
            pl.BlockSpec((None, t, hg * MLA_V), lambda bi, h, j: (bi, 0, h)),
        ],
        out_specs=pl.BlockSpec((None, tb, hg * MLA_V), lambda bi, h, j: (bi, j, h)),
        compiler_params=_cparams("parallel", "parallel", "arbitrary"),
    )(q, k, v)


def _rope_tables(n_ctx, n_lat):
    rows = n_lat // GRID_W
    row = jnp.repeat(jnp.arange(rows, dtype=F32), GRID_W)
    col = jnp.tile(jnp.arange(GRID_W, dtype=F32), rows)
    axis_dim = MLA_ROPE // 2
    inv_freq = ROPE_THETA ** (-jnp.arange(0, axis_dim, 2, dtype=F32) / axis_dim)
    ang = jnp.concatenate([row[:, None] * inv_freq, col[:, None] * inv_freq], axis=-1)
    ang = jnp.concatenate([jnp.zeros((n_ctx, axis_dim), F32), ang], axis=0)
    cos, sin = jnp.cos(ang), jnp.sin(ang)
    return (_pad_cols(jnp.concatenate([cos, cos], axis=-1), LANE),
            _pad_cols(jnp.concatenate([-sin, sin], axis=-1), LANE))


def _pick(n, prefs):
    for p in prefs:
        if n % p == 0:
            return p
    return n


def kernel(x, c, ctx, c_ctx, norm_g, w_mod, b_mod, w_in, w_out, hg_lb, hg_gn, rw_mu, rw_w0, rw_w2, rw_a0, rw_a2, rw_g2, rw_kk, rw_ka, rw_rk, rw_gn_w, rw_gn_b, mla_q_norm, mla_w_uq, mla_kv_norm, mla_w_ukv, mla_qn_g, mla_qr_g, mla_kn_g, mla_kr_g, ffn_up, ffn_dw, ffn_db, ffn_down):
    bsz, n_lat, d = x.shape
    n_ctx = ctx.shape[1]
    depth = w_in.shape[0]
    hg_w = hg_lb.shape[-1]
    rw_w = rw_w0.shape[-1]
    q_rank = mla_q_norm.shape[-1]
    kv_rank = mla_kv_norm.shape[-1]
    n_heads = mla_w_ukv.shape[-1] // (MLA_NOPE + MLA_V)
    d_ff = ffn_down.shape[1]
    lora = (rw_w2.shape[2], rw_w2.shape[2], rw_a2.shape[2], rw_a2.shape[2], rw_g2.shape[1])
    tb = n_ctx
    nb = 2 if bsz % 2 == 0 else 1
    nb_mm = 4 if bsz % 4 == 0 else nb
    assert n_lat % tb == 0 and tb % HG_CHUNK == 0 and max(lora) <= LANE

    cc = jnp.concatenate([c, c_ctx[None, :]], axis=0)
    cc = jnp.pad(cc, ((0, -(bsz + 1) % SUBLANE), (0, 0)))
    mods = _mods(cc, w_mod, b_mod)
    mod_lat = mods[:, :bsz].reshape(depth, bsz, 6, d).transpose(0, 2, 1, 3)
    mod_ctx = jnp.broadcast_to(mods[:, bsz].reshape(depth, 6, 1, d), (depth, 6, bsz, d))
    modtab = jnp.stack([mod_ctx, mod_lat], axis=2)[:, :, :, :, None, :]

    lb_p = jax.nn.softmax(hg_lb.astype(F32), axis=0)
    lower = jnp.cumsum(lb_p, axis=0) - lb_p[0]

    hg_cols = 5 * hg_w
    rw_cols = 3 * rw_w + sum(lora)
    w_hg = w_in[:, :, :hg_cols].astype(BF16)
    w_rw_raw = w_in[:, :, hg_cols:hg_cols + rw_cols]
    w_ml = w_in[:, :, hg_cols + rw_cols:]
    offs = [3 * rw_w]
    for n in lora:
        offs.append(offs[-1] + n)

    def pad_lora(a):
        parts = [a[..., :3 * rw_w]] + [_pad_cols(a[..., offs[i]:offs[i + 1]], LANE) for i in range(5)]
        return jnp.concatenate(parts, axis=-1)

    w_rw = pad_lora(w_rw_raw).astype(BF16)
    mu = pad_lora(rw_mu)[:, None, :]
    w_ml = _pad_cols(w_ml, q_rank + kv_rank + LANE).astype(BF16)

    def pad_rows(a):
        return jnp.pad(a, ((0, 0),) * (a.ndim - 2) + ((0, LANE - a.shape[-2]), (0, 0)))

    w2 = pad_rows(rw_w2).astype(BF16)
    a2 = pad_rows(rw_a2).astype(BF16)
    g2 = pad_rows(rw_g2).astype(BF16)

    wq = mla_w_uq.reshape(depth, q_rank, n_heads, MLA_NOPE + MLA_ROPE)
    wq = _pad_cols(wq, MLA_NOPE + LANE).reshape(depth, q_rank, n_heads * (MLA_NOPE + LANE)).astype(BF16)
    wkv = mla_w_ukv.reshape(depth, kv_rank, n_heads, MLA_NOPE + MLA_V)
    wkv = jnp.concatenate([wkv[..., :MLA_NOPE].reshape(depth, kv_rank, -1),
                           wkv[..., MLA_NOPE:].reshape(depth, kv_rank, -1)], axis=-1).astype(BF16)
    cos, sin = _rope_tables(n_ctx, n_lat)

    w_out_b = w_out.astype(BF16)
    up_b = ffn_up.astype(BF16)
    down_b = ffn_down.astype(BF16)

    tn_hg = _pick(hg_cols, (512, 256, 128))
    tn_d = _pick(d, (512, 256, 128))
    tk = _pick(d_ff, (512, 256, 128))

    xs = jnp.concatenate([ctx, x], axis=1)
    for l in range(depth):
        gain0, gain1 = norm_g[l, 0:1], norm_g[l, 1:2]
        mt = modtab[l]
        h = _norm(xs, gain0, mt, tb, nb)
        p_hg = _in_proj(h, w_hg[l], tb, nb_mm, tn_hg)
        p_rw = _in_proj(h, w_rw[l], tb, nb, w_rw.shape[-1])
        p_ml = _in_proj(h, w_ml[l], tb, nb_mm, w_ml.shape[-1])

        hg_f, hg_b = _hg_scan(p_hg, lower[l], tb)
        hg_o = _hg_finish(hg_f, hg_b, p_hg, hg_gn[l][None, :], tb)

        r, v, kk, w_f, k_f, b_f, w_b, k_b, b_b, g = _rw_prep(
            p_rw, mu[l], rw_w0[l], w2[l], rw_a0[l], a2[l], g2[l], rw_kk[l][None, :], rw_ka[l][None, :], tb)
        rw_f, rw_b = _rw_scan(r, v, kk, w_f, k_f, b_f, w_b, k_b, b_b, tb)
        rw_o = _rw_finish(rw_f, rw_b, r, k_f, k_b, v, g, rw_gn_w[l][None, :], rw_gn_b[l][None, :],
                          rw_rk[l].reshape(1, rw_w), tb)

        q, k, vv = _mla_prep(p_ml, cos, sin, mla_q_norm[l][None, :], mla_kv_norm[l][None, :], wq[l], wkv[l],
                             mla_qn_g[l][None, :], _pad_cols(mla_qr_g[l][None, :], LANE),
                             mla_kn_g[l][None, :], _pad_cols(mla_kr_g[l][None, :], LANE), tb)
        ml_o = _mla_attn(q, k, vv, tb)

        xs = _out_proj(xs, hg_o, rw_o, ml_o, w_out_b[l], mt, tb, nb_mm, tn_d)
        xs = _ffn(xs, gain1, mt, up_b[l], ffn_dw[l], ffn_db[l][None, :], down_b[l], tb, nb, tk)
    return xs[:, n_ctx:]
```

```python
import functools
import math

import jax
import jax.numpy as jnp
from jax import lax
from jax.experimental import pallas as pl
from jax.experimental.pallas import tpu as pltpu

F32 = jnp.float32
BF16 = jnp.bfloat16

NORM_EPS = 1e-6
GRID_W = 64
ROPE_THETA = 10000.0
HG_HD = 128
HG_CHUNK = 16
RW_HD = 64
RW_GN_EPS = 64e-5
RW_DECAY_MAX = math.exp(-0.5)
MLA_V = 128
MLA_NOPE = 128
MLA_ROPE = 64
MLA_SCALE = (MLA_NOPE + MLA_ROPE) ** -0.5
LANE = 128
SUBLANE = 8
RW_CHUNK = 64
NORM_ROW_TILE = 16
ATTN_HEADS = 2
FFN_COL_GROUPS = 1
FFN_ROW_TILE = 32
VMEM_LIMIT = 56 * 1024 * 1024
HI = lax.Precision.HIGHEST


def _cparams(*sem):
    return pltpu.CompilerParams(dimension_semantics=sem, vmem_limit_bytes=VMEM_LIMIT)


def _bdot(a, b):
    return jnp.dot(a.astype(BF16), b.astype(BF16), preferred_element_type=F32)


def _sigmoid(x):
    return 1.0 / (1.0 + jnp.exp(-x))


def _silu(x):
    return x * _sigmoid(x)


def _pad_cols(a, width):
    return jnp.pad(a, [(0, 0)] * (a.ndim - 1) + [(0, width - a.shape[-1])])


def _mods_kernel(c_ref, w_ref, b_ref, o_ref):
    o_ref[...] = _bdot(_silu(c_ref[...]), w_ref[...]) + b_ref[...]


def _mods(cc, w_mod, b_mod):
    n_layers, d, n = w_mod.shape
    rows = cc.shape[0]
    tn = 768 if n % 768 == 0 else n
    return pl.pallas_call(
        _mods_kernel,
        out_shape=jax.ShapeDtypeStruct((n_layers, rows, n), F32),
        grid=(n_layers, n // tn),
        in_specs=[
            pl.BlockSpec((rows, d), lambda l, j: (0, 0)),
            pl.BlockSpec((None, d, tn), lambda l, j: (l, 0, j)),
            pl.BlockSpec((None, 1, tn), lambda l, j: (l, 0, j)),
        ],
        out_specs=pl.BlockSpec((None, rows, tn), lambda l, j: (l, 0, j)),
        compiler_params=_cparams("parallel", "parallel"),
    )(cc, w_mod, b_mod.reshape(n_layers, 1, n))


def _norm_mod(x, gain, shift, scale):
    y = x * lax.rsqrt(jnp.mean(x * x, axis=-1, keepdims=True) + NORM_EPS) * gain
    return y * (1.0 + scale) + shift


def _norm_mod_rows(x_ref, h_ref, g_ref, sh_ref, sc_ref):
    nb, tb, _ = x_ref.shape
    rt = NORM_ROW_TILE
    for n in range(nb):
        mult = g_ref[...] * (1.0 + sc_ref[n])
        shift = sh_ref[n]

        def tile(i, carry, n=n, mult=mult, shift=shift):
            r = pl.multiple_of(i * rt, rt)
            x = x_ref[n, pl.ds(r, rt), :]
            rs = lax.rsqrt(jnp.mean(x * x, axis=-1, keepdims=True) + NORM_EPS)
            h = (x * rs * mult + shift).astype(h_ref.dtype)
            if len(h_ref.shape) == 3:
                h_ref[n, pl.ds(r, rt), :] = h
            else:
                h_ref[pl.ds(pl.multiple_of(n * tb + r, rt), rt), :] = h
            return carry

        lax.fori_loop(0, tb // rt, tile, 0, unroll=4)


def _mod_spec(m, nb, d):
    return pl.BlockSpec((None, None, nb, 1, d), lambda bi, j: (m, jnp.minimum(j, 1), bi, 0, 0))


def _norm_kernel(x_ref, g_ref, sh_ref, sc_ref, h_ref):
    _norm_mod_rows(x_ref, h_ref, g_ref, sh_ref, sc_ref)


def _norm(x, gain, modtab, tb, nb):
    b, t, d = x.shape
    return pl.pallas_call(
        _norm_kernel,
        out_shape=jax.ShapeDtypeStruct((b, t, d), BF16),
        grid=(b // nb, t // tb),
        in_specs=[
            pl.BlockSpec((nb, tb, d), lambda bi, j: (bi, j, 0)),
            pl.BlockSpec((1, d), lambda bi, j: (0, 0)),
            _mod_spec(0, nb, d),
            _mod_spec(1, nb, d),
        ],
        out_specs=pl.BlockSpec((nb, tb, d), lambda bi, j: (bi, j, 0)),
        compiler_params=_cparams("parallel", "parallel"),
    )(x, gain, modtab, modtab)


def _in_proj_kernel(h_ref, w_ref, o_ref):
    nb, tb, d = h_ref.shape
    o_ref[...] = jnp.dot(h_ref[...].reshape(nb * tb, d), w_ref[...],
                         preferred_element_type=F32).reshape(o_ref.shape)


def _in_proj(h, w, tb, nb, tn):
    b, t, d = h.shape
    n = w.shape[1]
    return pl.pallas_call(
        _in_proj_kernel,
        out_shape=jax.ShapeDtypeStruct((b, t, n), F32),
        grid=(b // nb, t // tb, n // tn),
        in_specs=[
            pl.BlockSpec((nb, tb, d), lambda bi, j, k: (bi, j, 0)),
            pl.BlockSpec((d, tn), lambda bi, j, k: (0, k)),
        ],
        out_specs=pl.BlockSpec((nb, tb, tn), lambda bi, j, k: (bi, j, k)),
        compiler_params=_cparams("parallel", "parallel", "arbitrary"),
    )(h, w)


def _out_proj_kernel(x_ref, hg_ref, rw_ref, ml_ref, w_ref, gate_ref, o_ref):
    nb, tb, tn = o_ref.shape
    mixed = jnp.concatenate([hg_ref[...], rw_ref[...], ml_ref[...]], axis=-1)
    acc = jnp.dot(mixed.reshape(nb * tb, mixed.shape[-1]), w_ref[...], preferred_element_type=F32)
    o_ref[...] = x_ref[...] + gate_ref[...] * acc.reshape(nb, tb, tn)


def _out_proj(x, hg, rw, ml, w, modtab, tb, nb, tn):
    b, t, d = x.shape
    return pl.pallas_call(
        _out_proj_kernel,
        out_shape=jax.ShapeDtypeStruct((b, t, d), F32),
        grid=(b // nb, t // tb, d // tn),
        in_specs=[
            pl.BlockSpec((nb, tb, tn), lambda bi, j, k: (bi, j, k)),
            pl.BlockSpec((nb, tb, hg.shape[-1]), lambda bi, j, k: (bi, j, 0)),
            pl.BlockSpec((nb, tb, rw.shape[-1]), lambda bi, j, k: (bi, j, 0)),
            pl.BlockSpec((nb, tb, ml.shape[-1]), lambda bi, j, k: (bi, j, 0)),
            pl.BlockSpec((d, tn), lambda bi, j, k: (0, k)),
            pl.BlockSpec((None, None, nb, 1, tn), lambda bi, j, k: (2, jnp.minimum(j, 1), bi, 0, k)),
        ],
        out_specs=pl.BlockSpec((nb, tb, tn), lambda bi, j, k: (bi, j, k)),
        compiler_params=_cparams("parallel", "parallel", "arbitrary"),
    )(x, hg, rw, ml, w, modtab)


def _shift_rows(u, tb, first_rows, last_rows):
    rows = u.shape[0]
    row = lax.broadcasted_iota(jnp.int32, u.shape, 0)
    prev = pltpu.roll(u, 1, 0)
    nxt = pltpu.roll(u, rows - 1, 0)
    for n, (fr, lr) in enumerate(zip(first_rows, last_rows)):
        prev = jnp.where(row == n * tb, fr, prev)
        nxt = jnp.where(row == n * tb + tb - 1, lr, nxt)
    return prev, nxt


def _ffn_kernel(x_ref, xp_ref, xn_ref, g_ref, sh_ref, sc_ref, gate_ref, wa_ref, wb_ref, dwa_ref,
                dwb_ref, dba_ref, dbb_ref, wd_ref, o_ref, h_ref, ua_ref, ub_ref, act_ref):
    nb, tb, d = x_ref.shape
    rows = nb * tb
    halo = 2 * SUBLANE
    rt = FFN_ROW_TILE
    j = pl.program_id(1)
    k = pl.program_id(2)
    n_k = pl.num_programs(2)
    n_j = pl.num_programs(1)

    @pl.when(k == 0)
    def _():
        p_ok = (j >= 2).astype(F32)
        n_ok = jnp.logical_and(j >= 1, j < n_j - 1).astype(F32)
        g, sh, sc = g_ref[...], sh_ref[...], sc_ref[...]
        _norm_mod_rows(x_ref, h_ref, g_ref, sh_ref, sc_ref)
        hp = (_norm_mod(xp_ref[...], g, sh, sc) * p_ok).astype(BF16)
        hn = (_norm_mod(xn_ref[...], g, sh, sc) * n_ok).astype(BF16)
        for n in range(nb):
            h_ref[rows + n * halo:rows + n * halo + SUBLANE, :] = hp[n]
            h_ref[rows + n * halo + SUBLANE:rows + (n + 1) * halo, :] = hn[n]
        o_ref[...] = jnp.zeros_like(o_ref)

    tk = ua_ref.shape[1]
    tc = tk // FFN_COL_GROUPS
    groups = [slice(i * tc, (i + 1) * tc) for i in range(FFN_COL_GROUPS)]
    for cs in groups:
        ua_ref[:, cs] = jnp.dot(h_ref[...], wa_ref[:, cs], preferred_element_type=F32)
        ub_ref[:, cs] = jnp.dot(h_ref[...], wb_ref[:, cs], preferred_element_type=F32)

    trow = lax.broadcasted_iota(jnp.int32, (rt, tc), 0)

    def conv(u_ref, dw, db, cs, n, r):
        base = n * tb + r
        mid = u_ref[base:base + rt, cs]
        if r == 0:
            edge = u_ref[rows + n * halo + SUBLANE - 1:rows + n * halo + SUBLANE, cs]
            prev = jnp.where(trow == 0, edge, pltpu.roll(mid, 1, 0))
        else:
            prev = u_ref[base - 1:base - 1 + rt, cs]
        if r == tb - rt:
            edge = u_ref[rows + n * halo + SUBLANE:rows + n * halo + SUBLANE + 1, cs]
            nxt = jnp.where(trow == rt - 1, edge, pltpu.roll(mid, rt - 1, 0))
        else:
            nxt = u_ref[base + 1:base + 1 + rt, cs]
        return prev * dw[0:1, cs] + mid * dw[1:2, cs] + nxt * dw[2:3, cs] + db[:, cs]

    dwa, dwb, dba, dbb = dwa_ref[...], dwb_ref[...], dba_ref[...], dbb_ref[...]
    for cs in groups:
        for n in range(nb):
            for r in range(0, tb, rt):
                a = conv(ua_ref, dwa, dba, cs, n, r)
                b = conv(ub_ref, dwb, dbb, cs, n, r)
                act_ref[n * tb + r:n * tb + r + rt, cs] = (_silu(a) * b).astype(BF16)
        o_ref[...] += jnp.dot(act_ref[:, cs], wd_ref[cs, :], preferred_element_type=F32).reshape(nb, tb, d)

    @pl.when(k == n_k - 1)
    def _():
        o_ref[...] = x_ref[...] + gate_ref[...] * o_ref[...]


def _ffn(x, gain, modtab, w_up, dw, db, w_down, tb, nb):
    b, t, d = x.shape
    _, n_k, _, tk = w_up.shape
    r8 = tb // SUBLANE
    n8 = t // SUBLANE

    def mspec(m):
        return pl.BlockSpec((None, None, nb, 1, d), lambda bi, j, k: (m, jnp.minimum(j, 1), bi, 0, 0))

    return pl.pallas_call(
        _ffn_kernel,
        out_shape=jax.ShapeDtypeStruct((b, t, d), F32),
        grid=(b // nb, t // tb, n_k),
        in_specs=[
            pl.BlockSpec((nb, tb, d), lambda bi, j, k: (bi, j, 0)),
            pl.BlockSpec((nb, SUBLANE, d), lambda bi, j, k: (bi, jnp.maximum(j * r8 - 1, 0), 0)),
            pl.BlockSpec((nb, SUBLANE, d), lambda bi, j, k: (bi, jnp.minimum((j + 1) * r8, n8 - 1), 0)),
            pl.BlockSpec((1, d), lambda bi, j, k: (0, 0)),
            mspec(3), mspec(4), mspec(5),
            pl.BlockSpec((None, None, d, tk), lambda bi, j, k: (0, k, 0, 0)),
            pl.BlockSpec((None, None, d, tk), lambda bi, j, k: (1, k, 0, 0)),
            pl.BlockSpec((3, tk), lambda bi, j, k: (0, k)),
            pl.BlockSpec((3, tk), lambda bi, j, k: (0, n_k + k)),
            pl.BlockSpec((1, tk), lambda bi, j, k: (0, k)),
            pl.BlockSpec((1, tk), lambda bi, j, k: (0, n_k + k)),
            pl.BlockSpec((tk, d), lambda bi, j, k: (k, 0)),
        ],
        out_specs=pl.BlockSpec((nb, tb, d), lambda bi, j, k: (bi, j, 0)),
        scratch_shapes=[pltpu.VMEM((nb * tb + nb * 2 * SUBLANE, d), BF16),
                        pltpu.VMEM((nb * tb + nb * 2 * SUBLANE, tk), F32),
                        pltpu.VMEM((nb * tb + nb * 2 * SUBLANE, tk), F32),
                        pltpu.VMEM((nb * tb, tk), BF16)],
        compiler_params=_cparams("parallel", "parallel", "arbitrary"),
    )(x, x, x, gain, modtab, modtab, modtab, w_up, w_up, dw, dw, db, db, w_down)


def _chunk_cumsum(x, c, reverse):
    rows = x.shape[0]
    ri = lax.broadcasted_iota(jnp.int32, (rows, rows), 0)
    ci = lax.broadcasted_iota(jnp.int32, (rows, rows), 1)
    tri = jnp.logical_and((ri // c) == (ci // c), (ci >= ri) if reverse else (ci <= ri)).astype(BF16)
    hi = x.astype(BF16)
    rest = x - hi.astype(F32)
    mid = rest.astype(BF16)
    lo = (rest - mid.astype(F32)).astype(BF16)
    return (jnp.dot(tri, hi, preferred_element_type=F32) + jnp.dot(tri, mid, preferred_element_type=F32)
            + jnp.dot(tri, lo, preferred_element_type=F32))


def _bwd_block(j, n_j):
    return jnp.where(j == 0, 0, n_j - j)


def _hg_scan_kernel(qf_ref, zf_ref, if_ref, qb_ref, zb_ref, ib_ref, lb_ref, of_ref, ob_ref,
                    st_ref, g_ref, k_ref):
    tb, w = zf_ref.shape
    n_heads = w // HG_HD
    n_chunks = tb // HG_CHUNK
    c = HG_CHUNK

    @pl.when(pl.program_id(1) == 0)
    def _():
        st_ref[...] = jnp.zeros_like(st_ref)

    ones = jnp.ones((HG_HD, HG_HD), BF16)
    trow = lax.broadcasted_iota(jnp.int32, (c, HG_HD), 0)

    dirs = ((qf_ref, zf_ref, if_ref, of_ref), (qb_ref, zb_ref, ib_ref, ob_ref))
    for dr, (q_ref, z_ref, i_ref, o_ref) in enumerate(dirs):
        lb = lb_ref[dr:dr + 1, :]
        f = lb + (1.0 - lb) * _sigmoid(z_ref[...])
        g_ref[dr] = _chunk_cumsum(jnp.log(f), c, dr == 1)
        k_ref[dr] = 1.0 - f

    def dir_chunk(dr, r0):
        q_ref, _z_ref, i_ref, o_ref = dirs[dr]
        heads = []
        tiles = []
        for h in range(n_heads):
            ls = slice(h * HG_HD, (h + 1) * HG_HD)
            q = q_ref[pl.ds(r0, c), ls]
            v = i_ref[pl.ds(r0, c), ls]
            g = g_ref[dr, pl.ds(r0, c), ls]
            kk = k_ref[dr, pl.ds(r0, c), ls]
            heads.append((ls, q, v, g, kk))
            for s in range(c):
                seen = (trow >= s) if dr == 0 else (trow <= s)
                tiles.append(jnp.where(seen, q * kk[s:s + 1] * jnp.exp(g - g[s:s + 1]), 0.0).astype(BF16))
        attn = jnp.dot(jnp.concatenate(tiles, axis=0), ones, preferred_element_type=F32)
        inter = []
        for h, (ls, q, v, g, kk) in enumerate(heads):
            g_last = g[c - 1:c] if dr == 0 else g[0:1]
            st = st_ref[dr, h]
            inter.append(lax.dot_general((q * jnp.exp(g)).astype(BF16), st.astype(BF16),
                                         (((1,), (1,)), ((), ())), preferred_element_type=F32))
            kd = kk * jnp.exp(g_last - g)
            st_ref[dr, h] = st * jnp.exp(g_last) + lax.dot_general(
                v.astype(BF16), kd.astype(BF16), (((0,), (0,)), ((), ())), preferred_element_type=F32)
        yield
        for h, (ls, q, v, g, kk) in enumerate(heads):
            o = inter[h]
            for s in range(c):
                o += attn[(h * c + s) * c:(h * c + s + 1) * c] * v[s:s + 1]
            o_ref[pl.ds(r0, c), ls] = o

    def chunk(ic, carry):
        live = [dir_chunk(dr, pl.multiple_of((ic if dr == 0 else n_chunks - 1 - ic) * c, c)) for dr in range(2)]
        while live:
            live = [gen for gen in live if next(gen, True) is None]
        return carry

    lax.fori_loop(0, n_chunks, chunk, 0)


def _hg_scan(p_hg, lb, tb):
    b, t, _ = p_hg.shape
    w = lb.shape[-1]
    n_j = t // tb

    def fwd(col):
        return pl.BlockSpec((None, tb, w), lambda bi, j: (bi, j, col))

    def bwd(col):
        return pl.BlockSpec((None, tb, w), lambda bi, j: (bi, _bwd_block(j, n_j), col))

    out = jax.ShapeDtypeStruct((b, t, w), F32)
    return pl.pallas_call(
        _hg_scan_kernel,
        out_shape=(out, out),
        grid=(b, n_j),
        in_specs=[fwd(0), fwd(1), fwd(3), bwd(0), bwd(2), bwd(3), pl.BlockSpec((2, w), lambda bi, j: (0, 0))],
        out_specs=(fwd(0), bwd(0)),
        scratch_shapes=[pltpu.VMEM((2, w // HG_HD, HG_HD, HG_HD), F32), pltpu.VMEM((2, tb, w), F32),
                        pltpu.VMEM((2, tb, w), F32)],
        compiler_params=_cparams("parallel", "arbitrary"),
    )(p_hg, p_hg, p_hg, p_hg, p_hg, p_hg, lb)


def _hg_finish_kernel(of_ref, ob_ref, gate_ref, gn_ref, o_ref):
    w = of_ref.shape[-1]
    for h in range(w // HG_HD):
        ls = slice(h * HG_HD, (h + 1) * HG_HD)
        o = of_ref[:, ls] + ob_ref[:, ls]
        o = o * lax.rsqrt(jnp.mean(o * o, axis=-1, keepdims=True) + NORM_EPS) * gn_ref[...]
        o_ref[:, ls] = (o * _silu(gate_ref[:, ls])).astype(o_ref.dtype)


def _hg_finish(o_f, o_b, p_hg, gn, tb):
    b, t, w = o_f.shape
    spec = pl.BlockSpec((None, tb, w), lambda bi, j: (bi, j, 0))
    return pl.pallas_call(
        _hg_finish_kernel,
        out_shape=jax.ShapeDtypeStruct((b, t, w), BF16),
        grid=(b, t // tb),
        in_specs=[spec, spec, pl.BlockSpec((None, tb, w), lambda bi, j: (bi, j, 4)),
                  pl.BlockSpec((1, HG_HD), lambda bi, j: (0, 0))],
        out_specs=spec,
        compiler_params=_cparams("parallel", "parallel"),
    )(o_f, o_b, p_hg, gn)


def _head_ones(width, hd):
    r = lax.broadcasted_iota(jnp.int32, (width, width), 0) // hd
    c = lax.broadcasted_iota(jnp.int32, (width, width), 1) // hd
    return (r == c).astype(F32)


def _head_sum(x, hd):
    same_head = _head_ones(LANE, hd).astype(BF16)
    out = []
    for i in range(x.shape[-1] // LANE):
        xs = x[:, i * LANE:(i + 1) * LANE]
        hi = xs.astype(BF16)
        lo = (xs - hi.astype(F32)).astype(BF16)
        out.append(jnp.dot(hi, same_head, preferred_element_type=F32)
                   + jnp.dot(lo, same_head, preferred_element_type=F32))
    return jnp.concatenate(out, axis=-1)


def _rw_prep_kernel(p_ref, pp_ref, pn_ref, mu_ref, w0_ref, w2_ref, a0_ref, a2_ref, g2_ref, kk_ref,
                    ka_ref, r_o, v_o, kk_o, wf_o, kf_o, bf_o, wb_o, kb_o, bb_o, g_o):
    tb = p_ref.shape[0]
    w = r_o.shape[-1]
    lw = w2_ref.shape[1]
    j = pl.program_id(1)
    n_j = pl.num_programs(1)
    p = p_ref[...]
    p_ok = (j >= 2).astype(F32)
    n_ok = jnp.logical_and(j >= 1, j < n_j - 1).astype(F32)
    prev, nxt = _shift_rows(p, tb, [pp_ref[SUBLANE - 1:SUBLANE, :] * p_ok], [pn_ref[0:1, :] * n_ok])
    s = p + mu_ref[...] * (0.5 * (prev + nxt) - p)
    r, k, v = s[:, 0:w], s[:, w:2 * w], s[:, 2 * w:3 * w]
    lora = [s[:, 3 * w + i * lw:3 * w + (i + 1) * lw] for i in range(5)]
    kk = k * kk_ref[...]
    ssq = _head_sum(kk * kk, RW_HD)
    kk = kk / jnp.maximum(jnp.sqrt(ssq), 1e-12)
    r_o[...] = r
    v_o[...] = v
    kk_o[...] = kk
    for dr, (w_o, k_o, b_o) in enumerate(((wf_o, kf_o, bf_o), (wb_o, kb_o, bb_o))):
        xw, xa = lora[dr], lora[2 + dr]
        w_o[...] = -RW_DECAY_MAX * _sigmoid(w0_ref[dr:dr + 1, :] + _bdot(jnp.tanh(xw), w2_ref[dr]))
        a = _sigmoid(a0_ref[dr:dr + 1, :] + _bdot(xa, a2_ref[dr]))
        k_o[...] = k * (1.0 + (a - 1.0) * ka_ref[...])
        b_o[...] = kk * a
    g_o[...] = _bdot(_sigmoid(lora[4]), g2_ref[...])


def _rw_prep(p_rw, mu, w0, w2, a0, a2, g2, k_k, k_a, tb):
    b, t, n = p_rw.shape
    w = w0.shape[-1]
    r8, n8 = tb // SUBLANE, t // SUBLANE
    out = jax.ShapeDtypeStruct((b, t, w), F32)
    ospec = pl.BlockSpec((None, tb, w), lambda bi, j: (bi, j, 0))

    def const(a):
        return pl.BlockSpec(a.shape, lambda bi, j: (0,) * a.ndim)

    return pl.pallas_call(
        _rw_prep_kernel,
        out_shape=(out,) * 10,
        grid=(b, t // tb),
        in_specs=[
            pl.BlockSpec((None, tb, n), lambda bi, j: (bi, j, 0)),
            pl.BlockSpec((None, SUBLANE, n), lambda bi, j: (bi, jnp.maximum(j * r8 - 1, 0), 0)),
            pl.BlockSpec((None, SUBLANE, n), lambda bi, j: (bi, jnp.minimum((j + 1) * r8, n8 - 1), 0)),
            const(mu), const(w0), const(w2), const(a0), const(a2), const(g2), const(k_k), const(k_a),
        ],
        out_specs=(ospec,) * 10,
        compiler_params=_cparams("parallel", "parallel"),
    )(p_rw, p_rw, p_rw, mu, w0, w2, a0, a2, g2, k_k, k_a)


def _rw_scan_kernel(rf_ref, vf_ref, kkf_ref, wf_ref, kf_ref, bf_ref, rb_ref, vb_ref, kkb_ref, wb_ref,
                    kb_ref, bb_ref, of_ref, ob_ref, st_ref, g_ref):
    tb, w = rf_ref.shape
    n_pairs = w // LANE
    hd = RW_HD
    c = RW_CHUNK
    n_chunks = tb // c
    assert c == hd and tb % c == 0

    @pl.when(pl.program_id(1) == 0)
    def _():
        st_ref[...] = jnp.zeros_like(st_ref)

    dirs = ((rf_ref, vf_ref, kkf_ref, wf_ref, kf_ref, bf_ref, of_ref),
            (rb_ref, vb_ref, kkb_ref, wb_ref, kb_ref, bb_ref, ob_ref))
    for dr, refs in enumerate(dirs):
        g_ref[dr] = _chunk_cumsum(refs[3][...], c, dr == 1)

    row = lax.broadcasted_iota(jnp.int32, (c, LANE), 0)
    col = lax.broadcasted_iota(jnp.int32, (c, LANE), 1) % hd
    head_a = lax.broadcasted_iota(jnp.int32, (c, LANE), 1) < hd
    head_a2 = (lax.broadcasted_iota(jnp.int32, (c, 2 * LANE), 1) % LANE) < hd
    r2 = lax.broadcasted_iota(jnp.int32, (LANE, LANE), 0)
    c2 = lax.broadcasted_iota(jnp.int32, (LANE, LANE), 1)
    same_head = (r2 // hd) == (c2 // hd)
    eye = r2 == c2

    def stack2(y):
        m = head_a if y.shape[1] == LANE else head_a2
        return jnp.concatenate([jnp.where(m, y, 0.0), jnp.where(m, 0.0, y)], axis=0).astype(BF16)

    def tn(x, y):
        return lax.dot_general(x.astype(BF16), y.astype(BF16), (((0,), (0,)), ((), ())),
                               preferred_element_type=F32)

    def pair_chunk(dr, hp, r0):
        refs = dirs[dr]
        before = (col < row) if dr == 0 else (col > row)
        upto = (col <= row) if dr == 0 else (col >= row)
        ls = slice(hp * LANE, (hp + 1) * LANE)
        rr, vv, kk, lw, kd, bb = [ref[pl.ds(r0, c), ls] for ref in refs[:6]]
        g = g_ref[dr, pl.ds(r0, c), ls]
        g_end = g[c - 1:c] if dr == 0 else g[0:1]
        at = -kk * jnp.exp(g - lw)
        rt = rr * jnp.exp(g)
        e_inv = jnp.exp(-g)
        e_out = jnp.exp(g_end - g)
        bh_kh = jnp.concatenate([bb * e_out, kd * e_out], axis=0)
        abk = lax.dot_general(jnp.concatenate([at, rt], axis=0).astype(BF16),
                              jnp.concatenate([stack2(bb * e_inv), stack2(kd * e_inv)], axis=0),
                              (((1,), (1,)), ((), ())), preferred_element_type=F32)
        yield
        x = jnp.where(before, abk[0:c, 0:LANE], 0.0)
        a_ak = jnp.where(before, abk[0:c, LANE:], 0.0)
        a_r = jnp.concatenate([jnp.where(upto, abk[c:, 0:LANE], 0.0), jnp.where(upto, abk[c:, LANE:], 0.0)], axis=1)
        u0 = jnp.dot(a_ak.astype(BF16), stack2(vv), preferred_element_type=F32)
        yield
        y = jnp.concatenate([at, u0], axis=1)
        span = 1
        while span < c:
            span *= 2
            rhs = stack2(y) if span >= c else jnp.concatenate([stack2(y), stack2(x)], axis=1)
            z = jnp.dot(x.astype(BF16), rhs, preferred_element_type=F32)
            yield
            y = y + z[:, 0:2 * LANE]
            if span < c:
                x = z[:, 2 * LANE:]
        zero = jnp.zeros((c, LANE), F32)
        v_pad = jnp.concatenate([zero, vv], axis=1)
        wu = jnp.dot(a_r.astype(BF16), jnp.concatenate([stack2(y), stack2(v_pad)], axis=0),
                     preferred_element_type=F32)
        mn = tn(bh_kh, jnp.concatenate([y, v_pad], axis=0))
        yield
        q_eff = rt + wu[:, 0:LANE]
        m_bd = jnp.where(same_head, mn[:, 0:LANE], 0.0) + jnp.where(eye, jnp.exp(g_end), 0.0)
        n_bd = jnp.where(same_head, mn[:, LANE:], 0.0)
        st = st_ref[dr * n_pairs + hp].astype(BF16)
        so = jnp.dot(jnp.concatenate([q_eff, m_bd], axis=0).astype(BF16), st, preferred_element_type=F32)
        yield
        refs[6][pl.ds(r0, c), ls] = so[0:c] + wu[:, LANE:]
        st_ref[dr * n_pairs + hp] = so[c:] + n_bd

    def chunk(ic, carry):
        live = [pair_chunk(dr, hp, pl.multiple_of((ic if dr == 0 else n_chunks - 1 - ic) * c, c))
                for dr in range(2) for hp in range(n_pairs)]
        while live:
            live = [gen for gen in live if next(gen, True) is None]
        return carry

    lax.fori_loop(0, n_chunks, chunk, 0)


def _rw_scan(r, v, kk, w_f, k_f, b_f, w_b, k_b, b_b, tb):
    b, t, w = r.shape
    n_j = t // tb
    fwd = pl.BlockSpec((None, tb, w), lambda bi, j: (bi, j, 0))
    bwd = pl.BlockSpec((None, tb, w), lambda bi, j: (bi, _bwd_block(j, n_j), 0))
    out = jax.ShapeDtypeStruct((b, t, w), F32)
    return pl.pallas_call(
        _rw_scan_kernel,
        out_shape=(out, out),
        grid=(b, n_j),
        in_specs=[fwd] * 6 + [bwd] * 6,
        out_specs=(fwd, bwd),
        scratch_shapes=[pltpu.VMEM((2 * (w // LANE), LANE, LANE), F32), pltpu.VMEM((2, tb, w), F32)],
        compiler_params=_cparams("parallel", "arbitrary"),
    )(r, v, kk, w_f, k_f, b_f, r, v, kk, w_b, k_b, b_b)


def _rw_finish_kernel(of_ref, ob_ref, r_ref, kf_ref, kb_ref, v_ref, g_ref, gnw_ref, gnb_ref, rk_ref, o_ref):
    o = of_ref[...] + ob_ref[...]
    cen = o - _head_sum(o, RW_HD) * (1.0 / RW_HD)
    var = _head_sum(cen * cen, RW_HD) * (1.0 / RW_HD)
    o = cen * lax.rsqrt(var + RW_GN_EPS) * gnw_ref[...] + gnb_ref[...]
    dot_rk = _head_sum(r_ref[...] * (kf_ref[...] + kb_ref[...]) * rk_ref[...], RW_HD)
    o_ref[...] = ((o + dot_rk * v_ref[...]) * g_ref[...]).astype(o_ref.dtype)


def _rw_finish(o_f, o_b, r, k_f, k_b, v, g, gn_w, gn_b, r_k, tb):
    b, t, w = o_f.shape
    spec = pl.BlockSpec((None, tb, w), lambda bi, j: (bi, j, 0))
    cspec = pl.BlockSpec((1, w), lambda bi, j: (0, 0))
    return pl.pallas_call(
        _rw_finish_kernel,
        out_shape=jax.ShapeDtypeStruct((b, t, w), BF16),
        grid=(b, t // tb),
        in_specs=[spec] * 7 + [cspec] * 3,
        out_specs=spec,
        compiler_params=_cparams("parallel", "parallel"),
    )(o_f, o_b, r, k_f, k_b, v, g, gn_w, gn_b, r_k)


def _rms(x, gain, n):
    return x * lax.rsqrt(jnp.sum(x * x, axis=-1, keepdims=True) * (1.0 / n) + NORM_EPS) * gain


def _rope128(x, cos, sin):
    half = MLA_ROPE // 2
    lane = lax.broadcasted_iota(jnp.int32, x.shape, 1)
    swapped = jnp.where(lane < half, pltpu.roll(x, LANE - half, 1), pltpu.roll(x, half, 1))
    return x * cos + swapped * sin


def _mla_prep_kernel(p_ref, cos_ref, sin_ref, qn_ref, kvn_ref, wq_ref, wkv_ref, qng_ref, qrg_ref,
                     kng_ref, krg_ref, q_o, k_o, v_o):
    q_rank = qn_ref.shape[-1]
    kv_rank = kvn_ref.shape[-1]
    n_heads = v_o.shape[-1] // MLA_V
    slot = MLA_NOPE + LANE
    cos, sin = cos_ref[...], sin_ref[...]
    p = p_ref[...]
    q = _bdot(_rms(p[:, 0:q_rank], qn_ref[...], q_rank), wq_ref[...])
    kv = _bdot(_rms(p[:, q_rank:q_rank + kv_rank], kvn_ref[...], kv_rank), wkv_ref[...])
    k_rope = _rope128(_rms(p[:, q_rank + kv_rank:], krg_ref[...], MLA_ROPE), cos, sin).astype(k_o.dtype)
    for h in range(n_heads):
        q_nope = _rms(q[:, h * slot:h * slot + MLA_NOPE], qng_ref[...], MLA_NOPE)
        q_rope = _rope128(_rms(q[:, h * slot + MLA_NOPE:(h + 1) * slot], qrg_ref[...], MLA_ROPE), cos, sin)
        q_o[:, h * slot:h * slot + MLA_NOPE] = (q_nope * MLA_SCALE).astype(q_o.dtype)
        q_o[:, h * slot + MLA_NOPE:(h + 1) * slot] = (q_rope * MLA_SCALE).astype(q_o.dtype)
        k_nope = _rms(kv[:, h * MLA_NOPE:(h + 1) * MLA_NOPE], kng_ref[...], MLA_NOPE)
        k_o[:, h * slot:h * slot + MLA_NOPE] = k_nope.astype(k_o.dtype)
        k_o[:, h * slot + MLA_NOPE:(h + 1) * slot] = k_rope
    v_o[...] = kv[:, n_heads * MLA_NOPE:].astype(v_o.dtype)


def _mla_prep(p_ml, cos, sin, q_norm, kv_norm, w_uq, w_ukv, qn_g, qr_g, kn_g, kr_g, tb):
    b, t, n = p_ml.shape
    n_heads = w_uq.shape[1] // (MLA_NOPE + LANE)

    def const(a):
        return pl.BlockSpec(a.shape, lambda bi, j: (0,) * a.ndim)

    def out(width):
        return (jax.ShapeDtypeStruct((b, t, width), BF16), pl.BlockSpec((None, tb, width), lambda bi, j: (bi, j, 0)))

    outs = [out(n_heads * (MLA_NOPE + LANE)), out(n_heads * (MLA_NOPE + LANE)), out(n_heads * MLA_V)]
    return pl.pallas_call(
        _mla_prep_kernel,
        out_shape=tuple(o[0] for o in outs),
        grid=(b, t // tb),
        in_specs=[
            pl.BlockSpec((None, tb, n), lambda bi, j: (bi, j, 0)),
            pl.BlockSpec((tb, LANE), lambda bi, j: (j, 0)),
            pl.BlockSpec((tb, LANE), lambda bi, j: (j, 0)),
            const(q_norm), const(kv_norm), const(w_uq), const(w_ukv), const(qn_g), const(qr_g),
            const(kn_g), const(kr_g),
        ],
        out_specs=tuple(o[1] for o in outs),
        compiler_params=_cparams("parallel", "parallel"),
    )(p_ml, cos, sin, q_norm, kv_norm, w_uq, w_ukv, qn_g, qr_g, kn_g, kr_g)


def _mla_attn_kernel(q_ref, k_ref, v_ref, o_ref):
    tb = q_ref.shape[0]
    slot = MLA_NOPE + LANE
    n_heads = o_ref.shape[1] // MLA_V

    def head(h, n_keys):
        qs = slice(h * slot, (h + 1) * slot)
        vs = slice(h * MLA_V, (h + 1) * MLA_V)
        s = lax.dot_general(q_ref[:, qs], k_ref[0:n_keys, qs], (((1,), (1,)), ((), ())),
                            preferred_element_type=F32)
        yield
        e = jnp.exp(s - jnp.max(s, axis=-1, keepdims=True))
        o = jnp.dot(e.astype(BF16), v_ref[0:n_keys, vs], preferred_element_type=F32)
        yield
        o_ref[:, vs] = (o / jnp.sum(e, axis=-1, keepdims=True)).astype(o_ref.dtype)

    def attend(n_keys):
        live = [head(h, n_keys) for h in range(n_heads)]
        while live:
            live = [gen for gen in live if next(gen, True) is None]

    @pl.when(pl.program_id(2) == 0)
    def _():
        attend(tb)

    @pl.when(pl.program_id(2) > 0)
    def _():
        attend(k_ref.shape[0])


def _mla_attn(q, k, v, tb):
    b, t, _ = q.shape
    n_heads = v.shape[-1] // MLA_V
    slot = MLA_NOPE + LANE
    hg = ATTN_HEADS if n_heads % ATTN_HEADS == 0 else 1
    return pl.pallas_call(
        _mla_attn_kernel,
        out_shape=jax.ShapeDtypeStruct((b, t, n_heads * MLA_V), BF16),
        grid=(b, n_heads // hg, t // tb),
        in_specs=[
            pl.BlockSpec((None, tb, hg * slot), lambda bi, h, j: (bi, j, h)),
            pl.BlockSpec((None, t, hg * slot), lambda bi, h, j: (bi, 0, h)),
            pl.BlockSpec((None, t, hg * MLA_V), lambda bi, h, j: (bi, 0, h)),
        ],
        out_specs=pl.BlockSpec((None, tb, hg * MLA_V), lambda bi, h, j: (bi, j, h)),
        compiler_params=_cparams("parallel", "parallel", "arbitrary"),
    )(q, k, v)


def _rope_tables(n_ctx, n_lat):
    rows = n_lat // GRID_W
    row = jnp.repeat(jnp.arange(rows, dtype=F32), GRID_W)
    col = jnp.tile(jnp.arange(GRID_W, dtype=F32), rows)
    axis_dim = MLA_ROPE // 2
    inv_freq = ROPE_THETA ** (-jnp.arange(0, axis_dim, 2, dtype=F32) / axis_dim)
    ang = jnp.concatenate([row[:, None] * inv_freq, col[:, None] * inv_freq], axis=-1)
    ang = jnp.concatenate([jnp.zeros((n_ctx, axis_dim), F32), ang], axis=0)
    cos, sin = jnp.cos(ang), jnp.sin(ang)
    return (_pad_cols(jnp.concatenate([cos, cos], axis=-1), LANE),
            _pad_cols(jnp.concatenate([-sin, sin], axis=-1), LANE))


def _pick(n, prefs):
    for p in prefs:
        if n % p == 0:
            return p
    return n


def kernel(x, c, ctx, c_ctx, norm_g, w_mod, b_mod, w_in, w_out, hg_lb, hg_gn, rw_mu, rw_w0, rw_w2, rw_a0, rw_a2, rw_g2, rw_kk, rw_ka, rw_rk, rw_gn_w, rw_gn_b, mla_q_norm, mla_w_uq, mla_kv_norm, mla_w_ukv, mla_qn_g, mla_qr_g, mla_kn_g, mla_kr_g, ffn_up, ffn_dw, ffn_db, ffn_down):
    bsz, n_lat, d = x.shape
    n_ctx = ctx.shape[1]
    depth = w_in.shape[0]
    hg_w = hg_lb.shape[-1]
    rw_w = rw_w0.shape[-1]
    q_rank = mla_q_norm.shape[-1]
    kv_rank = mla_kv_norm.shape[-1]
    n_heads = mla_w_ukv.shape[-1] // (MLA_NOPE + MLA_V)
    d_ff = ffn_down.shape[1]
    lora = (rw_w2.shape[2], rw_w2.shape[2], rw_a2.shape[2], rw_a2.shape[2], rw_g2.shape[1])
    tb = n_ctx
    nb = 2 if bsz % 2 == 0 else 1
    nb_mm = 4 if bsz % 4 == 0 else nb
    assert n_lat % tb == 0 and tb % HG_CHUNK == 0 and max(lora) <= LANE

    cc = jnp.concatenate([c, c_ctx[None, :]], axis=0)
    cc = jnp.pad(cc, ((0, -(bsz + 1) % SUBLANE), (0, 0)))
    mods = _mods(cc, w_mod, b_mod)
    mod_lat = mods[:, :bsz].reshape(depth, bsz, 6, d).transpose(0, 2, 1, 3)
    mod_ctx = jnp.broadcast_to(mods[:, bsz].reshape(depth, 6, 1, d), (depth, 6, bsz, d))
    modtab = jnp.stack([mod_ctx, mod_lat], axis=2)[:, :, :, :, None, :]

    lb_p = jax.nn.softmax(hg_lb.astype(F32), axis=0)
    lower = jnp.cumsum(lb_p, axis=0) - lb_p[0]

    hg_cols = 5 * hg_w
    rw_cols = 3 * rw_w + sum(lora)
    w_hg = w_in[:, :, :hg_cols].astype(BF16)
    w_rw_raw = w_in[:, :, hg_cols:hg_cols + rw_cols]
    w_ml = w_in[:, :, hg_cols + rw_cols:]
    offs = [3 * rw_w]
    for n in lora:
        offs.append(offs[-1] + n)

    def pad_lora(a):
        parts = [a[..., :3 * rw_w]] + [_pad_cols(a[..., offs[i]:offs[i + 1]], LANE) for i in range(5)]
        return jnp.concatenate(parts, axis=-1)

    w_rw = pad_lora(w_rw_raw).astype(BF16)
    mu = pad_lora(rw_mu)[:, None, :]
    w_ml = _pad_cols(w_ml, q_rank + kv_rank + LANE).astype(BF16)

    def pad_rows(a):
        return jnp.pad(a, ((0, 0),) * (a.ndim - 2) + ((0, LANE - a.shape[-2]), (0, 0)))

    w2 = pad_rows(rw_w2).astype(BF16)
    a2 = pad_rows(rw_a2).astype(BF16)
    g2 = pad_rows(rw_g2).astype(BF16)

    wq = mla_w_uq.reshape(depth, q_rank, n_heads, MLA_NOPE + MLA_ROPE)
    wq = _pad_cols(wq, MLA_NOPE + LANE).reshape(depth, q_rank, n_heads * (MLA_NOPE + LANE)).astype(BF16)
    wkv = mla_w_ukv.reshape(depth, kv_rank, n_heads, MLA_NOPE + MLA_V)
    wkv = jnp.concatenate([wkv[..., :MLA_NOPE].reshape(depth, kv_rank, -1),
                           wkv[..., MLA_NOPE:].reshape(depth, kv_rank, -1)], axis=-1).astype(BF16)
    cos, sin = _rope_tables(n_ctx, n_lat)

    w_out_b = w_out.astype(BF16)
    tk = _pick(d_ff, (512, 256, 128))
    up_b = ffn_up.astype(BF16).reshape(depth, d, 2, d_ff // tk, tk).transpose(0, 2, 3, 1, 4)
    down_b = ffn_down.astype(BF16)

    xs = jnp.concatenate([ctx, x], axis=1)
    for l in range(depth):
        gain0, gain1 = norm_g[l, 0:1], norm_g[l, 1:2]
        mt = modtab[l]
        h = _norm(xs, gain0, mt, tb, nb)
        p_hg = _in_proj(h, w_hg[l], tb, nb, hg_cols)
        p_rw = _in_proj(h, w_rw[l], tb, nb, w_rw.shape[-1])
        p_ml = _in_proj(h, w_ml[l], tb, nb_mm, w_ml.shape[-1])

        hg_f, hg_b = _hg_scan(p_hg, lower[l], tb)
        hg_o = _hg_finish(hg_f, hg_b, p_hg, hg_gn[l][None, :], tb)

        r, v, kk, w_f, k_f, b_f, w_b, k_b, b_b, g = _rw_prep(
            p_rw, mu[l], rw_w0[l], w2[l], rw_a0[l], a2[l], g2[l], rw_kk[l][None, :], rw_ka[l][None, :], tb)
        rw_f, rw_b = _rw_scan(r, v, kk, w_f, k_f, b_f, w_b, k_b, b_b, tb)
        rw_o = _rw_finish(rw_f, rw_b, r, k_f, k_b, v, g, rw_gn_w[l][None, :], rw_gn_b[l][None, :],
                          rw_rk[l].reshape(1, rw_w), tb)

        q, k, vv = _mla_prep(p_ml, cos, sin, mla_q_norm[l][None, :], mla_kv_norm[l][None, :], wq[l], wkv[l],
                             mla_qn_g[l][None, :], _pad_cols(mla_qr_g[l][None, :], LANE),
                             mla_kn_g[l][None, :], _pad_cols(mla_kr_g[l][None, :], LANE), tb)
        ml_o = _mla_attn(q, k, vv, tb)

        xs = _out_proj(xs, hg_o, rw_o, ml_o, w_out_b[l], mt, tb, nb, d)
        xs = _ffn(xs, gain1, mt, up_b[l], ffn_dw[l], ffn_db[l][None, :], down_b[l], tb, nb)
    return xs[:, n_ctx:]
```

```python
import functools
import math

import jax
import jax.numpy as jnp
from jax import lax
from jax.experimental import pallas as pl
from jax.experimental.pallas import tpu as pltpu

F32 = jnp.float32
BF16 = jnp.bfloat16

NORM_EPS = 1e-6
GRID_W = 64
ROPE_THETA = 10000.0
HG_HD = 128
HG_CHUNK = 16
RW_HD = 64
RW_GN_EPS = 64e-5
RW_DECAY_MAX = math.exp(-0.5)
MLA_V = 128
MLA_NOPE = 128
MLA_ROPE = 64
MLA_SCALE = (MLA_NOPE + MLA_ROPE) ** -0.5
LOG2E = math.log2(math.e)
LANE = 128
SUBLANE = 8
RW_CHUNK = 64
NORM_ROW_TILE = 16
ATTN_HEADS = 2
FFN_ROW_TILE = 32
VMEM_LIMIT = 56 * 1024 * 1024


def _cparams(*sem):
    return pltpu.CompilerParams(dimension_semantics=sem, vmem_limit_bytes=VMEM_LIMIT)


def _bdot(a, b):
    return jnp.dot(a.astype(BF16), b.astype(BF16), preferred_element_type=F32)


def _sigmoid(x):
    return 1.0 / (1.0 + jnp.exp(-x))


def _silu(x):
    return x * _sigmoid(x)


def _pad_cols(a, width):
    return jnp.pad(a, [(0, 0)] * (a.ndim - 1) + [(0, width - a.shape[-1])])


def _mods_kernel(c_ref, w_ref, b_ref, o_ref):
    o_ref[...] = _bdot(_silu(c_ref[...]), w_ref[...]) + b_ref[...]


def _mods(cc, w_mod, b_mod):
    n_layers, d, n = w_mod.shape
    rows = cc.shape[0]
    tn = 768 if n % 768 == 0 else n
    return pl.pallas_call(
        _mods_kernel,
        out_shape=jax.ShapeDtypeStruct((n_layers, rows, n), F32),
        grid=(n_layers, n // tn),
        in_specs=[
            pl.BlockSpec((rows, d), lambda l, j: (0, 0)),
            pl.BlockSpec((None, d, tn), lambda l, j: (l, 0, j)),
            pl.BlockSpec((None, 1, tn), lambda l, j: (l, 0, j)),
        ],
        out_specs=pl.BlockSpec((None, rows, tn), lambda l, j: (l, 0, j)),
        compiler_params=_cparams("parallel", "parallel"),
    )(cc, w_mod, b_mod.reshape(n_layers, 1, n))


def _norm_mod(x, gain, shift, scale):
    y = x * lax.rsqrt(jnp.mean(x * x, axis=-1, keepdims=True) + NORM_EPS) * gain
    return y * (1.0 + scale) + shift


def _norm_mod_rows(x_ref, h_ref, g_ref, sh_ref, sc_ref):
    nb, tb, _ = x_ref.shape
    rt = NORM_ROW_TILE
    for n in range(nb):
        mult = g_ref[...] * (1.0 + sc_ref[n])
        shift = sh_ref[n]

        def tile(i, carry, n=n, mult=mult, shift=shift):
            r = pl.multiple_of(i * rt, rt)
            x = x_ref[n, pl.ds(r, rt), :]
            rs = lax.rsqrt(jnp.mean(x * x, axis=-1, keepdims=True) + NORM_EPS)
            h = (x * rs * mult + shift).astype(h_ref.dtype)
            if len(h_ref.shape) == 3:
                h_ref[n, pl.ds(r, rt), :] = h
            else:
                h_ref[pl.ds(pl.multiple_of(n * tb + r, rt), rt), :] = h
            return carry

        lax.fori_loop(0, tb // rt, tile, 0, unroll=4)


def _mod_spec(m, nb, d):
    return pl.BlockSpec((None, None, nb, 1, d), lambda bi, j: (m, jnp.minimum(j, 1), bi, 0, 0))


def _norm_kernel(x_ref, g_ref, sh_ref, sc_ref, h_ref):
    _norm_mod_rows(x_ref, h_ref, g_ref, sh_ref, sc_ref)


def _norm(x, gain, modtab, tb, nb):
    b, t, d = x.shape
    return pl.pallas_call(
        _norm_kernel,
        out_shape=jax.ShapeDtypeStruct((b, t, d), BF16),
        grid=(b // nb, t // tb),
        in_specs=[
            pl.BlockSpec((nb, tb, d), lambda bi, j: (bi, j, 0)),
            pl.BlockSpec((1, d), lambda bi, j: (0, 0)),
            _mod_spec(0, nb, d),
            _mod_spec(1, nb, d),
        ],
        out_specs=pl.BlockSpec((nb, tb, d), lambda bi, j: (bi, j, 0)),
        compiler_params=_cparams("parallel", "parallel"),
    )(x, gain, modtab, modtab)


def _in_proj_kernel(h_ref, w_ref, o_ref):
    nb, tb, d = h_ref.shape
    o_ref[...] = jnp.dot(h_ref[...].reshape(nb * tb, d), w_ref[...],
                         preferred_element_type=F32).reshape(o_ref.shape)


def _in_proj(h, w, tb, nb, tn):
    b, t, d = h.shape
    n = w.shape[1]
    return pl.pallas_call(
        _in_proj_kernel,
        out_shape=jax.ShapeDtypeStruct((b, t, n), F32),
        grid=(b // nb, t // tb, n // tn),
        in_specs=[
            pl.BlockSpec((nb, tb, d), lambda bi, j, k: (bi, j, 0)),
            pl.BlockSpec((d, tn), lambda bi, j, k: (0, k)),
        ],
        out_specs=pl.BlockSpec((nb, tb, tn), lambda bi, j, k: (bi, j, k)),
        compiler_params=_cparams("parallel", "parallel", "arbitrary"),
    )(h, w)


def _out_proj_kernel(x_ref, hg_ref, rw_ref, ml_ref, w_ref, gate_ref, o_ref):
    nb, tb, tn = o_ref.shape
    mixed = jnp.concatenate([hg_ref[...], rw_ref[...], ml_ref[...]], axis=-1)
    acc = jnp.dot(mixed.reshape(nb * tb, mixed.shape[-1]), w_ref[...], preferred_element_type=F32)
    o_ref[...] = x_ref[...] + gate_ref[...] * acc.reshape(nb, tb, tn)


def _out_proj(x, hg, rw, ml, w, modtab, tb, nb, tn, j0):
    b, t, d = x.shape
    return pl.pallas_call(
        _out_proj_kernel,
        out_shape=jax.ShapeDtypeStruct((b, t - j0 * tb, d), F32),
        grid=(b // nb, t // tb - j0, d // tn),
        in_specs=[
            pl.BlockSpec((nb, tb, tn), lambda bi, j, k: (bi, j + j0, k)),
            pl.BlockSpec((nb, tb, hg.shape[-1]), lambda bi, j, k: (bi, j + j0, 0)),
            pl.BlockSpec((nb, tb, rw.shape[-1]), lambda bi, j, k: (bi, j + j0, 0)),
            pl.BlockSpec((nb, tb, ml.shape[-1]), lambda bi, j, k: (bi, j + j0, 0)),
            pl.BlockSpec((d, tn), lambda bi, j, k: (0, k)),
            pl.BlockSpec((None, None, nb, 1, tn), lambda bi, j, k: (2, jnp.minimum(j + j0, 1), bi, 0, k)),
        ],
        out_specs=pl.BlockSpec((nb, tb, tn), lambda bi, j, k: (bi, j, k)),
        compiler_params=_cparams("parallel", "parallel", "arbitrary"),
    )(x, hg, rw, ml, w, modtab)


def _shift_rows(u, tb, first_rows, last_rows):
    rows = u.shape[0]
    row = lax.broadcasted_iota(jnp.int32, u.shape, 0)
    prev = pltpu.roll(u, 1, 0)
    nxt = pltpu.roll(u, rows - 1, 0)
    for n, (fr, lr) in enumerate(zip(first_rows, last_rows)):
        prev = jnp.where(row == n * tb, fr, prev)
        nxt = jnp.where(row == n * tb + tb - 1, lr, nxt)
    return prev, nxt


def _ffn_kernel(ctx_blocks, x_ref, xp_ref, xn_ref, g_ref, sh_ref, sc_ref, gate_ref, wa_ref, wb_ref, dwa_ref,
                dwb_ref, dba_ref, dbb_ref, wd_ref, o_ref, h_ref, ua_ref, ub_ref, act_ref):
    nb, tb, d = x_ref.shape
    rows = nb * tb
    halo = 2 * SUBLANE
    rt = FFN_ROW_TILE
    j = pl.program_id(1)
    k = pl.program_id(2)
    n_k = pl.num_programs(2)
    n_j = pl.num_programs(1)
    trow = lax.broadcasted_iota(jnp.int32, (rt, act_ref.shape[1]), 0)

    def conv(u_ref, dw, db, n, r):
        base = n * tb + r
        mid = u_ref[base:base + rt, :]
        if r == 0:
            edge = u_ref[rows + n * halo + SUBLANE - 1:rows + n * halo + SUBLANE, :]
            prev = jnp.where(trow == 0, edge, pltpu.roll(mid, 1, 0))
        else:
            prev = u_ref[base - 1:base - 1 + rt, :]
        if r == tb - rt:
            edge = u_ref[rows + n * halo + SUBLANE:rows + n * halo + SUBLANE + 1, :]
            nxt = jnp.where(trow == rt - 1, edge, pltpu.roll(mid, rt - 1, 0))
        else:
            nxt = u_ref[base + 1:base + 1 + rt, :]
        return prev * dw[0:1] + mid * dw[1:2] + nxt * dw[2:3] + db

    @pl.when(k == 0)
    def _():
        p_ok = (j >= ctx_blocks + 1).astype(F32)
        n_ok = jnp.logical_and(j >= ctx_blocks, j < n_j - 1).astype(F32)
        g, sh, sc = g_ref[...], sh_ref[...], sc_ref[...]
        _norm_mod_rows(x_ref, h_ref, g_ref, sh_ref, sc_ref)
        hp = (_norm_mod(xp_ref[...], g, sh, sc) * p_ok).astype(BF16)
        hn = (_norm_mod(xn_ref[...], g, sh, sc) * n_ok).astype(BF16)
        for n in range(nb):
            h_ref[rows + n * halo:rows + n * halo + SUBLANE, :] = hp[n]
            h_ref[rows + n * halo + SUBLANE:rows + (n + 1) * halo, :] = hn[n]
        o_ref[...] = jnp.zeros_like(o_ref)

    ua_ref[...] = jnp.dot(h_ref[...], wa_ref[...], preferred_element_type=F32)
    ub_ref[...] = jnp.dot(h_ref[...], wb_ref[...], preferred_element_type=F32)
    dwa, dwb, dba, dbb = dwa_ref[...], dwb_ref[...], dba_ref[...], dbb_ref[...]
    for n in range(nb):
        for r in range(0, tb, rt):
            a = conv(ua_ref, dwa, dba, n, r)
            b = conv(ub_ref, dwb, dbb, n, r)
            act_ref[n * tb + r:n * tb + r + rt, :] = (_silu(a) * b).astype(BF16)
    o_ref[...] += jnp.dot(act_ref[...], wd_ref[...], preferred_element_type=F32).reshape(nb, tb, d)

    @pl.when(k == n_k - 1)
    def _():
        o_ref[...] = x_ref[...] + gate_ref[...] * o_ref[...]


def _ffn(x, gain, modtab, w_up, dw, db, w_down, tb, nb, tk, ctx_blocks):
    b, t, d = x.shape
    n_k = w_down.shape[0] // tk
    r8 = tb // SUBLANE
    n8 = t // SUBLANE
    rows_ext = nb * tb + nb * 2 * SUBLANE

    def mspec(m):
        return pl.BlockSpec((None, None, nb, 1, d),
                            lambda bi, j, k: (m, jnp.minimum(j, 1) if ctx_blocks else 1, bi, 0, 0))

    return pl.pallas_call(
        functools.partial(_ffn_kernel, ctx_blocks),
        out_shape=jax.ShapeDtypeStruct((b, t, d), F32),
        grid=(b // nb, t // tb, n_k),
        in_specs=[
            pl.BlockSpec((nb, tb, d), lambda bi, j, k: (bi, j, 0)),
            pl.BlockSpec((nb, SUBLANE, d), lambda bi, j, k: (bi, jnp.maximum(j * r8 - 1, 0), 0)),
            pl.BlockSpec((nb, SUBLANE, d), lambda bi, j, k: (bi, jnp.minimum((j + 1) * r8, n8 - 1), 0)),
            pl.BlockSpec((1, d), lambda bi, j, k: (0, 0)),
            mspec(3), mspec(4), mspec(5),
            pl.BlockSpec((d, tk), lambda bi, j, k: (0, k)),
            pl.BlockSpec((d, tk), lambda bi, j, k: (0, n_k + k)),
            pl.BlockSpec((3, tk), lambda bi, j, k: (0, k)),
            pl.BlockSpec((3, tk), lambda bi, j, k: (0, n_k + k)),
            pl.BlockSpec((1, tk), lambda bi, j, k: (0, k)),
            pl.BlockSpec((1, tk), lambda bi, j, k: (0, n_k + k)),
            pl.BlockSpec((tk, d), lambda bi, j, k: (k, 0)),
        ],
        out_specs=pl.BlockSpec((nb, tb, d), lambda bi, j, k: (bi, j, 0)),
        scratch_shapes=[pltpu.VMEM((rows_ext, d), BF16), pltpu.VMEM((rows_ext, tk), F32),
                        pltpu.VMEM((rows_ext, tk), F32), pltpu.VMEM((nb * tb, tk), BF16)],
        compiler_params=_cparams("parallel", "parallel", "arbitrary"),
    )(x, x, x, gain, modtab, modtab, modtab, w_up, w_up, dw, dw, db, db, w_down)


def _chunk_cumsum(x, c, reverse):
    rows = x.shape[0]
    ri = lax.broadcasted_iota(jnp.int32, (rows, rows), 0)
    ci = lax.broadcasted_iota(jnp.int32, (rows, rows), 1)
    tri = jnp.logical_and((ri // c) == (ci // c), (ci >= ri) if reverse else (ci <= ri)).astype(BF16)
    hi = x.astype(BF16)
    rest = x - hi.astype(F32)
    mid = rest.astype(BF16)
    lo = (rest - mid.astype(F32)).astype(BF16)
    return (jnp.dot(tri, hi, preferred_element_type=F32) + jnp.dot(tri, mid, preferred_element_type=F32)
            + jnp.dot(tri, lo, preferred_element_type=F32))


def _bwd_block(j, n_j):
    return jnp.where(j == 0, 0, n_j - j)


def _hg_scan_kernel(qf_ref, zf_ref, if_ref, qb_ref, zb_ref, ib_ref, lb_ref, of_ref, ob_ref,
                    st_ref, g_ref, k_ref):
    tb, w = zf_ref.shape
    n_heads = w // HG_HD
    n_chunks = tb // HG_CHUNK
    c = HG_CHUNK

    @pl.when(pl.program_id(1) == 0)
    def _():
        st_ref[...] = jnp.zeros_like(st_ref)

    ones = jnp.ones((HG_HD, HG_HD), BF16)
    trow = lax.broadcasted_iota(jnp.int32, (c, HG_HD), 0)

    dirs = ((qf_ref, zf_ref, if_ref, of_ref), (qb_ref, zb_ref, ib_ref, ob_ref))
    for dr, (q_ref, z_ref, i_ref, o_ref) in enumerate(dirs):
        lb = lb_ref[dr:dr + 1, :]
        f = lb + (1.0 - lb) * _sigmoid(z_ref[...])
        g_ref[dr] = _chunk_cumsum(jnp.log2(f), c, dr == 1)
        k_ref[dr] = 1.0 - f

    def dir_chunk(dr, r0):
        q_ref, _z_ref, i_ref, o_ref = dirs[dr]
        heads = []
        tiles = []
        for h in range(n_heads):
            ls = slice(h * HG_HD, (h + 1) * HG_HD)
            q = q_ref[pl.ds(r0, c), ls]
            v = i_ref[pl.ds(r0, c), ls]
            g = g_ref[dr, pl.ds(r0, c), ls]
            kk = k_ref[dr, pl.ds(r0, c), ls]
            heads.append((ls, q, v, g, kk))
            for s in range(c):
                seen = (trow >= s) if dr == 0 else (trow <= s)
                tiles.append(jnp.where(seen, q * kk[s:s + 1] * jnp.exp2(g - g[s:s + 1]), 0.0).astype(BF16))
        attn = jnp.dot(jnp.concatenate(tiles, axis=0), ones, preferred_element_type=F32)
        inter = []
        for h, (ls, q, v, g, kk) in enumerate(heads):
            g_last = g[c - 1:c] if dr == 0 else g[0:1]
            st = st_ref[dr, h]
            inter.append(lax.dot_general((q * jnp.exp2(g)).astype(BF16), st.astype(BF16),
                                         (((1,), (1,)), ((), ())), preferred_element_type=F32))
            kd = kk * jnp.exp2(g_last - g)
            st_ref[dr, h] = st * jnp.exp2(g_last) + lax.dot_general(
                v.astype(BF16), kd.astype(BF16), (((0,), (0,)), ((), ())), preferred_element_type=F32)
        yield
        for h, (ls, q, v, g, kk) in enumerate(heads):
            o = inter[h]
            for s in range(c):
                o += attn[(h * c + s) * c:(h * c + s + 1) * c] * v[s:s + 1]
            o_ref[pl.ds(r0, c), ls] = o

    def chunk(ic, carry):
        live = [dir_chunk(dr, pl.multiple_of((ic if dr == 0 else n_chunks - 1 - ic) * c, c)) for dr in range(2)]
        while live:
            live = [gen for gen in live if next(gen, True) is None]
        return carry

    lax.fori_loop(0, n_chunks, chunk, 0)


def _hg_scan(p_hg, lb, tb):
    b, t, _ = p_hg.shape
    w = lb.shape[-1]
    n_j = t // tb

    def fwd(col):
        return pl.BlockSpec((None, tb, w), lambda bi, j: (bi, j, col))

    def bwd(col):
        return pl.BlockSpec((None, tb, w), lambda bi, j: (bi, _bwd_block(j, n_j), col))

    out = jax.ShapeDtypeStruct((b, t, w), F32)
    return pl.pallas_call(
        _hg_scan_kernel,
        out_shape=(out, out),
        grid=(b, n_j),
        in_specs=[fwd(0), fwd(1), fwd(3), bwd(0), bwd(2), bwd(3), pl.BlockSpec((2, w), lambda bi, j: (0, 0))],
        out_specs=(fwd(0), bwd(0)),
        scratch_shapes=[pltpu.VMEM((2, w // HG_HD, HG_HD, HG_HD), F32), pltpu.VMEM((2, tb, w), F32),
                        pltpu.VMEM((2, tb, w), F32)],
        compiler_params=_cparams("parallel", "arbitrary"),
    )(p_hg, p_hg, p_hg, p_hg, p_hg, p_hg, lb)


def _hg_finish_kernel(of_ref, ob_ref, gate_ref, gn_ref, o_ref):
    w = of_ref.shape[-1]
    for h in range(w // HG_HD):
        ls = slice(h * HG_HD, (h + 1) * HG_HD)
        o = of_ref[:, ls] + ob_ref[:, ls]
        o = o * lax.rsqrt(jnp.mean(o * o, axis=-1, keepdims=True) + NORM_EPS) * gn_ref[...]
        o_ref[:, ls] = (o * _silu(gate_ref[:, ls])).astype(o_ref.dtype)


def _hg_finish(o_f, o_b, p_hg, gn, tb):
    b, t, w = o_f.shape
    spec = pl.BlockSpec((None, tb, w), lambda bi, j: (bi, j, 0))
    return pl.pallas_call(
        _hg_finish_kernel,
        out_shape=jax.ShapeDtypeStruct((b, t, w), BF16),
        grid=(b, t // tb),
        in_specs=[spec, spec, pl.BlockSpec((None, tb, w), lambda bi, j: (bi, j, 4)),
                  pl.BlockSpec((1, HG_HD), lambda bi, j: (0, 0))],
        out_specs=spec,
        compiler_params=_cparams("parallel", "parallel"),
    )(o_f, o_b, p_hg, gn)


def _head_ones(width, hd):
    r = lax.broadcasted_iota(jnp.int32, (width, width), 0) // hd
    c = lax.broadcasted_iota(jnp.int32, (width, width), 1) // hd
    return (r == c).astype(F32)


def _head_sum(x, hd):
    same_head = _head_ones(LANE, hd).astype(BF16)
    out = []
    for i in range(x.shape[-1] // LANE):
        xs = x[:, i * LANE:(i + 1) * LANE]
        hi = xs.astype(BF16)
        lo = (xs - hi.astype(F32)).astype(BF16)
        out.append(jnp.dot(hi, same_head, preferred_element_type=F32)
                   + jnp.dot(lo, same_head, preferred_element_type=F32))
    return jnp.concatenate(out, axis=-1)


def _rw_prep_kernel(p_ref, pp_ref, pn_ref, mu_ref, w0_ref, w2_ref, a0_ref, a2_ref, g2_ref, kk_ref,
                    ka_ref, r_o, v_o, kk_o, wf_o, kf_o, bf_o, wb_o, kb_o, bb_o, g_o):
    tb = p_ref.shape[0]
    w = r_o.shape[-1]
    lw = w2_ref.shape[1]
    j = pl.program_id(1)
    n_j = pl.num_programs(1)
    p = p_ref[...]
    p_ok = (j >= 2).astype(F32)
    n_ok = jnp.logical_and(j >= 1, j < n_j - 1).astype(F32)
    prev, nxt = _shift_rows(p, tb, [pp_ref[SUBLANE - 1:SUBLANE, :] * p_ok], [pn_ref[0:1, :] * n_ok])
    s = p + mu_ref[...] * (0.5 * (prev + nxt) - p)
    r, k, v = s[:, 0:w], s[:, w:2 * w], s[:, 2 * w:3 * w]
    lora = [s[:, 3 * w + i * lw:3 * w + (i + 1) * lw] for i in range(5)]
    kk = k * kk_ref[...]
    ssq = _head_sum(kk * kk, RW_HD)
    kk = kk / jnp.maximum(jnp.sqrt(ssq), 1e-12)
    r_o[...] = r
    v_o[...] = v
    kk_o[...] = kk
    for dr, (w_o, k_o, b_o) in enumerate(((wf_o, kf_o, bf_o), (wb_o, kb_o, bb_o))):
        xw, xa = lora[dr], lora[2 + dr]
        w_o[...] = -RW_DECAY_MAX * _sigmoid(w0_ref[dr:dr + 1, :] + _bdot(jnp.tanh(xw), w2_ref[dr]))
        a = _sigmoid(a0_ref[dr:dr + 1, :] + _bdot(xa, a2_ref[dr]))
        k_o[...] = k * (1.0 + (a - 1.0) * ka_ref[...])
        b_o[...] = kk * a
    g_o[...] = _bdot(_sigmoid(lora[4]), g2_ref[...])


def _rw_prep(p_rw, mu, w0, w2, a0, a2, g2, k_k, k_a, tb):
    b, t, n = p_rw.shape
    w = w0.shape[-1]
    r8, n8 = tb // SUBLANE, t // SUBLANE
    out = jax.ShapeDtypeStruct((b, t, w), F32)
    ospec = pl.BlockSpec((None, tb, w), lambda bi, j: (bi, j, 0))

    def const(a):
        return pl.BlockSpec(a.shape, lambda bi, j: (0,) * a.ndim)

    return pl.pallas_call(
        _rw_prep_kernel,
        out_shape=(out,) * 10,
        grid=(b, t // tb),
        in_specs=[
            pl.BlockSpec((None, tb, n), lambda bi, j: (bi, j, 0)),
            pl.BlockSpec((None, SUBLANE, n), lambda bi, j: (bi, jnp.maximum(j * r8 - 1, 0), 0)),
            pl.BlockSpec((None, SUBLANE, n), lambda bi, j: (bi, jnp.minimum((j + 1) * r8, n8 - 1), 0)),
            const(mu), const(w0), const(w2), const(a0), const(a2), const(g2), const(k_k), const(k_a),
        ],
        out_specs=(ospec,) * 10,
        compiler_params=_cparams("parallel", "parallel"),
    )(p_rw, p_rw, p_rw, mu, w0, w2, a0, a2, g2, k_k, k_a)


def _rw_scan_kernel(rf_ref, vf_ref, kkf_ref, wf_ref, kf_ref, bf_ref, rb_ref, vb_ref, kkb_ref, wb_ref,
                    kb_ref, bb_ref, of_ref, ob_ref, st_ref, g_ref):
    tb, w = rf_ref.shape
    n_pairs = w // LANE
    hd = RW_HD
    c = RW_CHUNK
    n_chunks = tb // c
    assert c == hd and tb % c == 0

    @pl.when(pl.program_id(1) == 0)
    def _():
        st_ref[...] = jnp.zeros_like(st_ref)

    dirs = ((rf_ref, vf_ref, kkf_ref, wf_ref, kf_ref, bf_ref, of_ref),
            (rb_ref, vb_ref, kkb_ref, wb_ref, kb_ref, bb_ref, ob_ref))
    for dr, refs in enumerate(dirs):
        g_ref[dr] = _chunk_cumsum(refs[3][...], c, dr == 1)

    row = lax.broadcasted_iota(jnp.int32, (c, LANE), 0)
    col = lax.broadcasted_iota(jnp.int32, (c, LANE), 1) % hd
    head_a = lax.broadcasted_iota(jnp.int32, (c, LANE), 1) < hd
    head_a2 = (lax.broadcasted_iota(jnp.int32, (c, 2 * LANE), 1) % LANE) < hd
    r2 = lax.broadcasted_iota(jnp.int32, (LANE, LANE), 0)
    c2 = lax.broadcasted_iota(jnp.int32, (LANE, LANE), 1)
    same_head = (r2 // hd) == (c2 // hd)
    eye = r2 == c2

    def stack2(y):
        m = head_a if y.shape[1] == LANE else head_a2
        return jnp.concatenate([jnp.where(m, y, 0.0), jnp.where(m, 0.0, y)], axis=0).astype(BF16)

    def tn(x, y):
        return lax.dot_general(x.astype(BF16), y.astype(BF16), (((0,), (0,)), ((), ())),
                               preferred_element_type=F32)

    def pair_chunk(dr, hp, r0):
        refs = dirs[dr]
        before = (col < row) if dr == 0 else (col > row)
        upto = (col <= row) if dr == 0 else (col >= row)
        ls = slice(hp * LANE, (hp + 1) * LANE)
        rr, vv, kk, lw, kd, bb = [ref[pl.ds(r0, c), ls] for ref in refs[:6]]
        g = g_ref[dr, pl.ds(r0, c), ls]
        g_end = g[c - 1:c] if dr == 0 else g[0:1]
        at = -kk * jnp.exp(g - lw)
        rt = rr * jnp.exp(g)
        e_inv = jnp.exp(-g)
        e_out = jnp.exp(g_end - g)
        bh_kh = jnp.concatenate([bb * e_out, kd * e_out], axis=0)
        abk = lax.dot_general(jnp.concatenate([at, rt], axis=0).astype(BF16),
                              jnp.concatenate([stack2(bb * e_inv), stack2(kd * e_inv)], axis=0),
                              (((1,), (1,)), ((), ())), preferred_element_type=F32)
        yield
        x = jnp.where(before, abk[0:c, 0:LANE], 0.0)
        a_ak = jnp.where(before, abk[0:c, LANE:], 0.0)
        a_r = jnp.concatenate([jnp.where(upto, abk[c:, 0:LANE], 0.0), jnp.where(upto, abk[c:, LANE:], 0.0)], axis=1)
        u0 = jnp.dot(a_ak.astype(BF16), stack2(vv), preferred_element_type=F32)
        yield
        y = jnp.concatenate([at, u0], axis=1)
        span = 1
        while span < c:
            span *= 2
            rhs = stack2(y) if span >= c else jnp.concatenate([stack2(y), stack2(x)], axis=1)
            z = jnp.dot(x.astype(BF16), rhs, preferred_element_type=F32)
            yield
            y = y + z[:, 0:2 * LANE]
            if span < c:
                x = z[:, 2 * LANE:]
        zero = jnp.zeros((c, LANE), F32)
        v_pad = jnp.concatenate([zero, vv], axis=1)
        wu = jnp.dot(a_r.astype(BF16), jnp.concatenate([stack2(y), stack2(v_pad)], axis=0),
                     preferred_element_type=F32)
        mn = tn(bh_kh, jnp.concatenate([y, v_pad], axis=0))
        yield
        q_eff = rt + wu[:, 0:LANE]
        m_bd = jnp.where(same_head, mn[:, 0:LANE], 0.0) + jnp.where(eye, jnp.exp(g_end), 0.0)
        n_bd = jnp.where(same_head, mn[:, LANE:], 0.0)
        st = st_ref[dr * n_pairs + hp].astype(BF16)
        so = jnp.dot(jnp.concatenate([q_eff, m_bd], axis=0).astype(BF16), st, preferred_element_type=F32)
        yield
        refs[6][pl.ds(r0, c), ls] = so[0:c] + wu[:, LANE:]
        st_ref[dr * n_pairs + hp] = so[c:] + n_bd

    def chunk(ic, carry):
        live = [pair_chunk(dr, hp, pl.multiple_of((ic if dr == 0 else n_chunks - 1 - ic) * c, c))
                for dr in range(2) for hp in range(n_pairs)]
        while live:
            live = [gen for gen in live if next(gen, True) is None]
        return carry

    lax.fori_loop(0, n_chunks, chunk, 0)


def _rw_scan(r, v, kk, w_f, k_f, b_f, w_b, k_b, b_b, tb):
    b, t, w = r.shape
    n_j = t // tb
    fwd = pl.BlockSpec((None, tb, w), lambda bi, j: (bi, j, 0))
    bwd = pl.BlockSpec((None, tb, w), lambda bi, j: (bi, _bwd_block(j, n_j), 0))
    out = jax.ShapeDtypeStruct((b, t, w), F32)
    return pl.pallas_call(
        _rw_scan_kernel,
        out_shape=(out, out),
        grid=(b, n_j),
        in_specs=[fwd] * 6 + [bwd] * 6,
        out_specs=(fwd, bwd),
        scratch_shapes=[pltpu.VMEM((2 * (w // LANE), LANE, LANE), F32), pltpu.VMEM((2, tb, w), F32)],
        compiler_params=_cparams("parallel", "arbitrary"),
    )(r, v, kk, w_f, k_f, b_f, r, v, kk, w_b, k_b, b_b)


def _rw_finish_kernel(of_ref, ob_ref, r_ref, kf_ref, kb_ref, v_ref, g_ref, gnw_ref, gnb_ref, rk_ref, o_ref):
    o = of_ref[...] + ob_ref[...]
    cen = o - _head_sum(o, RW_HD) * (1.0 / RW_HD)
    var = _head_sum(cen * cen, RW_HD) * (1.0 / RW_HD)
    o = cen * lax.rsqrt(var + RW_GN_EPS) * gnw_ref[...] + gnb_ref[...]
    dot_rk = _head_sum(r_ref[...] * (kf_ref[...] + kb_ref[...]) * rk_ref[...], RW_HD)
    o_ref[...] = ((o + dot_rk * v_ref[...]) * g_ref[...]).astype(o_ref.dtype)


def _rw_finish(o_f, o_b, r, k_f, k_b, v, g, gn_w, gn_b, r_k, tb):
    b, t, w = o_f.shape
    spec = pl.BlockSpec((None, tb, w), lambda bi, j: (bi, j, 0))
    cspec = pl.BlockSpec((1, w), lambda bi, j: (0, 0))
    return pl.pallas_call(
        _rw_finish_kernel,
        out_shape=jax.ShapeDtypeStruct((b, t, w), BF16),
        grid=(b, t // tb),
        in_specs=[spec] * 7 + [cspec] * 3,
        out_specs=spec,
        compiler_params=_cparams("parallel", "parallel"),
    )(o_f, o_b, r, k_f, k_b, v, g, gn_w, gn_b, r_k)


def _rms(x, gain, n):
    return x * lax.rsqrt(jnp.sum(x * x, axis=-1, keepdims=True) * (1.0 / n) + NORM_EPS) * gain


def _rope128(x, cos, sin):
    half = MLA_ROPE // 2
    lane = lax.broadcasted_iota(jnp.int32, x.shape, 1)
    swapped = jnp.where(lane < half, pltpu.roll(x, LANE - half, 1), pltpu.roll(x, half, 1))
    return x * cos + swapped * sin


def _mla_prep_kernel(p_ref, cos_ref, sin_ref, qn_ref, kvn_ref, wq_ref, wkv_ref, qng_ref, qrg_ref,
                     kng_ref, krg_ref, q_o, k_o, v_o):
    q_rank = qn_ref.shape[-1]
    kv_rank = kvn_ref.shape[-1]
    n_heads = v_o.shape[-1] // MLA_V
    slot = MLA_NOPE + LANE
    cos, sin = cos_ref[...], sin_ref[...]
    p = p_ref[...]
    q = _bdot(_rms(p[:, 0:q_rank], qn_ref[...], q_rank), wq_ref[...])
    kv = _bdot(_rms(p[:, q_rank:q_rank + kv_rank], kvn_ref[...], kv_rank), wkv_ref[...])
    k_rope = _rope128(_rms(p[:, q_rank + kv_rank:], krg_ref[...], MLA_ROPE), cos, sin).astype(k_o.dtype)
    for h in range(n_heads):
        q_nope = _rms(q[:, h * slot:h * slot + MLA_NOPE], qng_ref[...], MLA_NOPE)
        q_rope = _rope128(_rms(q[:, h * slot + MLA_NOPE:(h + 1) * slot], qrg_ref[...], MLA_ROPE), cos, sin)
        q_o[:, h * slot:h * slot + MLA_NOPE] = (q_nope * (MLA_SCALE * LOG2E)).astype(q_o.dtype)
        q_o[:, h * slot + MLA_NOPE:(h + 1) * slot] = (q_rope * (MLA_SCALE * LOG2E)).astype(q_o.dtype)
        k_nope = _rms(kv[:, h * MLA_NOPE:(h + 1) * MLA_NOPE], kng_ref[...], MLA_NOPE)
        k_o[:, h * slot:h * slot + MLA_NOPE] = k_nope.astype(k_o.dtype)
        k_o[:, h * slot + MLA_NOPE:(h + 1) * slot] = k_rope
    v_o[...] = kv[:, n_heads * MLA_NOPE:].astype(v_o.dtype)


def _mla_prep(p_ml, cos, sin, q_norm, kv_norm, w_uq, w_ukv, qn_g, qr_g, kn_g, kr_g, tb):
    b, t, n = p_ml.shape
    n_heads = w_uq.shape[1] // (MLA_NOPE + LANE)

    def const(a):
        return pl.BlockSpec(a.shape, lambda bi, j: (0,) * a.ndim)

    def out(width):
        return (jax.ShapeDtypeStruct((b, t, width), BF16), pl.BlockSpec((None, tb, width), lambda bi, j: (bi, j, 0)))

    outs = [out(n_heads * (MLA_NOPE + LANE)), out(n_heads * (MLA_NOPE + LANE)), out(n_heads * MLA_V)]
    return pl.pallas_call(
        _mla_prep_kernel,
        out_shape=tuple(o[0] for o in outs),
        grid=(b, t // tb),
        in_specs=[
            pl.BlockSpec((None, tb, n), lambda bi, j: (bi, j, 0)),
            pl.BlockSpec((tb, LANE), lambda bi, j: (j, 0)),
            pl.BlockSpec((tb, LANE), lambda bi, j: (j, 0)),
            const(q_norm), const(kv_norm), const(w_uq), const(w_ukv), const(qn_g), const(qr_g),
            const(kn_g), const(kr_g),
        ],
        out_specs=tuple(o[1] for o in outs),
        compiler_params=_cparams("parallel", "parallel"),
    )(p_ml, cos, sin, q_norm, kv_norm, w_uq, w_ukv, qn_g, qr_g, kn_g, kr_g)


def _mla_attn_kernel(q_ref, k_ref, v_ref, o_ref):
    tb = q_ref.shape[0]
    slot = MLA_NOPE + LANE
    n_heads = o_ref.shape[1] // MLA_V

    def head(h, n_keys):
        qs = slice(h * slot, (h + 1) * slot)
        vs = slice(h * MLA_V, (h + 1) * MLA_V)
        s = lax.dot_general(q_ref[:, qs], k_ref[0:n_keys, qs], (((1,), (1,)), ((), ())),
                            preferred_element_type=F32)
        yield
        e = jnp.exp2(s - jnp.max(s, axis=-1, keepdims=True))
        o = jnp.dot(e.astype(BF16), v_ref[0:n_keys, vs], preferred_element_type=F32)
        yield
        o_ref[:, vs] = (o / jnp.sum(e, axis=-1, keepdims=True)).astype(o_ref.dtype)

    def attend(n_keys):
        live = [head(h, n_keys) for h in range(n_heads)]
        while live:
            live = [gen for gen in live if next(gen, True) is None]

    @pl.when(pl.program_id(2) == 0)
    def _():
        attend(tb)

    @pl.when(pl.program_id(2) > 0)
    def _():
        attend(k_ref.shape[0])


def _mla_attn(q, k, v, tb):
    b, t, _ = q.shape
    n_heads = v.shape[-1] // MLA_V
    slot = MLA_NOPE + LANE
    hg = ATTN_HEADS if n_heads % ATTN_HEADS == 0 else 1
    return pl.pallas_call(
        _mla_attn_kernel,
        out_shape=jax.ShapeDtypeStruct((b, t, n_heads * MLA_V), BF16),
        grid=(b, n_heads // hg, t // tb),
        in_specs=[
            pl.BlockSpec((None, tb, hg * slot), lambda bi, h, j: (bi, j, h)),
            pl.BlockSpec((None, t, hg * slot), lambda bi, h, j: (bi, 0, h)),
            pl.BlockSpec((None, t, hg * MLA_V), lambda bi, h, j: (bi, 0, h)),
        ],
        out_specs=pl.BlockSpec((None, tb, hg * MLA_V), lambda bi, h, j: (bi, j, h)),
        compiler_params=_cparams("parallel", "parallel", "arbitrary"),
    )(q, k, v)


def _rope_tables(n_ctx, n_lat):
    rows = n_lat // GRID_W
    row = jnp.repeat(jnp.arange(rows, dtype=F32), GRID_W)
    col = jnp.tile(jnp.arange(GRID_W, dtype=F32), rows)
    axis_dim = MLA_ROPE // 2
    inv_freq = ROPE_THETA ** (-jnp.arange(0, axis_dim, 2, dtype=F32) / axis_dim)
    ang = jnp.concatenate([row[:, None] * inv_freq, col[:, None] * inv_freq], axis=-1)
    ang = jnp.concatenate([jnp.zeros((n_ctx, axis_dim), F32), ang], axis=0)
    cos, sin = jnp.cos(ang), jnp.sin(ang)
    return (_pad_cols(jnp.concatenate([cos, cos], axis=-1), LANE),
            _pad_cols(jnp.concatenate([-sin, sin], axis=-1), LANE))


def _pick(n, prefs):
    for p in prefs:
        if n % p == 0:
            return p
    return n


def kernel(x, c, ctx, c_ctx, norm_g, w_mod, b_mod, w_in, w_out, hg_lb, hg_gn, rw_mu, rw_w0, rw_w2, rw_a0, rw_a2, rw_g2, rw_kk, rw_ka, rw_rk, rw_gn_w, rw_gn_b, mla_q_norm, mla_w_uq, mla_kv_norm, mla_w_ukv, mla_qn_g, mla_qr_g, mla_kn_g, mla_kr_g, ffn_up, ffn_dw, ffn_db, ffn_down):
    bsz, n_lat, d = x.shape
    n_ctx = ctx.shape[1]
    depth = w_in.shape[0]
    hg_w = hg_lb.shape[-1]
    rw_w = rw_w0.shape[-1]
    q_rank = mla_q_norm.shape[-1]
    kv_rank = mla_kv_norm.shape[-1]
    n_heads = mla_w_ukv.shape[-1] // (MLA_NOPE + MLA_V)
    d_ff = ffn_down.shape[1]
    lora = (rw_w2.shape[2], rw_w2.shape[2], rw_a2.shape[2], rw_a2.shape[2], rw_g2.shape[1])
    tb = n_ctx
    nb = 2 if bsz % 2 == 0 else 1
    nb_mm = 4 if bsz % 4 == 0 else nb
    assert n_lat % tb == 0 and tb % HG_CHUNK == 0 and max(lora) <= LANE

    cc = jnp.concatenate([c, c_ctx[None, :]], axis=0)
    cc = jnp.pad(cc, ((0, -(bsz + 1) % SUBLANE), (0, 0)))
    mods = _mods(cc, w_mod, b_mod)
    mod_lat = mods[:, :bsz].reshape(depth, bsz, 6, d).transpose(0, 2, 1, 3)
    mod_ctx = jnp.broadcast_to(mods[:, bsz].reshape(depth, 6, 1, d), (depth, 6, bsz, d))
    modtab = jnp.stack([mod_ctx, mod_lat], axis=2)[:, :, :, :, None, :]

    lb_p = jax.nn.softmax(hg_lb.astype(F32), axis=0)
    lower = jnp.cumsum(lb_p, axis=0) - lb_p[0]

    hg_cols = 5 * hg_w
    rw_cols = 3 * rw_w + sum(lora)
    w_hg = w_in[:, :, :hg_cols].astype(BF16)
    w_rw_raw = w_in[:, :, hg_cols:hg_cols + rw_cols]
    w_ml = w_in[:, :, hg_cols + rw_cols:]
    offs = [3 * rw_w]
    for n in lora:
        offs.append(offs[-1] + n)

    def pad_lora(a):
        parts = [a[..., :3 * rw_w]] + [_pad_cols(a[..., offs[i]:offs[i + 1]], LANE) for i in range(5)]
        return jnp.concatenate(parts, axis=-1)

    w_rw = pad_lora(w_rw_raw).astype(BF16)
    mu = pad_lora(rw_mu)[:, None, :]
    w_ml = _pad_cols(w_ml, q_rank + kv_rank + LANE).astype(BF16)

    def pad_rows(a):
        return jnp.pad(a, ((0, 0),) * (a.ndim - 2) + ((0, LANE - a.shape[-2]), (0, 0)))

    w2 = pad_rows(rw_w2).astype(BF16)
    a2 = pad_rows(rw_a2).astype(BF16)
    g2 = pad_rows(rw_g2).astype(BF16)

    wq = mla_w_uq.reshape(depth, q_rank, n_heads, MLA_NOPE + MLA_ROPE)
    wq = _pad_cols(wq, MLA_NOPE + LANE).reshape(depth, q_rank, n_heads * (MLA_NOPE + LANE)).astype(BF16)
    wkv = mla_w_ukv.reshape(depth, kv_rank, n_heads, MLA_NOPE + MLA_V)
    wkv = jnp.concatenate([wkv[..., :MLA_NOPE].reshape(depth, kv_rank, -1),
                           wkv[..., MLA_NOPE:].reshape(depth, kv_rank, -1)], axis=-1).astype(BF16)
    cos, sin = _rope_tables(n_ctx, n_lat)

    w_out_b = w_out.astype(BF16)
    tk = _pick(d_ff, (512, 256, 128))
    up_b = ffn_up.astype(BF16)
    down_b = ffn_down.astype(BF16)

    xs = jnp.concatenate([ctx, x], axis=1)
    for l in range(depth):
        gain0, gain1 = norm_g[l, 0:1], norm_g[l, 1:2]
        mt = modtab[l]
        h = _norm(xs, gain0, mt, tb, nb)
        p_hg = _in_proj(h, w_hg[l], tb, nb, hg_cols)
        p_rw = _in_proj(h, w_rw[l], tb, nb, w_rw.shape[-1])
        p_ml = _in_proj(h, w_ml[l], tb, nb_mm, w_ml.shape[-1])

        hg_f, hg_b = _hg_scan(p_hg, lower[l], tb)
        hg_o = _hg_finish(hg_f, hg_b, p_hg, hg_gn[l][None, :], tb)

        r, v, kk, w_f, k_f, b_f, w_b, k_b, b_b, g = _rw_prep(
            p_rw, mu[l], rw_w0[l], w2[l], rw_a0[l], a2[l], g2[l], rw_kk[l][None, :], rw_ka[l][None, :], tb)
        rw_f, rw_b = _rw_scan(r, v, kk, w_f, k_f, b_f, w_b, k_b, b_b, tb)
        rw_o = _rw_finish(rw_f, rw_b, r, k_f, k_b, v, g, rw_gn_w[l][None, :], rw_gn_b[l][None, :],
                          rw_rk[l].reshape(1, rw_w), tb)

        q, k, vv = _mla_prep(p_ml, cos, sin, mla_q_norm[l][None, :], mla_kv_norm[l][None, :], wq[l], wkv[l],
                             mla_qn_g[l][None, :], _pad_cols(mla_qr_g[l][None, :], LANE),
                             mla_kn_g[l][None, :], _pad_cols(mla_kr_g[l][None, :], LANE), tb)
        ml_o = _mla_attn(q, k, vv, tb)

        last = l == depth - 1
        xs = _out_proj(xs, hg_o, rw_o, ml_o, w_out_b[l], mt, tb, nb, d, 1 if last else 0)
        xs = _ffn(xs, gain1, mt, up_b[l], ffn_dw[l], ffn_db[l][None, :], down_b[l], tb, nb, tk, 0 if last else 1)
    return xs
```

```python
import functools
import math

import jax
import jax.numpy as jnp
from jax import lax
from jax.experimental import pallas as pl
from jax.experimental.pallas import tpu as pltpu

F32 = jnp.float32
BF16 = jnp.bfloat16

NORM_EPS = 1e-6
GRID_W = 64
ROPE_THETA = 10000.0
HG_HD = 128
HG_CHUNK = 16
RW_HD = 64
RW_GN_EPS = 64e-5
RW_DECAY_MAX = math.exp(-0.5)
MLA_V = 128
MLA_NOPE = 128
MLA_ROPE = 64
MLA_SCALE = (MLA_NOPE + MLA_ROPE) ** -0.5
LOG2E = math.log2(math.e)
LANE = 128
SUBLANE = 8
RW_CHUNK = 64
NORM_ROW_TILE = 16
ATTN_HEADS = 4
FFN_ROW_TILE = 32
VMEM_LIMIT = 56 * 1024 * 1024


def _cparams(*sem):
    return pltpu.CompilerParams(dimension_semantics=sem, vmem_limit_bytes=VMEM_LIMIT)


def _bdot(a, b):
    return jnp.dot(a.astype(BF16), b.astype(BF16), preferred_element_type=F32)


def _sigmoid(x):
    return 1.0 / (1.0 + jnp.exp(-x))


def _silu(x):
    return x * _sigmoid(x)


def _pad_cols(a, width):
    return jnp.pad(a, [(0, 0)] * (a.ndim - 1) + [(0, width - a.shape[-1])])


def _mods_kernel(c_ref, w_ref, b_ref, o_ref):
    o_ref[...] = _bdot(_silu(c_ref[...]), w_ref[...]) + b_ref[...]


def _mods(cc, w_mod, b_mod):
    n_layers, d, n = w_mod.shape
    rows = cc.shape[0]
    tn = 768 if n % 768 == 0 else n
    return pl.pallas_call(
        _mods_kernel,
        out_shape=jax.ShapeDtypeStruct((n_layers, rows, n), F32),
        grid=(n_layers, n // tn),
        in_specs=[
            pl.BlockSpec((rows, d), lambda l, j: (0, 0)),
            pl.BlockSpec((None, d, tn), lambda l, j: (l, 0, j)),
            pl.BlockSpec((None, 1, tn), lambda l, j: (l, 0, j)),
        ],
        out_specs=pl.BlockSpec((None, rows, tn), lambda l, j: (l, 0, j)),
        compiler_params=_cparams("parallel", "parallel"),
    )(cc, w_mod, b_mod.reshape(n_layers, 1, n))


def _norm_mod(x, gain, shift, scale):
    y = x * lax.rsqrt(jnp.mean(x * x, axis=-1, keepdims=True) + NORM_EPS) * gain
    return y * (1.0 + scale) + shift


def _norm_mod_rows(x_ref, h_ref, g_ref, sh_ref, sc_ref):
    nb, tb, _ = x_ref.shape
    rt = NORM_ROW_TILE
    for n in range(nb):
        mult = g_ref[...] * (1.0 + sc_ref[n])
        shift = sh_ref[n]

        def tile(i, carry, n=n, mult=mult, shift=shift):
            r = pl.multiple_of(i * rt, rt)
            x = x_ref[n, pl.ds(r, rt), :]
            rs = lax.rsqrt(jnp.mean(x * x, axis=-1, keepdims=True) + NORM_EPS)
            h = (x * rs * mult + shift).astype(h_ref.dtype)
            if len(h_ref.shape) == 3:
                h_ref[n, pl.ds(r, rt), :] = h
            else:
                h_ref[pl.ds(pl.multiple_of(n * tb + r, rt), rt), :] = h
            return carry

        lax.fori_loop(0, tb // rt, tile, 0, unroll=4)


def _mod_spec(m, nb, d):
    return pl.BlockSpec((None, None, nb, 1, d), lambda bi, j: (m, jnp.minimum(j, 1), bi, 0, 0))


def _norm_kernel(x_ref, g_ref, sh_ref, sc_ref, h_ref):
    _norm_mod_rows(x_ref, h_ref, g_ref, sh_ref, sc_ref)


def _norm(x, gain, modtab, tb, nb):
    b, t, d = x.shape
    return pl.pallas_call(
        _norm_kernel,
        out_shape=jax.ShapeDtypeStruct((b, t, d), BF16),
        grid=(b // nb, t // tb),
        in_specs=[
            pl.BlockSpec((nb, tb, d), lambda bi, j: (bi, j, 0)),
            pl.BlockSpec((1, d), lambda bi, j: (0, 0)),
            _mod_spec(0, nb, d),
            _mod_spec(1, nb, d),
        ],
        out_specs=pl.BlockSpec((nb, tb, d), lambda bi, j: (bi, j, 0)),
        compiler_params=_cparams("parallel", "parallel"),
    )(x, gain, modtab, modtab)


def _in_proj_kernel(h_ref, w_ref, o_ref):
    nb, tb, d = h_ref.shape
    o_ref[...] = jnp.dot(h_ref[...].reshape(nb * tb, d), w_ref[...],
                         preferred_element_type=F32).reshape(o_ref.shape)


def _in_proj(h, w, tb, nb, tn):
    b, t, d = h.shape
    n = w.shape[1]
    return pl.pallas_call(
        _in_proj_kernel,
        out_shape=jax.ShapeDtypeStruct((b, t, n), F32),
        grid=(b // nb, t // tb, n // tn),
        in_specs=[
            pl.BlockSpec((nb, tb, d), lambda bi, j, k: (bi, j, 0)),
            pl.BlockSpec((d, tn), lambda bi, j, k: (0, k)),
        ],
        out_specs=pl.BlockSpec((nb, tb, tn), lambda bi, j, k: (bi, j, k)),
        compiler_params=_cparams("parallel", "parallel", "arbitrary"),
    )(h, w)


def _out_proj_kernel(x_ref, hg_ref, rw_ref, ml_ref, w_ref, gate_ref, o_ref):
    nb, tb, tn = o_ref.shape
    mixed = jnp.concatenate([hg_ref[...], rw_ref[...], ml_ref[...]], axis=-1)
    acc = jnp.dot(mixed.reshape(nb * tb, mixed.shape[-1]), w_ref[...], preferred_element_type=F32)
    o_ref[...] = x_ref[...] + gate_ref[...] * acc.reshape(nb, tb, tn)


def _out_proj(x, hg, rw, ml, w, modtab, tb, nb, tn, j0):
    b, t, d = x.shape
    return pl.pallas_call(
        _out_proj_kernel,
        out_shape=jax.ShapeDtypeStruct((b, t - j0 * tb, d), F32),
        grid=(b // nb, t // tb - j0, d // tn),
        in_specs=[
            pl.BlockSpec((nb, tb, tn), lambda bi, j, k: (bi, j + j0, k)),
            pl.BlockSpec((nb, tb, hg.shape[-1]), lambda bi, j, k: (bi, j + j0, 0)),
            pl.BlockSpec((nb, tb, rw.shape[-1]), lambda bi, j, k: (bi, j + j0, 0)),
            pl.BlockSpec((nb, tb, ml.shape[-1]), lambda bi, j, k: (bi, j + j0, 0)),
            pl.BlockSpec((d, tn), lambda bi, j, k: (0, k)),
            pl.BlockSpec((None, None, nb, 1, tn), lambda bi, j, k: (2, jnp.minimum(j + j0, 1), bi, 0, k)),
        ],
        out_specs=pl.BlockSpec((nb, tb, tn), lambda bi, j, k: (bi, j, k)),
        compiler_params=_cparams("parallel", "parallel", "arbitrary"),
    )(x, hg, rw, ml, w, modtab)


def _shift_rows(u, tb, first_rows, last_rows):
    rows = u.shape[0]
    row = lax.broadcasted_iota(jnp.int32, u.shape, 0)
    prev = pltpu.roll(u, 1, 0)
    nxt = pltpu.roll(u, rows - 1, 0)
    for n, (fr, lr) in enumerate(zip(first_rows, last_rows)):
        prev = jnp.where(row == n * tb, fr, prev)
        nxt = jnp.where(row == n * tb + tb - 1, lr, nxt)
    return prev, nxt


def _ffn_kernel(ctx_blocks, x_ref, xp_ref, xn_ref, g_ref, sh_ref, sc_ref, gate_ref, wa_ref, wb_ref, dwa_ref,
                dwb_ref, dba_ref, dbb_ref, wd_ref, o_ref, h_ref, ua_ref, ub_ref, act_ref):
    nb, tb, d = x_ref.shape
    rows = nb * tb
    halo = 2 * SUBLANE
    rt = FFN_ROW_TILE
    j = pl.program_id(1)
    k = pl.program_id(2)
    n_k = pl.num_programs(2)
    n_j = pl.num_programs(1)
    trow = lax.broadcasted_iota(jnp.int32, (rt, act_ref.shape[1]), 0)

    def conv(u_ref, dw, db, n, r):
        base = n * tb + r
        mid = u_ref[base:base + rt, :]
        if r == 0:
            edge = u_ref[rows + n * halo + SUBLANE - 1:rows + n * halo + SUBLANE, :]
            prev = jnp.where(trow == 0, edge, pltpu.roll(mid, 1, 0))
        else:
            prev = u_ref[base - 1:base - 1 + rt, :]
        if r == tb - rt:
            edge = u_ref[rows + n * halo + SUBLANE:rows + n * halo + SUBLANE + 1, :]
            nxt = jnp.where(trow == rt - 1, edge, pltpu.roll(mid, rt - 1, 0))
        else:
            nxt = u_ref[base + 1:base + 1 + rt, :]
        return prev * dw[0:1] + mid * dw[1:2] + nxt * dw[2:3] + db

    @pl.when(k == 0)
    def _():
        p_ok = (j >= ctx_blocks + 1).astype(F32)
        n_ok = jnp.logical_and(j >= ctx_blocks, j < n_j - 1).astype(F32)
        g, sh, sc = g_ref[...], sh_ref[...], sc_ref[...]
        _norm_mod_rows(x_ref, h_ref, g_ref, sh_ref, sc_ref)
        hp = (_norm_mod(xp_ref[...], g, sh, sc) * p_ok).astype(BF16)
        hn = (_norm_mod(xn_ref[...], g, sh, sc) * n_ok).astype(BF16)
        for n in range(nb):
            h_ref[rows + n * halo:rows + n * halo + SUBLANE, :] = hp[n]
            h_ref[rows + n * halo + SUBLANE:rows + (n + 1) * halo, :] = hn[n]
        o_ref[...] = jnp.zeros_like(o_ref)

    ua_ref[...] = jnp.dot(h_ref[...], wa_ref[...], preferred_element_type=F32)
    ub_ref[...] = jnp.dot(h_ref[...], wb_ref[...], preferred_element_type=F32)
    dwa, dwb, dba, dbb = dwa_ref[...], dwb_ref[...], dba_ref[...], dbb_ref[...]
    for n in range(nb):
        for r in range(0, tb, rt):
            a = conv(ua_ref, dwa, dba, n, r)
            b = conv(ub_ref, dwb, dbb, n, r)
            act_ref[n * tb + r:n * tb + r + rt, :] = (_silu(a) * b).astype(BF16)
    o_ref[...] += jnp.dot(act_ref[...], wd_ref[...], preferred_element_type=F32).reshape(nb, tb, d)

    @pl.when(k == n_k - 1)
    def _():
        o_ref[...] = x_ref[...] + gate_ref[...] * o_ref[...]


def _ffn(x, gain, modtab, w_up, dw, db, w_down, tb, nb, tk, ctx_blocks):
    b, t, d = x.shape
    n_k = w_down.shape[0] // tk
    r8 = tb // SUBLANE
    n8 = t // SUBLANE
    rows_ext = nb * tb + nb * 2 * SUBLANE

    def mspec(m):
        return pl.BlockSpec((None, None, nb, 1, d),
                            lambda bi, j, k: (m, jnp.minimum(j, 1) if ctx_blocks else 1, bi, 0, 0))

    return pl.pallas_call(
        functools.partial(_ffn_kernel, ctx_blocks),
        out_shape=jax.ShapeDtypeStruct((b, t, d), F32),
        grid=(b // nb, t // tb, n_k),
        in_specs=[
            pl.BlockSpec((nb, tb, d), lambda bi, j, k: (bi, j, 0)),
            pl.BlockSpec((nb, SUBLANE, d), lambda bi, j, k: (bi, jnp.maximum(j * r8 - 1, 0), 0)),
            pl.BlockSpec((nb, SUBLANE, d), lambda bi, j, k: (bi, jnp.minimum((j + 1) * r8, n8 - 1), 0)),
            pl.BlockSpec((1, d), lambda bi, j, k: (0, 0)),
            mspec(3), mspec(4), mspec(5),
            pl.BlockSpec((d, tk), lambda bi, j, k: (0, k)),
            pl.BlockSpec((d, tk), lambda bi, j, k: (0, n_k + k)),
            pl.BlockSpec((3, tk), lambda bi, j, k: (0, k)),
            pl.BlockSpec((3, tk), lambda bi, j, k: (0, n_k + k)),
            pl.BlockSpec((1, tk), lambda bi, j, k: (0, k)),
            pl.BlockSpec((1, tk), lambda bi, j, k: (0, n_k + k)),
            pl.BlockSpec((tk, d), lambda bi, j, k: (k, 0)),
        ],
        out_specs=pl.BlockSpec((nb, tb, d), lambda bi, j, k: (bi, j, 0)),
        scratch_shapes=[pltpu.VMEM((rows_ext, d), BF16), pltpu.VMEM((rows_ext, tk), F32),
                        pltpu.VMEM((rows_ext, tk), F32), pltpu.VMEM((nb * tb, tk), BF16)],
        compiler_params=_cparams("parallel", "parallel", "arbitrary"),
    )(x, x, x, gain, modtab, modtab, modtab, w_up, w_up, dw, dw, db, db, w_down)


def _chunk_cumsum(x, c, reverse):
    rows = x.shape[0]
    ri = lax.broadcasted_iota(jnp.int32, (rows, rows), 0)
    ci = lax.broadcasted_iota(jnp.int32, (rows, rows), 1)
    tri = jnp.logical_and((ri // c) == (ci // c), (ci >= ri) if reverse else (ci <= ri)).astype(BF16)
    hi = x.astype(BF16)
    rest = x - hi.astype(F32)
    mid = rest.astype(BF16)
    lo = (rest - mid.astype(F32)).astype(BF16)
    return (jnp.dot(tri, hi, preferred_element_type=F32) + jnp.dot(tri, mid, preferred_element_type=F32)
            + jnp.dot(tri, lo, preferred_element_type=F32))


def _bwd_block(j, n_j):
    return jnp.where(j == 0, 0, n_j - j)


def _hg_scan_kernel(qf_ref, zf_ref, if_ref, qb_ref, zb_ref, ib_ref, lb_ref, of_ref, ob_ref,
                    st_ref, g_ref, k_ref):
    tb, w = zf_ref.shape
    n_heads = w // HG_HD
    n_chunks = tb // HG_CHUNK
    c = HG_CHUNK

    @pl.when(pl.program_id(1) == 0)
    def _():
        st_ref[...] = jnp.zeros_like(st_ref)

    ones = jnp.ones((HG_HD, HG_HD), BF16)
    trow = lax.broadcasted_iota(jnp.int32, (c, HG_HD), 0)

    dirs = ((qf_ref, zf_ref, if_ref, of_ref), (qb_ref, zb_ref, ib_ref, ob_ref))
    for dr, (q_ref, z_ref, i_ref, o_ref) in enumerate(dirs):
        lb = lb_ref[dr:dr + 1, :]
        f = lb + (1.0 - lb) * _sigmoid(z_ref[...])
        g_ref[dr] = _chunk_cumsum(jnp.log2(f), c, dr == 1)
        k_ref[dr] = 1.0 - f

    def dir_chunk(dr, r0):
        q_ref, _z_ref, i_ref, o_ref = dirs[dr]
        heads = []
        tiles = []
        for h in range(n_heads):
            ls = slice(h * HG_HD, (h + 1) * HG_HD)
            q = q_ref[pl.ds(r0, c), ls]
            v = i_ref[pl.ds(r0, c), ls]
            g = g_ref[dr, pl.ds(r0, c), ls]
            kk = k_ref[dr, pl.ds(r0, c), ls]
            heads.append((ls, q, v, g, kk))
            for s in range(c):
                seen = (trow >= s) if dr == 0 else (trow <= s)
                tiles.append(jnp.where(seen, q * kk[s:s + 1] * jnp.exp2(g - g[s:s + 1]), 0.0).astype(BF16))
        attn = jnp.dot(jnp.concatenate(tiles, axis=0), ones, preferred_element_type=F32)
        inter = []
        for h, (ls, q, v, g, kk) in enumerate(heads):
            g_last = g[c - 1:c] if dr == 0 else g[0:1]
            st = st_ref[dr, h]
            inter.append(lax.dot_general((q * jnp.exp2(g)).astype(BF16), st.astype(BF16),
                                         (((1,), (1,)), ((), ())), preferred_element_type=F32))
            kd = kk * jnp.exp2(g_last - g)
            st_ref[dr, h] = st * jnp.exp2(g_last) + lax.dot_general(
                v.astype(BF16), kd.astype(BF16), (((0,), (0,)), ((), ())), preferred_element_type=F32)
        yield
        for h, (ls, q, v, g, kk) in enumerate(heads):
            o = inter[h]
            for s in range(c):
                o += attn[(h * c + s) * c:(h * c + s + 1) * c] * v[s:s + 1]
            o_ref[pl.ds(r0, c), ls] = o

    def chunk(ic, carry):
        live = [dir_chunk(dr, pl.multiple_of((ic if dr == 0 else n_chunks - 1 - ic) * c, c)) for dr in range(2)]
        while live:
            live = [gen for gen in live if next(gen, True) is None]
        return carry

    lax.fori_loop(0, n_chunks, chunk, 0)


def _hg_scan(p_hg, lb, tb):
    b, t, _ = p_hg.shape
    w = lb.shape[-1]
    n_j = t // tb

    def fwd(col):
        return pl.BlockSpec((None, tb, w), lambda bi, j: (bi, j, col))

    def bwd(col):
        return pl.BlockSpec((None, tb, w), lambda bi, j: (bi, _bwd_block(j, n_j), col))

    out = jax.ShapeDtypeStruct((b, t, w), F32)
    return pl.pallas_call(
        _hg_scan_kernel,
        out_shape=(out, out),
        grid=(b, n_j),
        in_specs=[fwd(0), fwd(1), fwd(3), bwd(0), bwd(2), bwd(3), pl.BlockSpec((2, w), lambda bi, j: (0, 0))],
        out_specs=(fwd(0), bwd(0)),
        scratch_shapes=[pltpu.VMEM((2, w // HG_HD, HG_HD, HG_HD), F32), pltpu.VMEM((2, tb, w), F32),
                        pltpu.VMEM((2, tb, w), F32)],
        compiler_params=_cparams("parallel", "arbitrary"),
    )(p_hg, p_hg, p_hg, p_hg, p_hg, p_hg, lb)


def _hg_finish_kernel(of_ref, ob_ref, gate_ref, gn_ref, o_ref):
    w = of_ref.shape[-1]
    for h in range(w // HG_HD):
        ls = slice(h * HG_HD, (h + 1) * HG_HD)
        o = of_ref[:, ls] + ob_ref[:, ls]
        o = o * lax.rsqrt(jnp.mean(o * o, axis=-1, keepdims=True) + NORM_EPS) * gn_ref[...]
        o_ref[:, ls] = (o * _silu(gate_ref[:, ls])).astype(o_ref.dtype)


def _hg_finish(o_f, o_b, p_hg, gn, tb):
    b, t, w = o_f.shape
    spec = pl.BlockSpec((None, tb, w), lambda bi, j: (bi, j, 0))
    return pl.pallas_call(
        _hg_finish_kernel,
        out_shape=jax.ShapeDtypeStruct((b, t, w), BF16),
        grid=(b, t // tb),
        in_specs=[spec, spec, pl.BlockSpec((None, tb, w), lambda bi, j: (bi, j, 4)),
                  pl.BlockSpec((1, HG_HD), lambda bi, j: (0, 0))],
        out_specs=spec,
        compiler_params=_cparams("parallel", "parallel"),
    )(o_f, o_b, p_hg, gn)


def _head_ones(width, hd):
    r = lax.broadcasted_iota(jnp.int32, (width, width), 0) // hd
    c = lax.broadcasted_iota(jnp.int32, (width, width), 1) // hd
    return (r == c).astype(F32)


def _head_sum(x, hd):
    same_head = _head_ones(LANE, hd).astype(BF16)
    out = []
    for i in range(x.shape[-1] // LANE):
        xs = x[:, i * LANE:(i + 1) * LANE]
        hi = xs.astype(BF16)
        lo = (xs - hi.astype(F32)).astype(BF16)
        out.append(jnp.dot(hi, same_head, preferred_element_type=F32)
                   + jnp.dot(lo, same_head, preferred_element_type=F32))
    return jnp.concatenate(out, axis=-1)


def _rw_prep_kernel(p_ref, pp_ref, pn_ref, mu_ref, w0_ref, w2_ref, a0_ref, a2_ref, g2_ref, kk_ref,
                    ka_ref, r_o, v_o, kk_o, wf_o, kf_o, bf_o, wb_o, kb_o, bb_o, g_o):
    tb = p_ref.shape[0]
    w = r_o.shape[-1]
    lw = w2_ref.shape[1]
    j = pl.program_id(1)
    n_j = pl.num_programs(1)
    p = p_ref[...]
    p_ok = (j >= 2).astype(F32)
    n_ok = jnp.logical_and(j >= 1, j < n_j - 1).astype(F32)
    prev, nxt = _shift_rows(p, tb, [pp_ref[SUBLANE - 1:SUBLANE, :] * p_ok], [pn_ref[0:1, :] * n_ok])
    s = p + mu_ref[...] * (0.5 * (prev + nxt) - p)
    r, k, v = s[:, 0:w], s[:, w:2 * w], s[:, 2 * w:3 * w]
    lora = [s[:, 3 * w + i * lw:3 * w + (i + 1) * lw] for i in range(5)]
    kk = k * kk_ref[...]
    ssq = _head_sum(kk * kk, RW_HD)
    kk = kk / jnp.maximum(jnp.sqrt(ssq), 1e-12)
    r_o[...] = r
    v_o[...] = v
    kk_o[...] = kk
    for dr, (w_o, k_o, b_o) in enumerate(((wf_o, kf_o, bf_o), (wb_o, kb_o, bb_o))):
        xw, xa = lora[dr], lora[2 + dr]
        w_o[...] = -RW_DECAY_MAX * _sigmoid(w0_ref[dr:dr + 1, :] + _bdot(jnp.tanh(xw), w2_ref[dr]))
        a = _sigmoid(a0_ref[dr:dr + 1, :] + _bdot(xa, a2_ref[dr]))
        k_o[...] = k * (1.0 + (a - 1.0) * ka_ref[...])
        b_o[...] = kk * a
    g_o[...] = _bdot(_sigmoid(lora[4]), g2_ref[...])


def _rw_prep(p_rw, mu, w0, w2, a0, a2, g2, k_k, k_a, tb):
    b, t, n = p_rw.shape
    w = w0.shape[-1]
    r8, n8 = tb // SUBLANE, t // SUBLANE
    out = jax.ShapeDtypeStruct((b, t, w), F32)
    ospec = pl.BlockSpec((None, tb, w), lambda bi, j: (bi, j, 0))

    def const(a):
        return pl.BlockSpec(a.shape, lambda bi, j: (0,) * a.ndim)

    return pl.pallas_call(
        _rw_prep_kernel,
        out_shape=(out,) * 10,
        grid=(b, t // tb),
        in_specs=[
            pl.BlockSpec((None, tb, n), lambda bi, j: (bi, j, 0)),
            pl.BlockSpec((None, SUBLANE, n), lambda bi, j: (bi, jnp.maximum(j * r8 - 1, 0), 0)),
            pl.BlockSpec((None, SUBLANE, n), lambda bi, j: (bi, jnp.minimum((j + 1) * r8, n8 - 1), 0)),
            const(mu), const(w0), const(w2), const(a0), const(a2), const(g2), const(k_k), const(k_a),
        ],
        out_specs=(ospec,) * 10,
        compiler_params=_cparams("parallel", "parallel"),
    )(p_rw, p_rw, p_rw, mu, w0, w2, a0, a2, g2, k_k, k_a)


def _rw_scan_kernel(rf_ref, vf_ref, kkf_ref, wf_ref, kf_ref, bf_ref, rb_ref, vb_ref, kkb_ref, wb_ref,
                    kb_ref, bb_ref, of_ref, ob_ref, st_ref, g_ref):
    tb, w = rf_ref.shape
    n_pairs = w // LANE
    hd = RW_HD
    c = RW_CHUNK
    n_chunks = tb // c
    assert c == hd and tb % c == 0

    @pl.when(pl.program_id(1) == 0)
    def _():
        st_ref[...] = jnp.zeros_like(st_ref)

    dirs = ((rf_ref, vf_ref, kkf_ref, wf_ref, kf_ref, bf_ref, of_ref),
            (rb_ref, vb_ref, kkb_ref, wb_ref, kb_ref, bb_ref, ob_ref))
    for dr, refs in enumerate(dirs):
        g_ref[dr] = _chunk_cumsum(refs[3][...], c, dr == 1)

    row = lax.broadcasted_iota(jnp.int32, (c, LANE), 0)
    col = lax.broadcasted_iota(jnp.int32, (c, LANE), 1) % hd
    head_a = lax.broadcasted_iota(jnp.int32, (c, LANE), 1) < hd
    head_a2 = (lax.broadcasted_iota(jnp.int32, (c, 2 * LANE), 1) % LANE) < hd
    r2 = lax.broadcasted_iota(jnp.int32, (LANE, LANE), 0)
    c2 = lax.broadcasted_iota(jnp.int32, (LANE, LANE), 1)
    same_head = (r2 // hd) == (c2 // hd)
    eye = r2 == c2

    def stack2(y):
        m = head_a if y.shape[1] == LANE else head_a2
        return jnp.concatenate([jnp.where(m, y, 0.0), jnp.where(m, 0.0, y)], axis=0).astype(BF16)

    def tn(x, y):
        return lax.dot_general(x.astype(BF16), y.astype(BF16), (((0,), (0,)), ((), ())),
                               preferred_element_type=F32)

    def pair_chunk(dr, hp, r0):
        refs = dirs[dr]
        before = (col < row) if dr == 0 else (col > row)
        upto = (col <= row) if dr == 0 else (col >= row)
        ls = slice(hp * LANE, (hp + 1) * LANE)
        rr, vv, kk, lw, kd, bb = [ref[pl.ds(r0, c), ls] for ref in refs[:6]]
        g = g_ref[dr, pl.ds(r0, c), ls]
        g_end = g[c - 1:c] if dr == 0 else g[0:1]
        at = -kk * jnp.exp(g - lw)
        rt = rr * jnp.exp(g)
        e_inv = jnp.exp(-g)
        e_out = jnp.exp(g_end - g)
        bh_kh = jnp.concatenate([bb * e_out, kd * e_out], axis=0)
        abk = lax.dot_general(jnp.concatenate([at, rt], axis=0).astype(BF16),
                              jnp.concatenate([stack2(bb * e_inv), stack2(kd * e_inv)], axis=0),
                              (((1,), (1,)), ((), ())), preferred_element_type=F32)
        yield
        x = jnp.where(before, abk[0:c, 0:LANE], 0.0)
        a_ak = jnp.where(before, abk[0:c, LANE:], 0.0)
        a_r = jnp.concatenate([jnp.where(upto, abk[c:, 0:LANE], 0.0), jnp.where(upto, abk[c:, LANE:], 0.0)], axis=1)
        u0 = jnp.dot(a_ak.astype(BF16), stack2(vv), preferred_element_type=F32)
        yield
        y = jnp.concatenate([at, u0], axis=1)
        span = 1
        while span < c:
            span *= 2
            rhs = stack2(y) if span >= c else jnp.concatenate([stack2(y), stack2(x)], axis=1)
            z = jnp.dot(x.astype(BF16), rhs, preferred_element_type=F32)
            yield
            y = y + z[:, 0:2 * LANE]
            if span < c:
                x = z[:, 2 * LANE:]
        zero = jnp.zeros((c, LANE), F32)
        v_pad = jnp.concatenate([zero, vv], axis=1)
        wu = jnp.dot(a_r.astype(BF16), jnp.concatenate([stack2(y), stack2(v_pad)], axis=0),
                     preferred_element_type=F32)
        mn = tn(bh_kh, jnp.concatenate([y, v_pad], axis=0))
        yield
        q_eff = rt + wu[:, 0:LANE]
        m_bd = jnp.where(same_head, mn[:, 0:LANE], 0.0) + jnp.where(eye, jnp.exp(g_end), 0.0)
        n_bd = jnp.where(same_head, mn[:, LANE:], 0.0)
        st = st_ref[dr * n_pairs + hp].astype(BF16)
        so = jnp.dot(jnp.concatenate([q_eff, m_bd], axis=0).astype(BF16), st, preferred_element_type=F32)
        yield
        refs[6][pl.ds(r0, c), ls] = so[0:c] + wu[:, LANE:]
        st_ref[dr * n_pairs + hp] = so[c:] + n_bd

    def chunk(ic, carry):
        live = [pair_chunk(dr, hp, pl.multiple_of((ic if dr == 0 else n_chunks - 1 - ic) * c, c))
                for dr in range(2) for hp in range(n_pairs)]
        while live:
            live = [gen for gen in live if next(gen, True) is None]
        return carry

    lax.fori_loop(0, n_chunks, chunk, 0)


def _rw_scan(r, v, kk, w_f, k_f, b_f, w_b, k_b, b_b, tb):
    b, t, w = r.shape
    n_j = t // tb
    fwd = pl.BlockSpec((None, tb, w), lambda bi, j: (bi, j, 0))
    bwd = pl.BlockSpec((None, tb, w), lambda bi, j: (bi, _bwd_block(j, n_j), 0))
    out = jax.ShapeDtypeStruct((b, t, w), F32)
    return pl.pallas_call(
        _rw_scan_kernel,
        out_shape=(out, out),
        grid=(b, n_j),
        in_specs=[fwd] * 6 + [bwd] * 6,
        out_specs=(fwd, bwd),
        scratch_shapes=[pltpu.VMEM((2 * (w // LANE), LANE, LANE), F32), pltpu.VMEM((2, tb, w), F32)],
        compiler_params=_cparams("parallel", "arbitrary"),
    )(r, v, kk, w_f, k_f, b_f, r, v, kk, w_b, k_b, b_b)


def _rw_finish_kernel(of_ref, ob_ref, r_ref, kf_ref, kb_ref, v_ref, g_ref, gnw_ref, gnb_ref, rk_ref, o_ref):
    o = of_ref[...] + ob_ref[...]
    cen = o - _head_sum(o, RW_HD) * (1.0 / RW_HD)
    var = _head_sum(cen * cen, RW_HD) * (1.0 / RW_HD)
    o = cen * lax.rsqrt(var + RW_GN_EPS) * gnw_ref[...] + gnb_ref[...]
    dot_rk = _head_sum(r_ref[...] * (kf_ref[...] + kb_ref[...]) * rk_ref[...], RW_HD)
    o_ref[...] = ((o + dot_rk * v_ref[...]) * g_ref[...]).astype(o_ref.dtype)


def _rw_finish(o_f, o_b, r, k_f, k_b, v, g, gn_w, gn_b, r_k, tb):
    b, t, w = o_f.shape
    spec = pl.BlockSpec((None, tb, w), lambda bi, j: (bi, j, 0))
    cspec = pl.BlockSpec((1, w), lambda bi, j: (0, 0))
    return pl.pallas_call(
        _rw_finish_kernel,
        out_shape=jax.ShapeDtypeStruct((b, t, w), BF16),
        grid=(b, t // tb),
        in_specs=[spec] * 7 + [cspec] * 3,
        out_specs=spec,
        compiler_params=_cparams("parallel", "parallel"),
    )(o_f, o_b, r, k_f, k_b, v, g, gn_w, gn_b, r_k)


def _rms(x, gain, n):
    return x * lax.rsqrt(jnp.sum(x * x, axis=-1, keepdims=True) * (1.0 / n) + NORM_EPS) * gain


def _rope128(x, cos, sin):
    half = MLA_ROPE // 2
    lane = lax.broadcasted_iota(jnp.int32, x.shape, 1)
    swapped = jnp.where(lane < half, pltpu.roll(x, LANE - half, 1), pltpu.roll(x, half, 1))
    return x * cos + swapped * sin


def _mla_prep_kernel(p_ref, cos_ref, sin_ref, qn_ref, kvn_ref, wq_ref, wkv_ref, qng_ref, qrg_ref,
                     kng_ref, krg_ref, q_o, k_o, v_o):
    q_rank = qn_ref.shape[-1]
    kv_rank = kvn_ref.shape[-1]
    n_heads = v_o.shape[-1] // MLA_V
    slot = MLA_NOPE + LANE
    cos, sin = cos_ref[...], sin_ref[...]
    p = p_ref[...]
    q = _bdot(_rms(p[:, 0:q_rank], qn_ref[...], q_rank), wq_ref[...])
    kv = _bdot(_rms(p[:, q_rank:q_rank + kv_rank], kvn_ref[...], kv_rank), wkv_ref[...])
    k_rope = _rope128(_rms(p[:, q_rank + kv_rank:], krg_ref[...], MLA_ROPE), cos, sin).astype(k_o.dtype)
    for h in range(n_heads):
        q_nope = _rms(q[:, h * slot:h * slot + MLA_NOPE], qng_ref[...], MLA_NOPE)
        q_rope = _rope128(_rms(q[:, h * slot + MLA_NOPE:(h + 1) * slot], qrg_ref[...], MLA_ROPE), cos, sin)
        q_o[:, h * slot:h * slot + MLA_NOPE] = (q_nope * (MLA_SCALE * LOG2E)).astype(q_o.dtype)
        q_o[:, h * slot + MLA_NOPE:(h + 1) * slot] = (q_rope * (MLA_SCALE * LOG2E)).astype(q_o.dtype)
        k_nope = _rms(kv[:, h * MLA_NOPE:(h + 1) * MLA_NOPE], kng_ref[...], MLA_NOPE)
        k_o[:, h * slot:h * slot + MLA_NOPE] = k_nope.astype(k_o.dtype)
        k_o[:, h * slot + MLA_NOPE:(h + 1) * slot] = k_rope
    v_o[...] = kv[:, n_heads * MLA_NOPE:].astype(v_o.dtype)


def _mla_prep(p_ml, cos, sin, q_norm, kv_norm, w_uq, w_ukv, qn_g, qr_g, kn_g, kr_g, tb):
    b, t, n = p_ml.shape
    n_heads = w_uq.shape[1] // (MLA_NOPE + LANE)

    def const(a):
        return pl.BlockSpec(a.shape, lambda bi, j: (0,) * a.ndim)

    def out(width):
        return (jax.ShapeDtypeStruct((b, t, width), BF16), pl.BlockSpec((None, tb, width), lambda bi, j: (bi, j, 0)))

    outs = [out(n_heads * (MLA_NOPE + LANE)), out(n_heads * (MLA_NOPE + LANE)), out(n_heads * MLA_V)]
    return pl.pallas_call(
        _mla_prep_kernel,
        out_shape=tuple(o[0] for o in outs),
        grid=(b, t // tb),
        in_specs=[
            pl.BlockSpec((None, tb, n), lambda bi, j: (bi, j, 0)),
            pl.BlockSpec((tb, LANE), lambda bi, j: (j, 0)),
            pl.BlockSpec((tb, LANE), lambda bi, j: (j, 0)),
            const(q_norm), const(kv_norm), const(w_uq), const(w_ukv), const(qn_g), const(qr_g),
            const(kn_g), const(kr_g),
        ],
        out_specs=tuple(o[1] for o in outs),
        compiler_params=_cparams("parallel", "parallel"),
    )(p_ml, cos, sin, q_norm, kv_norm, w_uq, w_ukv, qn_g, qr_g, kn_g, kr_g)


def _mla_attn_kernel(q_ref, k_ref, v_ref, o_ref):
    tb = q_ref.shape[0]
    slot = MLA_NOPE + LANE
    n_heads = o_ref.shape[1] // MLA_V

    def head(h, n_keys):
        qs = slice(h * slot, (h + 1) * slot)
        vs = slice(h * MLA_V, (h + 1) * MLA_V)
        s = lax.dot_general(q_ref[:, qs], k_ref[0:n_keys, qs], (((1,), (1,)), ((), ())),
                            preferred_element_type=F32)
        yield
        e = jnp.exp2(s - jnp.max(s, axis=-1, keepdims=True))
        o = jnp.dot(e.astype(BF16), v_ref[0:n_keys, vs], preferred_element_type=F32)
        yield
        o_ref[:, vs] = (o / jnp.sum(e, axis=-1, keepdims=True)).astype(o_ref.dtype)

    def attend(n_keys):
        live = [head(h, n_keys) for h in range(n_heads)]
        while live:
            live = [gen for gen in live if next(gen, True) is None]

    @pl.when(pl.program_id(2) == 0)
    def _():
        attend(tb)

    @pl.when(pl.program_id(2) > 0)
    def _():
        attend(k_ref.shape[0])


def _mla_attn(q, k, v, tb):
    b, t, _ = q.shape
    n_heads = v.shape[-1] // MLA_V
    slot = MLA_NOPE + LANE
    hg = ATTN_HEADS if n_heads % ATTN_HEADS == 0 else 1
    return pl.pallas_call(
        _mla_attn_kernel,
        out_shape=jax.ShapeDtypeStruct((b, t, n_heads * MLA_V), BF16),
        grid=(b, n_heads // hg, t // tb),
        in_specs=[
            pl.BlockSpec((None, tb, hg * slot), lambda bi, h, j: (bi, j, h)),
            pl.BlockSpec((None, t, hg * slot), lambda bi, h, j: (bi, 0, h)),
            pl.BlockSpec((None, t, hg * MLA_V), lambda bi, h, j: (bi, 0, h)),
        ],
        out_specs=pl.BlockSpec((None, tb, hg * MLA_V), lambda bi, h, j: (bi, j, h)),
        compiler_params=_cparams("parallel", "parallel", "arbitrary"),
    )(q, k, v)


def _rope_tables(n_ctx, n_lat):
    rows = n_lat // GRID_W
    row = jnp.repeat(jnp.arange(rows, dtype=F32), GRID_W)
    col = jnp.tile(jnp.arange(GRID_W, dtype=F32), rows)
    axis_dim = MLA_ROPE // 2
    inv_freq = ROPE_THETA ** (-jnp.arange(0, axis_dim, 2, dtype=F32) / axis_dim)
    ang = jnp.concatenate([row[:, None] * inv_freq, col[:, None] * inv_freq], axis=-1)
    ang = jnp.concatenate([jnp.zeros((n_ctx, axis_dim), F32), ang], axis=0)
    cos, sin = jnp.cos(ang), jnp.sin(ang)
    return (_pad_cols(jnp.concatenate([cos, cos], axis=-1), LANE),
            _pad_cols(jnp.concatenate([-sin, sin], axis=-1), LANE))


def _pick(n, prefs):
    for p in prefs:
        if n % p == 0:
            return p
    return n


def kernel(x, c, ctx, c_ctx, norm_g, w_mod, b_mod, w_in, w_out, hg_lb, hg_gn, rw_mu, rw_w0, rw_w2, rw_a0, rw_a2, rw_g2, rw_kk, rw_ka, rw_rk, rw_gn_w, rw_gn_b, mla_q_norm, mla_w_uq, mla_kv_norm, mla_w_ukv, mla_qn_g, mla_qr_g, mla_kn_g, mla_kr_g, ffn_up, ffn_dw, ffn_db, ffn_down):
    bsz, n_lat, d = x.shape
    n_ctx = ctx.shape[1]
    depth = w_in.shape[0]
    hg_w = hg_lb.shape[-1]
    rw_w = rw_w0.shape[-1]
    q_rank = mla_q_norm.shape[-1]
    kv_rank = mla_kv_norm.shape[-1]
    n_heads = mla_w_ukv.shape[-1] // (MLA_NOPE + MLA_V)
    d_ff = ffn_down.shape[1]
    lora = (rw_w2.shape[2], rw_w2.shape[2], rw_a2.shape[2], rw_a2.shape[2], rw_g2.shape[1])
    tb = n_ctx
    nb = 2 if bsz % 2 == 0 else 1
    nb_mm = 4 if bsz % 4 == 0 else nb
    assert n_lat % tb == 0 and tb % HG_CHUNK == 0 and max(lora) <= LANE

    cc = jnp.concatenate([c, c_ctx[None, :]], axis=0)
    cc = jnp.pad(cc, ((0, -(bsz + 1) % SUBLANE), (0, 0)))
    mods = _mods(cc, w_mod, b_mod)
    mod_lat = mods[:, :bsz].reshape(depth, bsz, 6, d).transpose(0, 2, 1, 3)
    mod_ctx = jnp.broadcast_to(mods[:, bsz].reshape(depth, 6, 1, d), (depth, 6, bsz, d))
    modtab = jnp.stack([mod_ctx, mod_lat], axis=2)[:, :, :, :, None, :]

    lb_p = jax.nn.softmax(hg_lb.astype(F32), axis=0)
    lower = jnp.cumsum(lb_p, axis=0) - lb_p[0]

    hg_cols = 5 * hg_w
    rw_cols = 3 * rw_w + sum(lora)
    w_in_b = w_in.astype(BF16)
    w_hg = w_in_b[:, :, :hg_cols]
    w_rw_raw = w_in_b[:, :, hg_cols:hg_cols + rw_cols]
    w_ml = w_in_b[:, :, hg_cols + rw_cols:]
    offs = [3 * rw_w]
    for n in lora:
        offs.append(offs[-1] + n)

    def pad_lora(a):
        parts = [a[..., :3 * rw_w]] + [_pad_cols(a[..., offs[i]:offs[i + 1]], LANE) for i in range(5)]
        return jnp.concatenate(parts, axis=-1)

    w_rw = pad_lora(w_rw_raw)
    mu = pad_lora(rw_mu)[:, None, :]
    w_ml = _pad_cols(w_ml, q_rank + kv_rank + LANE)

    def pad_rows(a):
        return jnp.pad(a, ((0, 0),) * (a.ndim - 2) + ((0, LANE - a.shape[-2]), (0, 0)))

    w2 = pad_rows(rw_w2).astype(BF16)
    a2 = pad_rows(rw_a2).astype(BF16)
    g2 = pad_rows(rw_g2).astype(BF16)

    wq = mla_w_uq.reshape(depth, q_rank, n_heads, MLA_NOPE + MLA_ROPE)
    wq = _pad_cols(wq, MLA_NOPE + LANE).reshape(depth, q_rank, n_heads * (MLA_NOPE + LANE)).astype(BF16)
    wkv = mla_w_ukv.reshape(depth, kv_rank, n_heads, MLA_NOPE + MLA_V)
    wkv = jnp.concatenate([wkv[..., :MLA_NOPE].reshape(depth, kv_rank, -1),
                           wkv[..., MLA_NOPE:].reshape(depth, kv_rank, -1)], axis=-1).astype(BF16)
    cos, sin = _rope_tables(n_ctx, n_lat)

    w_out_b = w_out.astype(BF16)
    tk = _pick(d_ff, (512, 256, 128))
    up_b = ffn_up.astype(BF16)
    down_b = ffn_down.astype(BF16)

    xs = jnp.concatenate([ctx, x], axis=1)
    for l in range(depth):
        gain0, gain1 = norm_g[l, 0:1], norm_g[l, 1:2]
        mt = modtab[l]
        h = _norm(xs, gain0, mt, tb, nb)
        p_hg = _in_proj(h, w_hg[l], tb, nb, hg_cols)
        p_rw = _in_proj(h, w_rw[l], tb, nb, w_rw.shape[-1])
        p_ml = _in_proj(h, w_ml[l], tb, nb_mm, w_ml.shape[-1])

        hg_f, hg_b = _hg_scan(p_hg, lower[l], tb)
        hg_o = _hg_finish(hg_f, hg_b, p_hg, hg_gn[l][None, :], tb)

        r, v, kk, w_f, k_f, b_f, w_b, k_b, b_b, g = _rw_prep(
            p_rw, mu[l], rw_w0[l], w2[l], rw_a0[l], a2[l], g2[l], rw_kk[l][None, :], rw_ka[l][None, :], tb)
        rw_f, rw_b = _rw_scan(r, v, kk, w_f, k_f, b_f, w_b, k_b, b_b, tb)
        rw_o = _rw_finish(rw_f, rw_b, r, k_f, k_b, v, g, rw_gn_w[l][None, :], rw_gn_b[l][None, :],
                          rw_rk[l].reshape(1, rw_w), tb)

        q, k, vv = _mla_prep(p_ml, cos, sin, mla_q_norm[l][None, :], mla_kv_norm[l][None, :], wq[l], wkv[l],
                             mla_qn_g[l][None, :], _pad_cols(mla_qr_g[l][None, :], LANE),
                             mla_kn_g[l][None, :], _pad_cols(mla_kr_g[l][None, :], LANE), tb)
        ml_o = _mla_attn(q, k, vv, tb)

        last = l == depth - 1
        xs = _out_proj(xs, hg_o, rw_o, ml_o, w_out_b[l], mt, tb, nb, d, 1 if last else 0)
        xs = _ffn(xs, gain1, mt, up_b[l], ffn_dw[l], ffn_db[l][None, :], down_b[l], tb, nb, tk, 0 if last else 1)
    return xs
```

```python
import functools
import math

import jax
import jax.numpy as jnp
from jax import lax
from jax.experimental import pallas as pl
from jax.experimental.pallas import tpu as pltpu

F32 = jnp.float32
BF16 = jnp.bfloat16

NORM_EPS = 1e-6
GRID_W = 64
ROPE_THETA = 10000.0
HG_HD = 128
HG_CHUNK = 16
RW_HD = 64
RW_GN_EPS = 64e-5
RW_DECAY_MAX = math.exp(-0.5)
MLA_V = 128
MLA_NOPE = 128
MLA_ROPE = 64
MLA_SCALE = (MLA_NOPE + MLA_ROPE) ** -0.5
LOG2E = math.log2(math.e)
LANE = 128
SUBLANE = 8
RW_CHUNK = 64
NORM_ROW_TILE = 16
ATTN_HEADS = 4
FFN_ROW_TILE = 32
VMEM_LIMIT = 56 * 1024 * 1024


def _cparams(*sem):
    return pltpu.CompilerParams(dimension_semantics=sem, vmem_limit_bytes=VMEM_LIMIT)


def _bdot(a, b):
    return jnp.dot(a.astype(BF16), b.astype(BF16), preferred_element_type=F32)


def _sigmoid(x):
    return 1.0 / (1.0 + jnp.exp(-x))


def _silu(x):
    return x * _sigmoid(x)


def _pad_cols(a, width):
    return jnp.pad(a, [(0, 0)] * (a.ndim - 1) + [(0, width - a.shape[-1])])


def _mods_kernel(c_ref, w_ref, b_ref, o_ref):
    o_ref[...] = _bdot(_silu(c_ref[...]), w_ref[...]) + b_ref[...]


def _mods(cc, w_mod, b_mod):
    n_layers, d, n = w_mod.shape
    rows = cc.shape[0]
    tn = 768 if n % 768 == 0 else n
    return pl.pallas_call(
        _mods_kernel,
        out_shape=jax.ShapeDtypeStruct((n_layers, rows, n), F32),
        grid=(n_layers, n // tn),
        in_specs=[
            pl.BlockSpec((rows, d), lambda l, j: (0, 0)),
            pl.BlockSpec((None, d, tn), lambda l, j: (l, 0, j)),
            pl.BlockSpec((None, 1, tn), lambda l, j: (l, 0, j)),
        ],
        out_specs=pl.BlockSpec((None, rows, tn), lambda l, j: (l, 0, j)),
        compiler_params=_cparams("parallel", "parallel"),
    )(cc, w_mod, b_mod.reshape(n_layers, 1, n))


def _norm_mod(x, gain, shift, scale):
    y = x * lax.rsqrt(jnp.mean(x * x, axis=-1, keepdims=True) + NORM_EPS) * gain
    return y * (1.0 + scale) + shift


def _norm_mod_rows(x_ref, h_ref, g_ref, sh_ref, sc_ref):
    nb, tb, _ = x_ref.shape
    rt = NORM_ROW_TILE
    for n in range(nb):
        mult = g_ref[...] * (1.0 + sc_ref[n])
        shift = sh_ref[n]

        def tile(i, carry, n=n, mult=mult, shift=shift):
            r = pl.multiple_of(i * rt, rt)
            x = x_ref[n, pl.ds(r, rt), :]
            rs = lax.rsqrt(jnp.mean(x * x, axis=-1, keepdims=True) + NORM_EPS)
            h = (x * rs * mult + shift).astype(h_ref.dtype)
            if len(h_ref.shape) == 3:
                h_ref[n, pl.ds(r, rt), :] = h
            else:
                h_ref[pl.ds(pl.multiple_of(n * tb + r, rt), rt), :] = h
            return carry

        lax.fori_loop(0, tb // rt, tile, 0, unroll=4)


def _mod_spec(m, nb, d):
    return pl.BlockSpec((None, None, nb, 1, d), lambda bi, j: (m, jnp.minimum(j, 1), bi, 0, 0))


def _norm_kernel(x_ref, g_ref, sh_ref, sc_ref, h_ref):
    _norm_mod_rows(x_ref, h_ref, g_ref, sh_ref, sc_ref)


def _norm(x, gain, modtab, tb, nb):
    b, t, d = x.shape
    return pl.pallas_call(
        _norm_kernel,
        out_shape=jax.ShapeDtypeStruct((b, t, d), BF16),
        grid=(b // nb, t // tb),
        in_specs=[
            pl.BlockSpec((nb, tb, d), lambda bi, j: (bi, j, 0)),
            pl.BlockSpec((1, d), lambda bi, j: (0, 0)),
            _mod_spec(0, nb, d),
            _mod_spec(1, nb, d),
        ],
        out_specs=pl.BlockSpec((nb, tb, d), lambda bi, j: (bi, j, 0)),
        compiler_params=_cparams("parallel", "parallel"),
    )(x, gain, modtab, modtab)


def _in_proj_kernel(h_ref, w_ref, o_ref):
    nb, tb, d = h_ref.shape
    o_ref[...] = jnp.dot(h_ref[...].reshape(nb * tb, d), w_ref[...],
                         preferred_element_type=F32).reshape(o_ref.shape)


def _in_proj(h, w, tb, nb, tn):
    b, t, d = h.shape
    n = w.shape[1]
    return pl.pallas_call(
        _in_proj_kernel,
        out_shape=jax.ShapeDtypeStruct((b, t, n), F32),
        grid=(b // nb, t // tb, n // tn),
        in_specs=[
            pl.BlockSpec((nb, tb, d), lambda bi, j, k: (bi, j, 0)),
            pl.BlockSpec((d, tn), lambda bi, j, k: (0, k)),
        ],
        out_specs=pl.BlockSpec((nb, tb, tn), lambda bi, j, k: (bi, j, k)),
        compiler_params=_cparams("parallel", "parallel", "arbitrary"),
    )(h, w)


def _out_proj_kernel(x_ref, hg_ref, rw_ref, ml_ref, w_ref, gate_ref, o_ref):
    nb, tb, tn = o_ref.shape
    mixed = jnp.concatenate([hg_ref[...], rw_ref[...], ml_ref[...]], axis=-1)
    acc = jnp.dot(mixed.reshape(nb * tb, mixed.shape[-1]), w_ref[...], preferred_element_type=F32)
    o_ref[...] = x_ref[...] + gate_ref[...] * acc.reshape(nb, tb, tn)


def _out_proj(x, hg, rw, ml, w, modtab, tb, nb, tn, j0):
    b, t, d = x.shape
    return pl.pallas_call(
        _out_proj_kernel,
        out_shape=jax.ShapeDtypeStruct((b, t - j0 * tb, d), F32),
        grid=(b // nb, t // tb - j0, d // tn),
        in_specs=[
            pl.BlockSpec((nb, tb, tn), lambda bi, j, k: (bi, j + j0, k)),
            pl.BlockSpec((nb, tb, hg.shape[-1]), lambda bi, j, k: (bi, j + j0, 0)),
            pl.BlockSpec((nb, tb, rw.shape[-1]), lambda bi, j, k: (bi, j + j0, 0)),
            pl.BlockSpec((nb, tb, ml.shape[-1]), lambda bi, j, k: (bi, j + j0, 0)),
            pl.BlockSpec((d, tn), lambda bi, j, k: (0, k)),
            pl.BlockSpec((None, None, nb, 1, tn), lambda bi, j, k: (2, jnp.minimum(j + j0, 1), bi, 0, k)),
        ],
        out_specs=pl.BlockSpec((nb, tb, tn), lambda bi, j, k: (bi, j, k)),
        compiler_params=_cparams("parallel", "parallel", "arbitrary"),
    )(x, hg, rw, ml, w, modtab)


def _shift_rows(u, tb, first_rows, last_rows):
    rows = u.shape[0]
    row = lax.broadcasted_iota(jnp.int32, u.shape, 0)
    prev = pltpu.roll(u, 1, 0)
    nxt = pltpu.roll(u, rows - 1, 0)
    for n, (fr, lr) in enumerate(zip(first_rows, last_rows)):
        prev = jnp.where(row == n * tb, fr, prev)
        nxt = jnp.where(row == n * tb + tb - 1, lr, nxt)
    return prev, nxt


def _ffn_kernel(ctx_blocks, x_ref, xp_ref, xn_ref, g_ref, sh_ref, sc_ref, gate_ref, wa_ref, wb_ref, dwa_ref,
                dwb_ref, dba_ref, dbb_ref, wd_ref, o_ref, h_ref, ua_ref, ub_ref, act_ref):
    nb, tb, d = x_ref.shape
    rows = nb * tb
    halo = 2 * SUBLANE
    rt = FFN_ROW_TILE
    j = pl.program_id(1)
    k = pl.program_id(2)
    n_k = pl.num_programs(2)
    n_j = pl.num_programs(1)
    trow = lax.broadcasted_iota(jnp.int32, (rt, act_ref.shape[1]), 0)

    def conv(u_ref, dw, db, n, r):
        base = n * tb + r
        mid = u_ref[base:base + rt, :]
        if r == 0:
            edge = u_ref[rows + n * halo + SUBLANE - 1:rows + n * halo + SUBLANE, :]
            prev = jnp.where(trow == 0, edge, pltpu.roll(mid, 1, 0))
        else:
            prev = u_ref[base - 1:base - 1 + rt, :]
        if r == tb - rt:
            edge = u_ref[rows + n * halo + SUBLANE:rows + n * halo + SUBLANE + 1, :]
            nxt = jnp.where(trow == rt - 1, edge, pltpu.roll(mid, rt - 1, 0))
        else:
            nxt = u_ref[base + 1:base + 1 + rt, :]
        return prev * dw[0:1] + mid * dw[1:2] + nxt * dw[2:3] + db

    @pl.when(k == 0)
    def _():
        p_ok = (j >= ctx_blocks + 1).astype(F32)
        n_ok = jnp.logical_and(j >= ctx_blocks, j < n_j - 1).astype(F32)
        g, sh, sc = g_ref[...], sh_ref[...], sc_ref[...]
        _norm_mod_rows(x_ref, h_ref, g_ref, sh_ref, sc_ref)
        hp = (_norm_mod(xp_ref[...], g, sh, sc) * p_ok).astype(BF16)
        hn = (_norm_mod(xn_ref[...], g, sh, sc) * n_ok).astype(BF16)
        for n in range(nb):
            h_ref[rows + n * halo:rows + n * halo + SUBLANE, :] = hp[n]
            h_ref[rows + n * halo + SUBLANE:rows + (n + 1) * halo, :] = hn[n]
        o_ref[...] = jnp.zeros_like(o_ref)

    ua_ref[...] = jnp.dot(h_ref[...], wa_ref[...], preferred_element_type=F32)
    ub_ref[...] = jnp.dot(h_ref[...], wb_ref[...], preferred_element_type=F32)
    dwa, dwb, dba, dbb = dwa_ref[...], dwb_ref[...], dba_ref[...], dbb_ref[...]
    for n in range(nb):
        for r in range(0, tb, rt):
            a = conv(ua_ref, dwa, dba, n, r)
            b = conv(ub_ref, dwb, dbb, n, r)
            act_ref[n * tb + r:n * tb + r + rt, :] = (_silu(a) * b).astype(BF16)
    o_ref[...] += jnp.dot(act_ref[...], wd_ref[...], preferred_element_type=F32).reshape(nb, tb, d)

    @pl.when(k == n_k - 1)
    def _():
        o_ref[...] = x_ref[...] + gate_ref[...] * o_ref[...]


def _ffn(x, gain, modtab, w_up, dw, db, w_down, tb, nb, tk, ctx_blocks):
    b, t, d = x.shape
    n_k = w_down.shape[0] // tk
    r8 = tb // SUBLANE
    n8 = t // SUBLANE
    rows_ext = nb * tb + nb * 2 * SUBLANE

    def mspec(m):
        return pl.BlockSpec((None, None, nb, 1, d),
                            lambda bi, j, k: (m, jnp.minimum(j, 1) if ctx_blocks else 1, bi, 0, 0))

    return pl.pallas_call(
        functools.partial(_ffn_kernel, ctx_blocks),
        out_shape=jax.ShapeDtypeStruct((b, t, d), F32),
        grid=(b // nb, t // tb, n_k),
        in_specs=[
            pl.BlockSpec((nb, tb, d), lambda bi, j, k: (bi, j, 0)),
            pl.BlockSpec((nb, SUBLANE, d), lambda bi, j, k: (bi, jnp.maximum(j * r8 - 1, 0), 0)),
            pl.BlockSpec((nb, SUBLANE, d), lambda bi, j, k: (bi, jnp.minimum((j + 1) * r8, n8 - 1), 0)),
            pl.BlockSpec((1, d), lambda bi, j, k: (0, 0)),
            mspec(3), mspec(4), mspec(5),
            pl.BlockSpec((d, tk), lambda bi, j, k: (0, k)),
            pl.BlockSpec((d, tk), lambda bi, j, k: (0, n_k + k)),
            pl.BlockSpec((3, tk), lambda bi, j, k: (0, k)),
            pl.BlockSpec((3, tk), lambda bi, j, k: (0, n_k + k)),
            pl.BlockSpec((1, tk), lambda bi, j, k: (0, k)),
            pl.BlockSpec((1, tk), lambda bi, j, k: (0, n_k + k)),
            pl.BlockSpec((tk, d), lambda bi, j, k: (k, 0)),
        ],
        out_specs=pl.BlockSpec((nb, tb, d), lambda bi, j, k: (bi, j, 0)),
        scratch_shapes=[pltpu.VMEM((rows_ext, d), BF16), pltpu.VMEM((rows_ext, tk), F32),
                        pltpu.VMEM((rows_ext, tk), F32), pltpu.VMEM((nb * tb, tk), BF16)],
        compiler_params=_cparams("parallel", "parallel", "arbitrary"),
    )(x, x, x, gain, modtab, modtab, modtab, w_up, w_up, dw, dw, db, db, w_down)


def _chunk_cumsum(x, c, reverse):
    rows = x.shape[0]
    ri = lax.broadcasted_iota(jnp.int32, (rows, rows), 0)
    ci = lax.broadcasted_iota(jnp.int32, (rows, rows), 1)
    tri = jnp.logical_and((ri // c) == (ci // c), (ci >= ri) if reverse else (ci <= ri)).astype(BF16)
    hi = x.astype(BF16)
    rest = x - hi.astype(F32)
    mid = rest.astype(BF16)
    lo = (rest - mid.astype(F32)).astype(BF16)
    return (jnp.dot(tri, hi, preferred_element_type=F32) + jnp.dot(tri, mid, preferred_element_type=F32)
            + jnp.dot(tri, lo, preferred_element_type=F32))


def _bwd_block(j, n_j):
    return jnp.where(j == 0, 0, n_j - j)


def _hg_scan_kernel(qf_ref, zf_ref, if_ref, qb_ref, zb_ref, ib_ref, lb_ref, of_ref, ob_ref,
                    st_ref, g_ref, k_ref):
    nb, tb, w = zf_ref.shape
    n_heads = w // HG_HD
    n_chunks = tb // HG_CHUNK
    c = HG_CHUNK

    @pl.when(pl.program_id(1) == 0)
    def _():
        st_ref[...] = jnp.zeros_like(st_ref)

    ones = jnp.ones((HG_HD, HG_HD), BF16)
    trow = lax.broadcasted_iota(jnp.int32, (c, HG_HD), 0)

    dirs = ((qf_ref, zf_ref, if_ref, of_ref), (qb_ref, zb_ref, ib_ref, ob_ref))
    for n in range(nb):
        for dr, (q_ref, z_ref, i_ref, o_ref) in enumerate(dirs):
            lb = lb_ref[dr:dr + 1, :]
            f = lb + (1.0 - lb) * _sigmoid(z_ref[n])
            g_ref[n, dr] = _chunk_cumsum(jnp.log2(f), c, dr == 1)
            k_ref[n, dr] = 1.0 - f

    def dir_chunk(n, dr, r0):
        q_ref, _z_ref, i_ref, o_ref = dirs[dr]
        heads = []
        tiles = []
        for h in range(n_heads):
            ls = slice(h * HG_HD, (h + 1) * HG_HD)
            q = q_ref[n, pl.ds(r0, c), ls]
            v = i_ref[n, pl.ds(r0, c), ls]
            g = g_ref[n, dr, pl.ds(r0, c), ls]
            kk = k_ref[n, dr, pl.ds(r0, c), ls]
            heads.append((ls, q, v, g, kk))
            for s in range(c):
                seen = (trow >= s) if dr == 0 else (trow <= s)
                tiles.append(jnp.where(seen, q * kk[s:s + 1] * jnp.exp2(g - g[s:s + 1]), 0.0).astype(BF16))
        attn = jnp.dot(jnp.concatenate(tiles, axis=0), ones, preferred_element_type=F32)
        inter = []
        for h, (ls, q, v, g, kk) in enumerate(heads):
            g_last = g[c - 1:c] if dr == 0 else g[0:1]
            st = st_ref[n, dr, h]
            inter.append(lax.dot_general((q * jnp.exp2(g)).astype(BF16), st.astype(BF16),
                                         (((1,), (1,)), ((), ())), preferred_element_type=F32))
            kd = kk * jnp.exp2(g_last - g)
            st_ref[n, dr, h] = st * jnp.exp2(g_last) + lax.dot_general(
                v.astype(BF16), kd.astype(BF16), (((0,), (0,)), ((), ())), preferred_element_type=F32)
        yield
        for h, (ls, q, v, g, kk) in enumerate(heads):
            o = inter[h]
            for s in range(c):
                o += attn[(h * c + s) * c:(h * c + s + 1) * c] * v[s:s + 1]
            o_ref[n, pl.ds(r0, c), ls] = o

    def chunk(ic, carry):
        live = [dir_chunk(n, dr, pl.multiple_of((ic if dr == 0 else n_chunks - 1 - ic) * c, c))
                for n in range(nb) for dr in range(2)]
        while live:
            live = [gen for gen in live if next(gen, True) is None]
        return carry

    lax.fori_loop(0, n_chunks, chunk, 0)


def _hg_scan(p_hg, lb, tb, nb):
    b, t, _ = p_hg.shape
    w = lb.shape[-1]
    n_j = t // tb

    def fwd(col):
        return pl.BlockSpec((nb, tb, w), lambda bi, j: (bi, j, col))

    def bwd(col):
        return pl.BlockSpec((nb, tb, w), lambda bi, j: (bi, _bwd_block(j, n_j), col))

    out = jax.ShapeDtypeStruct((b, t, w), F32)
    return pl.pallas_call(
        _hg_scan_kernel,
        out_shape=(out, out),
        grid=(b // nb, n_j),
        in_specs=[fwd(0), fwd(1), fwd(3), bwd(0), bwd(2), bwd(3), pl.BlockSpec((2, w), lambda bi, j: (0, 0))],
        out_specs=(fwd(0), bwd(0)),
        scratch_shapes=[pltpu.VMEM((nb, 2, w // HG_HD, HG_HD, HG_HD), F32), pltpu.VMEM((nb, 2, tb, w), F32),
                        pltpu.VMEM((nb, 2, tb, w), F32)],
        compiler_params=_cparams("parallel", "arbitrary"),
    )(p_hg, p_hg, p_hg, p_hg, p_hg, p_hg, lb)


def _hg_finish_kernel(of_ref, ob_ref, gate_ref, gn_ref, o_ref):
    w = of_ref.shape[-1]
    for h in range(w // HG_HD):
        ls = slice(h * HG_HD, (h + 1) * HG_HD)
        o = of_ref[:, ls] + ob_ref[:, ls]
        o = o * lax.rsqrt(jnp.mean(o * o, axis=-1, keepdims=True) + NORM_EPS) * gn_ref[...]
        o_ref[:, ls] = (o * _silu(gate_ref[:, ls])).astype(o_ref.dtype)


def _hg_finish(o_f, o_b, p_hg, gn, tb):
    b, t, w = o_f.shape
    spec = pl.BlockSpec((None, tb, w), lambda bi, j: (bi, j, 0))
    return pl.pallas_call(
        _hg_finish_kernel,
        out_shape=jax.ShapeDtypeStruct((b, t, w), BF16),
        grid=(b, t // tb),
        in_specs=[spec, spec, pl.BlockSpec((None, tb, w), lambda bi, j: (bi, j, 4)),
                  pl.BlockSpec((1, HG_HD), lambda bi, j: (0, 0))],
        out_specs=spec,
        compiler_params=_cparams("parallel", "parallel"),
    )(o_f, o_b, p_hg, gn)


def _head_ones(width, hd):
    r = lax.broadcasted_iota(jnp.int32, (width, width), 0) // hd
    c = lax.broadcasted_iota(jnp.int32, (width, width), 1) // hd
    return (r == c).astype(F32)


def _head_sum(x, hd):
    same_head = _head_ones(LANE, hd).astype(BF16)
    out = []
    for i in range(x.shape[-1] // LANE):
        xs = x[:, i * LANE:(i + 1) * LANE]
        hi = xs.astype(BF16)
        lo = (xs - hi.astype(F32)).astype(BF16)
        out.append(jnp.dot(hi, same_head, preferred_element_type=F32)
                   + jnp.dot(lo, same_head, preferred_element_type=F32))
    return jnp.concatenate(out, axis=-1)


def _rw_prep_kernel(p_ref, pp_ref, pn_ref, mu_ref, w0_ref, w2_ref, a0_ref, a2_ref, g2_ref, kk_ref,
                    ka_ref, r_o, v_o, kk_o, wf_o, kf_o, bf_o, wb_o, kb_o, bb_o, g_o):
    tb = p_ref.shape[0]
    w = r_o.shape[-1]
    lw = w2_ref.shape[1]
    j = pl.program_id(1)
    n_j = pl.num_programs(1)
    p = p_ref[...]
    p_ok = (j >= 2).astype(F32)
    n_ok = jnp.logical_and(j >= 1, j < n_j - 1).astype(F32)
    prev, nxt = _shift_rows(p, tb, [pp_ref[SUBLANE - 1:SUBLANE, :] * p_ok], [pn_ref[0:1, :] * n_ok])
    s = p + mu_ref[...] * (0.5 * (prev + nxt) - p)
    r, k, v = s[:, 0:w], s[:, w:2 * w], s[:, 2 * w:3 * w]
    lora = [s[:, 3 * w + i * lw:3 * w + (i + 1) * lw] for i in range(5)]
    kk = k * kk_ref[...]
    ssq = _head_sum(kk * kk, RW_HD)
    kk = kk / jnp.maximum(jnp.sqrt(ssq), 1e-12)
    r_o[...] = r
    v_o[...] = v
    kk_o[...] = kk
    for dr, (w_o, k_o, b_o) in enumerate(((wf_o, kf_o, bf_o), (wb_o, kb_o, bb_o))):
        xw, xa = lora[dr], lora[2 + dr]
        w_o[...] = -RW_DECAY_MAX * _sigmoid(w0_ref[dr:dr + 1, :] + _bdot(jnp.tanh(xw), w2_ref[dr]))
        a = _sigmoid(a0_ref[dr:dr + 1, :] + _bdot(xa, a2_ref[dr]))
        k_o[...] = k * (1.0 + (a - 1.0) * ka_ref[...])
        b_o[...] = kk * a
    g_o[...] = _bdot(_sigmoid(lora[4]), g2_ref[...])


def _rw_prep(p_rw, mu, w0, w2, a0, a2, g2, k_k, k_a, tb):
    b, t, n = p_rw.shape
    w = w0.shape[-1]
    r8, n8 = tb // SUBLANE, t // SUBLANE
    out = jax.ShapeDtypeStruct((b, t, w), F32)
    ospec = pl.BlockSpec((None, tb, w), lambda bi, j: (bi, j, 0))

    def const(a):
        return pl.BlockSpec(a.shape, lambda bi, j: (0,) * a.ndim)

    return pl.pallas_call(
        _rw_prep_kernel,
        out_shape=(out,) * 10,
        grid=(b, t // tb),
        in_specs=[
            pl.BlockSpec((None, tb, n), lambda bi, j: (bi, j, 0)),
            pl.BlockSpec((None, SUBLANE, n), lambda bi, j: (bi, jnp.maximum(j * r8 - 1, 0), 0)),
            pl.BlockSpec((None, SUBLANE, n), lambda bi, j: (bi, jnp.minimum((j + 1) * r8, n8 - 1), 0)),
            const(mu), const(w0), const(w2), const(a0), const(a2), const(g2), const(k_k), const(k_a),
        ],
        out_specs=(ospec,) * 10,
        compiler_params=_cparams("parallel", "parallel"),
    )(p_rw, p_rw, p_rw, mu, w0, w2, a0, a2, g2, k_k, k_a)


def _rw_scan_kernel(rf_ref, vf_ref, kkf_ref, wf_ref, kf_ref, bf_ref, rb_ref, vb_ref, kkb_ref, wb_ref,
                    kb_ref, bb_ref, of_ref, ob_ref, st_ref, g_ref):
    tb, w = rf_ref.shape
    n_pairs = w // LANE
    hd = RW_HD
    c = RW_CHUNK
    n_chunks = tb // c
    assert c == hd and tb % c == 0

    @pl.when(pl.program_id(1) == 0)
    def _():
        st_ref[...] = jnp.zeros_like(st_ref)

    dirs = ((rf_ref, vf_ref, kkf_ref, wf_ref, kf_ref, bf_ref, of_ref),
            (rb_ref, vb_ref, kkb_ref, wb_ref, kb_ref, bb_ref, ob_ref))
    for dr, refs in enumerate(dirs):
        g_ref[dr] = _chunk_cumsum(refs[3][...], c, dr == 1)

    row = lax.broadcasted_iota(jnp.int32, (c, LANE), 0)
    col = lax.broadcasted_iota(jnp.int32, (c, LANE), 1) % hd
    head_a = lax.broadcasted_iota(jnp.int32, (c, LANE), 1) < hd
    head_a2 = (lax.broadcasted_iota(jnp.int32, (c, 2 * LANE), 1) % LANE) < hd
    r2 = lax.broadcasted_iota(jnp.int32, (LANE, LANE), 0)
    c2 = lax.broadcasted_iota(jnp.int32, (LANE, LANE), 1)
    same_head = (r2 // hd) == (c2 // hd)
    eye = r2 == c2

    def stack2(y):
        m = head_a if y.shape[1] == LANE else head_a2
        return jnp.concatenate([jnp.where(m, y, 0.0), jnp.where(m, 0.0, y)], axis=0).astype(BF16)

    def tn(x, y):
        return lax.dot_general(x.astype(BF16), y.astype(BF16), (((0,), (0,)), ((), ())),
                               preferred_element_type=F32)

    def pair_chunk(dr, hp, r0):
        refs = dirs[dr]
        before = (col < row) if dr == 0 else (col > row)
        upto = (col <= row) if dr == 0 else (col >= row)
        ls = slice(hp * LANE, (hp + 1) * LANE)
        rr, vv, kk, lw, kd, bb = [ref[pl.ds(r0, c), ls] for ref in refs[:6]]
        g = g_ref[dr, pl.ds(r0, c), ls]
        g_end = g[c - 1:c] if dr == 0 else g[0:1]
        at = -kk * jnp.exp(g - lw)
        rt = rr * jnp.exp(g)
        e_inv = jnp.exp(-g)
        e_out = jnp.exp(g_end - g)
        bh_kh = jnp.concatenate([bb * e_out, kd * e_out], axis=0)
        abk = lax.dot_general(jnp.concatenate([at, rt], axis=0).astype(BF16),
                              jnp.concatenate([stack2(bb * e_inv), stack2(kd * e_inv)], axis=0),
                              (((1,), (1,)), ((), ())), preferred_element_type=F32)
        yield
        x = jnp.where(before, abk[0:c, 0:LANE], 0.0)
        a_ak = jnp.where(before, abk[0:c, LANE:], 0.0)
        a_r = jnp.concatenate([jnp.where(upto, abk[c:, 0:LANE], 0.0), jnp.where(upto, abk[c:, LANE:], 0.0)], axis=1)
        u0 = jnp.dot(a_ak.astype(BF16), stack2(vv), preferred_element_type=F32)
        yield
        y = jnp.concatenate([at, u0], axis=1)
        span = 1
        while span < c:
            span *= 2
            rhs = stack2(y) if span >= c else jnp.concatenate([stack2(y), stack2(x)], axis=1)
            z = jnp.dot(x.astype(BF16), rhs, preferred_element_type=F32)
            yield
            y = y + z[:, 0:2 * LANE]
            if span < c:
                x = z[:, 2 * LANE:]
        zero = jnp.zeros((c, LANE), F32)
        v_pad = jnp.concatenate([zero, vv], axis=1)
        wu = jnp.dot(a_r.astype(BF16), jnp.concatenate([stack2(y), stack2(v_pad)], axis=0),
                     preferred_element_type=F32)
        mn = tn(bh_kh, jnp.concatenate([y, v_pad], axis=0))
        yield
        q_eff = rt + wu[:, 0:LANE]
        m_bd = jnp.where(same_head, mn[:, 0:LANE], 0.0) + jnp.where(eye, jnp.exp(g_end), 0.0)
        n_bd = jnp.where(same_head, mn[:, LANE:], 0.0)
        st = st_ref[dr * n_pairs + hp].astype(BF16)
        so = jnp.dot(jnp.concatenate([q_eff, m_bd], axis=0).astype(BF16), st, preferred_element_type=F32)
        yield
        refs[6][pl.ds(r0, c), ls] = so[0:c] + wu[:, LANE:]
        st_ref[dr * n_pairs + hp] = so[c:] + n_bd

    def chunk(ic, carry):
        live = [pair_chunk(dr, hp, pl.multiple_of((ic if dr == 0 else n_chunks - 1 - ic) * c, c))
                for dr in range(2) for hp in range(n_pairs)]
        while live:
            live = [gen for gen in live if next(gen, True) is None]
        return carry

    lax.fori_loop(0, n_chunks, chunk, 0)


def _rw_scan(r, v, kk, w_f, k_f, b_f, w_b, k_b, b_b, tb):
    b, t, w = r.shape
    n_j = t // tb
    fwd = pl.BlockSpec((None, tb, w), lambda bi, j: (bi, j, 0))
    bwd = pl.BlockSpec((None, tb, w), lambda bi, j: (bi, _bwd_block(j, n_j), 0))
    out = jax.ShapeDtypeStruct((b, t, w), F32)
    return pl.pallas_call(
        _rw_scan_kernel,
        out_shape=(out, out),
        grid=(b, n_j),
        in_specs=[fwd] * 6 + [bwd] * 6,
        out_specs=(fwd, bwd),
        scratch_shapes=[pltpu.VMEM((2 * (w // LANE), LANE, LANE), F32), pltpu.VMEM((2, tb, w), F32)],
        compiler_params=_cparams("parallel", "arbitrary"),
    )(r, v, kk, w_f, k_f, b_f, r, v, kk, w_b, k_b, b_b)


def _rw_finish_kernel(of_ref, ob_ref, r_ref, kf_ref, kb_ref, v_ref, g_ref, gnw_ref, gnb_ref, rk_ref, o_ref):
    o = of_ref[...] + ob_ref[...]
    cen = o - _head_sum(o, RW_HD) * (1.0 / RW_HD)
    var = _head_sum(cen * cen, RW_HD) * (1.0 / RW_HD)
    o = cen * lax.rsqrt(var + RW_GN_EPS) * gnw_ref[...] + gnb_ref[...]
    dot_rk = _head_sum(r_ref[...] * (kf_ref[...] + kb_ref[...]) * rk_ref[...], RW_HD)
    o_ref[...] = ((o + dot_rk * v_ref[...]) * g_ref[...]).astype(o_ref.dtype)


def _rw_finish(o_f, o_b, r, k_f, k_b, v, g, gn_w, gn_b, r_k, tb):
    b, t, w = o_f.shape
    spec = pl.BlockSpec((None, tb, w), lambda bi, j: (bi, j, 0))
    cspec = pl.BlockSpec((1, w), lambda bi, j: (0, 0))
    return pl.pallas_call(
        _rw_finish_kernel,
        out_shape=jax.ShapeDtypeStruct((b, t, w), BF16),
        grid=(b, t // tb),
        in_specs=[spec] * 7 + [cspec] * 3,
        out_specs=spec,
        compiler_params=_cparams("parallel", "parallel"),
    )(o_f, o_b, r, k_f, k_b, v, g, gn_w, gn_b, r_k)


def _rms(x, gain, n):
    return x * lax.rsqrt(jnp.sum(x * x, axis=-1, keepdims=True) * (1.0 / n) + NORM_EPS) * gain


def _rope128(x, cos, sin):
    half = MLA_ROPE // 2
    lane = lax.broadcasted_iota(jnp.int32, x.shape, 1)
    swapped = jnp.where(lane < half, pltpu.roll(x, LANE - half, 1), pltpu.roll(x, half, 1))
    return x * cos + swapped * sin


def _mla_prep_kernel(p_ref, cos_ref, sin_ref, qn_ref, kvn_ref, wq_ref, wkv_ref, qng_ref, qrg_ref,
                     kng_ref, krg_ref, q_o, k_o, v_o):
    q_rank = qn_ref.shape[-1]
    kv_rank = kvn_ref.shape[-1]
    n_heads = v_o.shape[-1] // MLA_V
    slot = MLA_NOPE + LANE
    cos, sin = cos_ref[...], sin_ref[...]
    p = p_ref[...]
    q = _bdot(_rms(p[:, 0:q_rank], qn_ref[...], q_rank), wq_ref[...])
    kv = _bdot(_rms(p[:, q_rank:q_rank + kv_rank], kvn_ref[...], kv_rank), wkv_ref[...])
    k_rope = _rope128(_rms(p[:, q_rank + kv_rank:], krg_ref[...], MLA_ROPE), cos, sin).astype(k_o.dtype)
    for h in range(n_heads):
        q_nope = _rms(q[:, h * slot:h * slot + MLA_NOPE], qng_ref[...], MLA_NOPE)
        q_rope = _rope128(_rms(q[:, h * slot + MLA_NOPE:(h + 1) * slot], qrg_ref[...], MLA_ROPE), cos, sin)
        q_o[:, h * slot:h * slot + MLA_NOPE] = (q_nope * (MLA_SCALE * LOG2E)).astype(q_o.dtype)
        q_o[:, h * slot + MLA_NOPE:(h + 1) * slot] = (q_rope * (MLA_SCALE * LOG2E)).astype(q_o.dtype)
        k_nope = _rms(kv[:, h * MLA_NOPE:(h + 1) * MLA_NOPE], kng_ref[...], MLA_NOPE)
        k_o[:, h * slot:h * slot + MLA_NOPE] = k_nope.astype(k_o.dtype)
        k_o[:, h * slot + MLA_NOPE:(h + 1) * slot] = k_rope
    v_o[...] = kv[:, n_heads * MLA_NOPE:].astype(v_o.dtype)


def _mla_prep(p_ml, cos, sin, q_norm, kv_norm, w_uq, w_ukv, qn_g, qr_g, kn_g, kr_g, tb):
    b, t, n = p_ml.shape
    n_heads = w_uq.shape[1] // (MLA_NOPE + LANE)

    def const(a):
        return pl.BlockSpec(a.shape, lambda bi, j: (0,) * a.ndim)

    def out(width):
        return (jax.ShapeDtypeStruct((b, t, width), BF16), pl.BlockSpec((None, tb, width), lambda bi, j: (bi, j, 0)))

    outs = [out(n_heads * (MLA_NOPE + LANE)), out(n_heads * (MLA_NOPE + LANE)), out(n_heads * MLA_V)]
    return pl.pallas_call(
        _mla_prep_kernel,
        out_shape=tuple(o[0] for o in outs),
        grid=(b, t // tb),
        in_specs=[
            pl.BlockSpec((None, tb, n), lambda bi, j: (bi, j, 0)),
            pl.BlockSpec((tb, LANE), lambda bi, j: (j, 0)),
            pl.BlockSpec((tb, LANE), lambda bi, j: (j, 0)),
            const(q_norm), const(kv_norm), const(w_uq), const(w_ukv), const(qn_g), const(qr_g),
            const(kn_g), const(kr_g),
        ],
        out_specs=tuple(o[1] for o in outs),
        compiler_params=_cparams("parallel", "parallel"),
    )(p_ml, cos, sin, q_norm, kv_norm, w_uq, w_ukv, qn_g, qr_g, kn_g, kr_g)


def _mla_attn_kernel(q_ref, k_ref, v_ref, o_ref):
    tb = q_ref.shape[0]
    slot = MLA_NOPE + LANE
    n_heads = o_ref.shape[1] // MLA_V

    def head(h, n_keys):
        qs = slice(h * slot, (h + 1) * slot)
        vs = slice(h * MLA_V, (h + 1) * MLA_V)
        s = lax.dot_general(q_ref[:, qs], k_ref[0:n_keys, qs], (((1,), (1,)), ((), ())),
                            preferred_element_type=F32)
        yield
        e = jnp.exp2(s - jnp.max(s, axis=-1, keepdims=True))
        o = jnp.dot(e.astype(BF16), v_ref[0:n_keys, vs], preferred_element_type=F32)
        yield
        o_ref[:, vs] = (o / jnp.sum(e, axis=-1, keepdims=True)).astype(o_ref.dtype)

    def attend(n_keys):
        live = [head(h, n_keys) for h in range(n_heads)]
        while live:
            live = [gen for gen in live if next(gen, True) is None]

    @pl.when(pl.program_id(2) == 0)
    def _():
        attend(tb)

    @pl.when(pl.program_id(2) > 0)
    def _():
        attend(k_ref.shape[0])


def _mla_attn(q, k, v, tb):
    b, t, _ = q.shape
    n_heads = v.shape[-1] // MLA_V
    slot = MLA_NOPE + LANE
    hg = ATTN_HEADS if n_heads % ATTN_HEADS == 0 else 1
    return pl.pallas_call(
        _mla_attn_kernel,
        out_shape=jax.ShapeDtypeStruct((b, t, n_heads * MLA_V), BF16),
        grid=(b, n_heads // hg, t // tb),
        in_specs=[
            pl.BlockSpec((None, tb, hg * slot), lambda bi, h, j: (bi, j, h)),
            pl.BlockSpec((None, t, hg * slot), lambda bi, h, j: (bi, 0, h)),
            pl.BlockSpec((None, t, hg * MLA_V), lambda bi, h, j: (bi, 0, h)),
        ],
        out_specs=pl.BlockSpec((None, tb, hg * MLA_V), lambda bi, h, j: (bi, j, h)),
        compiler_params=_cparams("parallel", "parallel", "arbitrary"),
    )(q, k, v)


def _rope_tables(n_ctx, n_lat):
    rows = n_lat // GRID_W
    row = jnp.repeat(jnp.arange(rows, dtype=F32), GRID_W)
    col = jnp.tile(jnp.arange(GRID_W, dtype=F32), rows)
    axis_dim = MLA_ROPE // 2
    inv_freq = ROPE_THETA ** (-jnp.arange(0, axis_dim, 2, dtype=F32) / axis_dim)
    ang = jnp.concatenate([row[:, None] * inv_freq, col[:, None] * inv_freq], axis=-1)
    ang = jnp.concatenate([jnp.zeros((n_ctx, axis_dim), F32), ang], axis=0)
    cos, sin = jnp.cos(ang), jnp.sin(ang)
    return (_pad_cols(jnp.concatenate([cos, cos], axis=-1), LANE),
            _pad_cols(jnp.concatenate([-sin, sin], axis=-1), LANE))


def _pick(n, prefs):
    for p in prefs:
        if n % p == 0:
            return p
    return n


def kernel(x, c, ctx, c_ctx, norm_g, w_mod, b_mod, w_in, w_out, hg_lb, hg_gn, rw_mu, rw_w0, rw_w2, rw_a0, rw_a2, rw_g2, rw_kk, rw_ka, rw_rk, rw_gn_w, rw_gn_b, mla_q_norm, mla_w_uq, mla_kv_norm, mla_w_ukv, mla_qn_g, mla_qr_g, mla_kn_g, mla_kr_g, ffn_up, ffn_dw, ffn_db, ffn_down):
    bsz, n_lat, d = x.shape
    n_ctx = ctx.shape[1]
    depth = w_in.shape[0]
    hg_w = hg_lb.shape[-1]
    rw_w = rw_w0.shape[-1]
    q_rank = mla_q_norm.shape[-1]
    kv_rank = mla_kv_norm.shape[-1]
    n_heads = mla_w_ukv.shape[-1] // (MLA_NOPE + MLA_V)
    d_ff = ffn_down.shape[1]
    lora = (rw_w2.shape[2], rw_w2.shape[2], rw_a2.shape[2], rw_a2.shape[2], rw_g2.shape[1])
    tb = n_ctx
    nb = 2 if bsz % 2 == 0 else 1
    nb_mm = 4 if bsz % 4 == 0 else nb
    assert n_lat % tb == 0 and tb % HG_CHUNK == 0 and max(lora) <= LANE

    cc = jnp.concatenate([c, c_ctx[None, :]], axis=0)
    cc = jnp.pad(cc, ((0, -(bsz + 1) % SUBLANE), (0, 0)))
    mods = _mods(cc, w_mod, b_mod)
    mod_lat = mods[:, :bsz].reshape(depth, bsz, 6, d).transpose(0, 2, 1, 3)
    mod_ctx = jnp.broadcast_to(mods[:, bsz].reshape(depth, 6, 1, d), (depth, 6, bsz, d))
    modtab = jnp.stack([mod_ctx, mod_lat], axis=2)[:, :, :, :, None, :]

    lb_p = jax.nn.softmax(hg_lb.astype(F32), axis=0)
    lower = jnp.cumsum(lb_p, axis=0) - lb_p[0]

    hg_cols = 5 * hg_w
    rw_cols = 3 * rw_w + sum(lora)
    w_in_b = w_in.astype(BF16)
    w_hg = w_in_b[:, :, :hg_cols]
    w_rw_raw = w_in_b[:, :, hg_cols:hg_cols + rw_cols]
    w_ml = w_in_b[:, :, hg_cols + rw_cols:]
    offs = [3 * rw_w]
    for n in lora:
        offs.append(offs[-1] + n)

    def pad_lora(a):
        parts = [a[..., :3 * rw_w]] + [_pad_cols(a[..., offs[i]:offs[i + 1]], LANE) for i in range(5)]
        return jnp.concatenate(parts, axis=-1)

    w_rw = pad_lora(w_rw_raw)
    mu = pad_lora(rw_mu)[:, None, :]
    w_ml = _pad_cols(w_ml, q_rank + kv_rank + LANE)

    def pad_rows(a):
        return jnp.pad(a, ((0, 0),) * (a.ndim - 2) + ((0, LANE - a.shape[-2]), (0, 0)))

    w2 = pad_rows(rw_w2).astype(BF16)
    a2 = pad_rows(rw_a2).astype(BF16)
    g2 = pad_rows(rw_g2).astype(BF16)

    wq = mla_w_uq.reshape(depth, q_rank, n_heads, MLA_NOPE + MLA_ROPE)
    wq = _pad_cols(wq, MLA_NOPE + LANE).reshape(depth, q_rank, n_heads * (MLA_NOPE + LANE)).astype(BF16)
    wkv = mla_w_ukv.reshape(depth, kv_rank, n_heads, MLA_NOPE + MLA_V)
    wkv = jnp.concatenate([wkv[..., :MLA_NOPE].reshape(depth, kv_rank, -1),
                           wkv[..., MLA_NOPE:].reshape(depth, kv_rank, -1)], axis=-1).astype(BF16)
    cos, sin = _rope_tables(n_ctx, n_lat)

    w_out_b = w_out.astype(BF16)
    tk = _pick(d_ff, (512, 256, 128))
    up_b = ffn_up.astype(BF16)
    down_b = ffn_down.astype(BF16)

    xs = jnp.concatenate([ctx, x], axis=1)
    for l in range(depth):
        gain0, gain1 = norm_g[l, 0:1], norm_g[l, 1:2]
        mt = modtab[l]
        h = _norm(xs, gain0, mt, tb, nb)
        p_hg = _in_proj(h, w_hg[l], tb, nb, hg_cols)
        p_rw = _in_proj(h, w_rw[l], tb, nb, w_rw.shape[-1])
        p_ml = _in_proj(h, w_ml[l], tb, nb_mm, w_ml.shape[-1])

        hg_f, hg_b = _hg_scan(p_hg, lower[l], tb, nb)
        hg_o = _hg_finish(hg_f, hg_b, p_hg, hg_gn[l][None, :], tb)

        r, v, kk, w_f, k_f, b_f, w_b, k_b, b_b, g = _rw_prep(
            p_rw, mu[l], rw_w0[l], w2[l], rw_a0[l], a2[l], g2[l], rw_kk[l][None, :], rw_ka[l][None, :], tb)
        rw_f, rw_b = _rw_scan(r, v, kk, w_f, k_f, b_f, w_b, k_b, b_b, tb)
        rw_o = _rw_finish(rw_f, rw_b, r, k_f, k_b, v, g, rw_gn_w[l][None, :], rw_gn_b[l][None, :],
                          rw_rk[l].reshape(1, rw_w), tb)

        q, k, vv = _mla_prep(p_ml, cos, sin, mla_q_norm[l][None, :], mla_kv_norm[l][None, :], wq[l], wkv[l],
                             mla_qn_g[l][None, :], _pad_cols(mla_qr_g[l][None, :], LANE),
                             mla_kn_g[l][None, :], _pad_cols(mla_kr_g[l][None, :], LANE), tb)
        ml_o = _mla_attn(q, k, vv, tb)

        last = l == depth - 1
        xs = _out_proj(xs, hg_o, rw_o, ml_o, w_out_b[l], mt, tb, nb, d, 1 if last else 0)
        xs = _ffn(xs, gain1, mt, up_b[l], ffn_dw[l], ffn_db[l][None, :], down_b[l], tb, nb, tk, 0 if last else 1)
    return xs
```

```python
import functools
import math

import jax
import jax.numpy as jnp
from jax import lax
from jax.experimental import pallas as pl
from jax.experimental.pallas import tpu as pltpu

F32 = jnp.float32
BF16 = jnp.bfloat16

NORM_EPS = 1e-6
GRID_W = 64
ROPE_THETA = 10000.0
HG_HD = 128
HG_CHUNK = 16
RW_HD = 64
RW_GN_EPS = 64e-5
RW_DECAY_MAX = math.exp(-0.5)
MLA_V = 128
MLA_NOPE = 128
MLA_ROPE = 64
MLA_SCALE = (MLA_NOPE + MLA_ROPE) ** -0.5
LOG2E = math.log2(math.e)
LANE = 128
SUBLANE = 8
RW_CHUNK = 64
NORM_ROW_TILE = 16
ATTN_HEADS = 4
FFN_ROW_TILE = 32
VMEM_LIMIT = 56 * 1024 * 1024


def _cparams(*sem):
    return pltpu.CompilerParams(dimension_semantics=sem, vmem_limit_bytes=VMEM_LIMIT)


def _bdot(a, b):
    return jnp.dot(a.astype(BF16), b.astype(BF16), preferred_element_type=F32)


def _sigmoid(x):
    return 1.0 / (1.0 + jnp.exp(-x))


def _silu(x):
    return x * _sigmoid(x)


def _pad_cols(a, width):
    return jnp.pad(a, [(0, 0)] * (a.ndim - 1) + [(0, width - a.shape[-1])])


def _mods_kernel(c_ref, w_ref, b_ref, o_ref):
    o_ref[...] = _bdot(_silu(c_ref[...]), w_ref[...]) + b_ref[...]


def _mods(cc, w_mod, b_mod):
    n_layers, d, n = w_mod.shape
    rows = cc.shape[0]
    tn = 768 if n % 768 == 0 else n
    return pl.pallas_call(
        _mods_kernel,
        out_shape=jax.ShapeDtypeStruct((n_layers, rows, n), F32),
        grid=(n_layers, n // tn),
        in_specs=[
            pl.BlockSpec((rows, d), lambda l, j: (0, 0)),
            pl.BlockSpec((None, d, tn), lambda l, j: (l, 0, j)),
            pl.BlockSpec((None, 1, tn), lambda l, j: (l, 0, j)),
        ],
        out_specs=pl.BlockSpec((None, rows, tn), lambda l, j: (l, 0, j)),
        compiler_params=_cparams("parallel", "parallel"),
    )(cc, w_mod, b_mod.reshape(n_layers, 1, n))


def _norm_mod(x, gain, shift, scale):
    y = x * lax.rsqrt(jnp.mean(x * x, axis=-1, keepdims=True) + NORM_EPS) * gain
    return y * (1.0 + scale) + shift


def _norm_mod_rows(x_ref, h_ref, g_ref, sh_ref, sc_ref):
    nb, tb, _ = x_ref.shape
    rt = NORM_ROW_TILE
    for n in range(nb):
        mult = g_ref[...] * (1.0 + sc_ref[n])
        shift = sh_ref[n]

        def tile(i, carry, n=n, mult=mult, shift=shift):
            r = pl.multiple_of(i * rt, rt)
            x = x_ref[n, pl.ds(r, rt), :]
            rs = lax.rsqrt(jnp.mean(x * x, axis=-1, keepdims=True) + NORM_EPS)
            h = (x * rs * mult + shift).astype(h_ref.dtype)
            if len(h_ref.shape) == 3:
                h_ref[n, pl.ds(r, rt), :] = h
            else:
                h_ref[pl.ds(pl.multiple_of(n * tb + r, rt), rt), :] = h
            return carry

        lax.fori_loop(0, tb // rt, tile, 0, unroll=4)


def _mod_spec(m, nb, d):
    return pl.BlockSpec((None, None, nb, 1, d), lambda bi, j: (m, jnp.minimum(j, 1), bi, 0, 0))


def _norm_kernel(x_ref, g_ref, sh_ref, sc_ref, h_ref):
    _norm_mod_rows(x_ref, h_ref, g_ref, sh_ref, sc_ref)


def _norm(x, gain, modtab, tb, nb):
    b, t, d = x.shape
    return pl.pallas_call(
        _norm_kernel,
        out_shape=jax.ShapeDtypeStruct((b, t, d), BF16),
        grid=(b // nb, t // tb),
        in_specs=[
            pl.BlockSpec((nb, tb, d), lambda bi, j: (bi, j, 0)),
            pl.BlockSpec((1, d), lambda bi, j: (0, 0)),
            _mod_spec(0, nb, d),
            _mod_spec(1, nb, d),
        ],
        out_specs=pl.BlockSpec((nb, tb, d), lambda bi, j: (bi, j, 0)),
        compiler_params=_cparams("parallel", "parallel"),
    )(x, gain, modtab, modtab)


def _in_proj_kernel(h_ref, w_ref, o_ref):
    nb, tb, d = h_ref.shape
    o_ref[...] = jnp.dot(h_ref[...].reshape(nb * tb, d), w_ref[...],
                         preferred_element_type=F32).reshape(o_ref.shape)


def _in_proj(h, w, tb, nb, tn):
    b, t, d = h.shape
    n = w.shape[1]
    return pl.pallas_call(
        _in_proj_kernel,
        out_shape=jax.ShapeDtypeStruct((b, t, n), F32),
        grid=(b // nb, t // tb, n // tn),
        in_specs=[
            pl.BlockSpec((nb, tb, d), lambda bi, j, k: (bi, j, 0)),
            pl.BlockSpec((d, tn), lambda bi, j, k: (0, k)),
        ],
        out_specs=pl.BlockSpec((nb, tb, tn), lambda bi, j, k: (bi, j, k)),
        compiler_params=_cparams("parallel", "parallel", "arbitrary"),
    )(h, w)


def _out_proj_kernel(x_ref, hg_ref, rw_ref, ml_ref, w_ref, gate_ref, o_ref):
    nb, tb, tn = o_ref.shape
    mixed = jnp.concatenate([hg_ref[...], rw_ref[...], ml_ref[...]], axis=-1)
    acc = jnp.dot(mixed.reshape(nb * tb, mixed.shape[-1]), w_ref[...], preferred_element_type=F32)
    o_ref[...] = x_ref[...] + gate_ref[...] * acc.reshape(nb, tb, tn)


def _out_proj(x, hg, rw, ml, w, modtab, tb, nb, tn, j0):
    b, t, d = x.shape
    return pl.pallas_call(
        _out_proj_kernel,
        out_shape=jax.ShapeDtypeStruct((b, t - j0 * tb, d), F32),
        grid=(b // nb, t // tb - j0, d // tn),
        in_specs=[
            pl.BlockSpec((nb, tb, tn), lambda bi, j, k: (bi, j + j0, k)),
            pl.BlockSpec((nb, tb, hg.shape[-1]), lambda bi, j, k: (bi, j + j0, 0)),
            pl.BlockSpec((nb, tb, rw.shape[-1]), lambda bi, j, k: (bi, j + j0, 0)),
            pl.BlockSpec((nb, tb, ml.shape[-1]), lambda bi, j, k: (bi, j + j0, 0)),
            pl.BlockSpec((d, tn), lambda bi, j, k: (0, k)),
            pl.BlockSpec((None, None, nb, 1, tn), lambda bi, j, k: (2, jnp.minimum(j + j0, 1), bi, 0, k)),
        ],
        out_specs=pl.BlockSpec((nb, tb, tn), lambda bi, j, k: (bi, j, k)),
        compiler_params=_cparams("parallel", "parallel", "arbitrary"),
    )(x, hg, rw, ml, w, modtab)


def _shift_rows(u, tb, first_rows, last_rows):
    rows = u.shape[0]
    row = lax.broadcasted_iota(jnp.int32, u.shape, 0)
    prev = pltpu.roll(u, 1, 0)
    nxt = pltpu.roll(u, rows - 1, 0)
    for n, (fr, lr) in enumerate(zip(first_rows, last_rows)):
        prev = jnp.where(row == n * tb, fr, prev)
        nxt = jnp.where(row == n * tb + tb - 1, lr, nxt)
    return prev, nxt


def _ffn_kernel(ctx_blocks, x_ref, xp_ref, xn_ref, g_ref, sh_ref, sc_ref, gate_ref, wa_ref, wb_ref, dwa_ref,
                dwb_ref, dba_ref, dbb_ref, wd_ref, o_ref, h_ref, ua_ref, ub_ref, act_ref):
    nb, tb, d = x_ref.shape
    rows = nb * tb
    halo = 2 * SUBLANE
    rt = FFN_ROW_TILE
    j = pl.program_id(1)
    k = pl.program_id(2)
    n_k = pl.num_programs(2)
    n_j = pl.num_programs(1)
    trow = lax.broadcasted_iota(jnp.int32, (rt, act_ref.shape[1]), 0)

    def conv(u_ref, dw, db, n, r):
        base = n * tb + r
        mid = u_ref[base:base + rt, :]
        if r == 0:
            edge = u_ref[rows + n * halo + SUBLANE - 1:rows + n * halo + SUBLANE, :]
            prev = jnp.where(trow == 0, edge, pltpu.roll(mid, 1, 0))
        else:
            prev = u_ref[base - 1:base - 1 + rt, :]
        if r == tb - rt:
            edge = u_ref[rows + n * halo + SUBLANE:rows + n * halo + SUBLANE + 1, :]
            nxt = jnp.where(trow == rt - 1, edge, pltpu.roll(mid, rt - 1, 0))
        else:
            nxt = u_ref[base + 1:base + 1 + rt, :]
        return prev * dw[0:1] + mid * dw[1:2] + nxt * dw[2:3] + db

    @pl.when(k == 0)
    def _():
        p_ok = (j >= ctx_blocks + 1).astype(F32)
        n_ok = jnp.logical_and(j >= ctx_blocks, j < n_j - 1).astype(F32)
        g, sh, sc = g_ref[...], sh_ref[...], sc_ref[...]
        _norm_mod_rows(x_ref, h_ref, g_ref, sh_ref, sc_ref)
        hp = (_norm_mod(xp_ref[...], g, sh, sc) * p_ok).astype(BF16)
        hn = (_norm_mod(xn_ref[...], g, sh, sc) * n_ok).astype(BF16)
        for n in range(nb):
            h_ref[rows + n * halo:rows + n * halo + SUBLANE, :] = hp[n]
            h_ref[rows + n * halo + SUBLANE:rows + (n + 1) * halo, :] = hn[n]
        o_ref[...] = jnp.zeros_like(o_ref)

    ua_ref[...] = jnp.dot(h_ref[...], wa_ref[...], preferred_element_type=F32)
    ub_ref[...] = jnp.dot(h_ref[...], wb_ref[...], preferred_element_type=F32)
    dwa, dwb, dba, dbb = dwa_ref[...], dwb_ref[...], dba_ref[...], dbb_ref[...]
    for n in range(nb):
        for r in range(0, tb, rt):
            a = conv(ua_ref, dwa, dba, n, r)
            b = conv(ub_ref, dwb, dbb, n, r)
            act_ref[n * tb + r:n * tb + r + rt, :] = (_silu(a) * b).astype(BF16)
    o_ref[...] += jnp.dot(act_ref[...], wd_ref[...], preferred_element_type=F32).reshape(nb, tb, d)

    @pl.when(k == n_k - 1)
    def _():
        o_ref[...] = x_ref[...] + gate_ref[...] * o_ref[...]


def _ffn(x, gain, modtab, w_up, dw, db, w_down, tb, nb, tk, ctx_blocks):
    b, t, d = x.shape
    n_k = w_down.shape[0] // tk
    r8 = tb // SUBLANE
    n8 = t // SUBLANE
    rows_ext = nb * tb + nb * 2 * SUBLANE

    def mspec(m):
        return pl.BlockSpec((None, None, nb, 1, d),
                            lambda bi, j, k: (m, jnp.minimum(j, 1) if ctx_blocks else 1, bi, 0, 0))

    return pl.pallas_call(
        functools.partial(_ffn_kernel, ctx_blocks),
        out_shape=jax.ShapeDtypeStruct((b, t, d), F32),
        grid=(b // nb, t // tb, n_k),
        in_specs=[
            pl.BlockSpec((nb, tb, d), lambda bi, j, k: (bi, j, 0)),
            pl.BlockSpec((nb, SUBLANE, d), lambda bi, j, k: (bi, jnp.maximum(j * r8 - 1, 0), 0)),
            pl.BlockSpec((nb, SUBLANE, d), lambda bi, j, k: (bi, jnp.minimum((j + 1) * r8, n8 - 1), 0)),
            pl.BlockSpec((1, d), lambda bi, j, k: (0, 0)),
            mspec(3), mspec(4), mspec(5),
            pl.BlockSpec((d, tk), lambda bi, j, k: (0, k)),
            pl.BlockSpec((d, tk), lambda bi, j, k: (0, n_k + k)),
            pl.BlockSpec((3, tk), lambda bi, j, k: (0, k)),
            pl.BlockSpec((3, tk), lambda bi, j, k: (0, n_k + k)),
            pl.BlockSpec((1, tk), lambda bi, j, k: (0, k)),
            pl.BlockSpec((1, tk), lambda bi, j, k: (0, n_k + k)),
            pl.BlockSpec((tk, d), lambda bi, j, k: (k, 0)),
        ],
        out_specs=pl.BlockSpec((nb, tb, d), lambda bi, j, k: (bi, j, 0)),
        scratch_shapes=[pltpu.VMEM((rows_ext, d), BF16), pltpu.VMEM((rows_ext, tk), F32),
                        pltpu.VMEM((rows_ext, tk), F32), pltpu.VMEM((nb * tb, tk), BF16)],
        compiler_params=_cparams("parallel", "parallel", "arbitrary"),
    )(x, x, x, gain, modtab, modtab, modtab, w_up, w_up, dw, dw, db, db, w_down)


def _chunk_cumsum(x, c, reverse):
    rows = x.shape[0]
    ri = lax.broadcasted_iota(jnp.int32, (rows, rows), 0)
    ci = lax.broadcasted_iota(jnp.int32, (rows, rows), 1)
    tri = jnp.logical_and((ri // c) == (ci // c), (ci >= ri) if reverse else (ci <= ri)).astype(BF16)
    hi = x.astype(BF16)
    rest = x - hi.astype(F32)
    mid = rest.astype(BF16)
    lo = (rest - mid.astype(F32)).astype(BF16)
    return (jnp.dot(tri, hi, preferred_element_type=F32) + jnp.dot(tri, mid, preferred_element_type=F32)
            + jnp.dot(tri, lo, preferred_element_type=F32))


def _bwd_block(j, n_j):
    return jnp.where(j == 0, 0, n_j - j)


def _hg_scan_kernel(qf_ref, zf_ref, if_ref, qb_ref, zb_ref, ib_ref, lb_ref, of_ref, ob_ref,
                    st_ref, g_ref, k_ref):
    nb, tb, w = zf_ref.shape
    n_heads = w // HG_HD
    n_chunks = tb // HG_CHUNK
    c = HG_CHUNK

    @pl.when(pl.program_id(1) == 0)
    def _():
        st_ref[...] = jnp.zeros_like(st_ref)

    ones = jnp.ones((HG_HD, HG_HD), BF16)
    trow = lax.broadcasted_iota(jnp.int32, (c, HG_HD), 0)

    dirs = ((qf_ref, zf_ref, if_ref, of_ref), (qb_ref, zb_ref, ib_ref, ob_ref))
    for n in range(nb):
        for dr, (q_ref, z_ref, i_ref, o_ref) in enumerate(dirs):
            lb = lb_ref[dr:dr + 1, :]
            f = lb + (1.0 - lb) * _sigmoid(z_ref[n])
            g_ref[n, dr] = _chunk_cumsum(jnp.log2(f), c, dr == 1)
            k_ref[n, dr] = 1.0 - f

    def dir_chunk(n, dr, r0):
        q_ref, _z_ref, i_ref, o_ref = dirs[dr]
        heads = []
        tiles = []
        for h in range(n_heads):
            ls = slice(h * HG_HD, (h + 1) * HG_HD)
            q = q_ref[n, pl.ds(r0, c), ls]
            v = i_ref[n, pl.ds(r0, c), ls]
            g = g_ref[n, dr, pl.ds(r0, c), ls]
            kk = k_ref[n, dr, pl.ds(r0, c), ls]
            heads.append((ls, q, v, g, kk))
            for s in range(c):
                seen = (trow >= s) if dr == 0 else (trow <= s)
                tiles.append(jnp.where(seen, q * kk[s:s + 1] * jnp.exp2(g - g[s:s + 1]), 0.0).astype(BF16))
        attn = jnp.dot(jnp.concatenate(tiles, axis=0), ones, preferred_element_type=F32)
        inter = []
        for h, (ls, q, v, g, kk) in enumerate(heads):
            g_last = g[c - 1:c] if dr == 0 else g[0:1]
            st = st_ref[n, dr, h]
            inter.append(lax.dot_general((q * jnp.exp2(g)).astype(BF16), st.astype(BF16),
                                         (((1,), (1,)), ((), ())), preferred_element_type=F32))
            kd = kk * jnp.exp2(g_last - g)
            st_ref[n, dr, h] = st * jnp.exp2(g_last) + lax.dot_general(
                v.astype(BF16), kd.astype(BF16), (((0,), (0,)), ((), ())), preferred_element_type=F32)
        yield
        for h, (ls, q, v, g, kk) in enumerate(heads):
            o = inter[h]
            for s in range(c):
                o += attn[(h * c + s) * c:(h * c + s + 1) * c] * v[s:s + 1]
            o_ref[n, pl.ds(r0, c), ls] = o

    def chunk(ic, carry):
        live = [dir_chunk(n, dr, pl.multiple_of((ic if dr == 0 else n_chunks - 1 - ic) * c, c))
                for n in range(nb) for dr in range(2)]
        while live:
            live = [gen for gen in live if next(gen, True) is None]
        return carry

    lax.fori_loop(0, n_chunks, chunk, 0)


def _hg_scan(p_hg, lb, tb, nb):
    b, t, _ = p_hg.shape
    w = lb.shape[-1]
    n_j = t // tb

    def fwd(col):
        return pl.BlockSpec((nb, tb, w), lambda bi, j: (bi, j, col))

    def bwd(col):
        return pl.BlockSpec((nb, tb, w), lambda bi, j: (bi, _bwd_block(j, n_j), col))

    out = jax.ShapeDtypeStruct((b, t, w), F32)
    return pl.pallas_call(
        _hg_scan_kernel,
        out_shape=(out, out),
        grid=(b // nb, n_j),
        in_specs=[fwd(0), fwd(1), fwd(3), bwd(0), bwd(2), bwd(3), pl.BlockSpec((2, w), lambda bi, j: (0, 0))],
        out_specs=(fwd(0), bwd(0)),
        scratch_shapes=[pltpu.VMEM((nb, 2, w // HG_HD, HG_HD, HG_HD), F32), pltpu.VMEM((nb, 2, tb, w), F32),
                        pltpu.VMEM((nb, 2, tb, w), F32)],
        compiler_params=_cparams("parallel", "arbitrary"),
    )(p_hg, p_hg, p_hg, p_hg, p_hg, p_hg, lb)


def _hg_finish_kernel(of_ref, ob_ref, gate_ref, gn_ref, o_ref):
    w = of_ref.shape[-1]
    for h in range(w // HG_HD):
        ls = slice(h * HG_HD, (h + 1) * HG_HD)
        o = of_ref[:, ls] + ob_ref[:, ls]
        o = o * lax.rsqrt(jnp.mean(o * o, axis=-1, keepdims=True) + NORM_EPS) * gn_ref[...]
        o_ref[:, ls] = (o * _silu(gate_ref[:, ls])).astype(o_ref.dtype)


def _hg_finish(o_f, o_b, p_hg, gn, tb):
    b, t, w = o_f.shape
    spec = pl.BlockSpec((None, tb, w), lambda bi, j: (bi, j, 0))
    return pl.pallas_call(
        _hg_finish_kernel,
        out_shape=jax.ShapeDtypeStruct((b, t, w), BF16),
        grid=(b, t // tb),
        in_specs=[spec, spec, pl.BlockSpec((None, tb, w), lambda bi, j: (bi, j, 4)),
                  pl.BlockSpec((1, HG_HD), lambda bi, j: (0, 0))],
        out_specs=spec,
        compiler_params=_cparams("parallel", "parallel"),
    )(o_f, o_b, p_hg, gn)


def _head_ones(width, hd):
    r = lax.broadcasted_iota(jnp.int32, (width, width), 0) // hd
    c = lax.broadcasted_iota(jnp.int32, (width, width), 1) // hd
    return (r == c).astype(F32)


def _head_sum(x, hd):
    same_head = _head_ones(LANE, hd).astype(BF16)
    out = []
    for i in range(x.shape[-1] // LANE):
        xs = x[:, i * LANE:(i + 1) * LANE]
        hi = xs.astype(BF16)
        lo = (xs - hi.astype(F32)).astype(BF16)
        out.append(jnp.dot(hi, same_head, preferred_element_type=F32)
                   + jnp.dot(lo, same_head, preferred_element_type=F32))
    return jnp.concatenate(out, axis=-1)


def _rw_prep_kernel(p_ref, pp_ref, pn_ref, mu_ref, w0_ref, w2_ref, a0_ref, a2_ref, g2_ref, kk_ref,
                    ka_ref, r_o, v_o, kk_o, wf_o, kf_o, bf_o, wb_o, kb_o, bb_o, g_o):
    tb = p_ref.shape[0]
    w = r_o.shape[-1]
    lw = w2_ref.shape[1]
    j = pl.program_id(1)
    n_j = pl.num_programs(1)
    p = p_ref[...]
    p_ok = (j >= 2).astype(F32)
    n_ok = jnp.logical_and(j >= 1, j < n_j - 1).astype(F32)
    prev, nxt = _shift_rows(p, tb, [pp_ref[SUBLANE - 1:SUBLANE, :] * p_ok], [pn_ref[0:1, :] * n_ok])
    s = p + mu_ref[...] * (0.5 * (prev + nxt) - p)
    r, k, v = s[:, 0:w], s[:, w:2 * w], s[:, 2 * w:3 * w]
    lora = [s[:, 3 * w + i * lw:3 * w + (i + 1) * lw] for i in range(5)]
    kk = k * kk_ref[...]
    ssq = _head_sum(kk * kk, RW_HD)
    kk = kk / jnp.maximum(jnp.sqrt(ssq), 1e-12)
    r_o[...] = r
    v_o[...] = v
    kk_o[...] = kk
    for dr, (w_o, k_o, b_o) in enumerate(((wf_o, kf_o, bf_o), (wb_o, kb_o, bb_o))):
        xw, xa = lora[dr], lora[2 + dr]
        w_o[...] = -RW_DECAY_MAX * _sigmoid(w0_ref[dr:dr + 1, :] + _bdot(jnp.tanh(xw), w2_ref[dr]))
        a = _sigmoid(a0_ref[dr:dr + 1, :] + _bdot(xa, a2_ref[dr]))
        k_o[...] = k * (1.0 + (a - 1.0) * ka_ref[...])
        b_o[...] = kk * a
    g_o[...] = _bdot(_sigmoid(lora[4]), g2_ref[...])


def _rw_prep(p_rw, mu, w0, w2, a0, a2, g2, k_k, k_a, tb):
    b, t, n = p_rw.shape
    w = w0.shape[-1]
    r8, n8 = tb // SUBLANE, t // SUBLANE
    out = jax.ShapeDtypeStruct((b, t, w), F32)
    ospec = pl.BlockSpec((None, tb, w), lambda bi, j: (bi, j, 0))

    def const(a):
        return pl.BlockSpec(a.shape, lambda bi, j: (0,) * a.ndim)

    return pl.pallas_call(
        _rw_prep_kernel,
        out_shape=(out,) * 10,
        grid=(b, t // tb),
        in_specs=[
            pl.BlockSpec((None, tb, n), lambda bi, j: (bi, j, 0)),
            pl.BlockSpec((None, SUBLANE, n), lambda bi, j: (bi, jnp.maximum(j * r8 - 1, 0), 0)),
            pl.BlockSpec((None, SUBLANE, n), lambda bi, j: (bi, jnp.minimum((j + 1) * r8, n8 - 1), 0)),
            const(mu), const(w0), const(w2), const(a0), const(a2), const(g2), const(k_k), const(k_a),
        ],
        out_specs=(ospec,) * 10,
        compiler_params=_cparams("parallel", "parallel"),
    )(p_rw, p_rw, p_rw, mu, w0, w2, a0, a2, g2, k_k, k_a)


def _rw_scan_kernel(rf_ref, vf_ref, kkf_ref, wf_ref, kf_ref, bf_ref, rb_ref, vb_ref, kkb_ref, wb_ref,
                    kb_ref, bb_ref, of_ref, ob_ref, st_ref, g_ref):
    nb, tb, w = rf_ref.shape
    n_pairs = w // LANE
    hd = RW_HD
    c = RW_CHUNK
    n_chunks = tb // c
    assert c == hd and tb % c == 0

    @pl.when(pl.program_id(1) == 0)
    def _():
        st_ref[...] = jnp.zeros_like(st_ref)

    dirs = ((rf_ref, vf_ref, kkf_ref, wf_ref, kf_ref, bf_ref, of_ref),
            (rb_ref, vb_ref, kkb_ref, wb_ref, kb_ref, bb_ref, ob_ref))
    for n in range(nb):
        for dr, refs in enumerate(dirs):
            g_ref[n, dr] = _chunk_cumsum(refs[3][n], c, dr == 1)

    row = lax.broadcasted_iota(jnp.int32, (c, LANE), 0)
    col = lax.broadcasted_iota(jnp.int32, (c, LANE), 1) % hd
    head_a = lax.broadcasted_iota(jnp.int32, (c, LANE), 1) < hd
    head_a2 = (lax.broadcasted_iota(jnp.int32, (c, 2 * LANE), 1) % LANE) < hd
    r2 = lax.broadcasted_iota(jnp.int32, (LANE, LANE), 0)
    c2 = lax.broadcasted_iota(jnp.int32, (LANE, LANE), 1)
    same_head = (r2 // hd) == (c2 // hd)
    eye = r2 == c2

    def stack2(y):
        m = head_a if y.shape[1] == LANE else head_a2
        return jnp.concatenate([jnp.where(m, y, 0.0), jnp.where(m, 0.0, y)], axis=0).astype(BF16)

    def tn(x, y):
        return lax.dot_general(x.astype(BF16), y.astype(BF16), (((0,), (0,)), ((), ())),
                               preferred_element_type=F32)

    def pair_chunk(n, dr, hp, r0):
        refs = dirs[dr]
        before = (col < row) if dr == 0 else (col > row)
        upto = (col <= row) if dr == 0 else (col >= row)
        ls = slice(hp * LANE, (hp + 1) * LANE)
        rr, vv, kk, lw, kd, bb = [ref[n, pl.ds(r0, c), ls] for ref in refs[:6]]
        g = g_ref[n, dr, pl.ds(r0, c), ls]
        slot = (n * 2 + dr) * n_pairs + hp
        g_end = g[c - 1:c] if dr == 0 else g[0:1]
        at = -kk * jnp.exp(g - lw)
        rt = rr * jnp.exp(g)
        e_inv = jnp.exp(-g)
        e_out = jnp.exp(g_end - g)
        bh_kh = jnp.concatenate([bb * e_out, kd * e_out], axis=0)
        abk = lax.dot_general(jnp.concatenate([at, rt], axis=0).astype(BF16),
                              jnp.concatenate([stack2(bb * e_inv), stack2(kd * e_inv)], axis=0),
                              (((1,), (1,)), ((), ())), preferred_element_type=F32)
        yield
        x = jnp.where(before, abk[0:c, 0:LANE], 0.0)
        a_ak = jnp.where(before, abk[0:c, LANE:], 0.0)
        a_r = jnp.concatenate([jnp.where(upto, abk[c:, 0:LANE], 0.0), jnp.where(upto, abk[c:, LANE:], 0.0)], axis=1)
        u0 = jnp.dot(a_ak.astype(BF16), stack2(vv), preferred_element_type=F32)
        yield
        y = jnp.concatenate([at, u0], axis=1)
        span = 1
        while span < c:
            span *= 2
            rhs = stack2(y) if span >= c else jnp.concatenate([stack2(y), stack2(x)], axis=1)
            z = jnp.dot(x.astype(BF16), rhs, preferred_element_type=F32)
            yield
            y = y + z[:, 0:2 * LANE]
            if span < c:
                x = z[:, 2 * LANE:]
        zero = jnp.zeros((c, LANE), F32)
        v_pad = jnp.concatenate([zero, vv], axis=1)
        wu = jnp.dot(a_r.astype(BF16), jnp.concatenate([stack2(y), stack2(v_pad)], axis=0),
                     preferred_element_type=F32)
        mn = tn(bh_kh, jnp.concatenate([y, v_pad], axis=0))
        yield
        q_eff = rt + wu[:, 0:LANE]
        m_bd = jnp.where(same_head, mn[:, 0:LANE], 0.0) + jnp.where(eye, jnp.exp(g_end), 0.0)
        n_bd = jnp.where(same_head, mn[:, LANE:], 0.0)
        st = st_ref[slot].astype(BF16)
        so = jnp.dot(jnp.concatenate([q_eff, m_bd], axis=0).astype(BF16), st, preferred_element_type=F32)
        yield
        refs[6][n, pl.ds(r0, c), ls] = so[0:c] + wu[:, LANE:]
        st_ref[slot] = so[c:] + n_bd

    def chunk(ic, carry):
        live = [pair_chunk(n, dr, hp, pl.multiple_of((ic if dr == 0 else n_chunks - 1 - ic) * c, c))
                for n in range(nb) for dr in range(2) for hp in range(n_pairs)]
        while live:
            live = [gen for gen in live if next(gen, True) is None]
        return carry

    lax.fori_loop(0, n_chunks, chunk, 0)


def _rw_scan(r, v, kk, w_f, k_f, b_f, w_b, k_b, b_b, tb, nb):
    b, t, w = r.shape
    n_j = t // tb
    fwd = pl.BlockSpec((nb, tb, w), lambda bi, j: (bi, j, 0))
    bwd = pl.BlockSpec((nb, tb, w), lambda bi, j: (bi, _bwd_block(j, n_j), 0))
    out = jax.ShapeDtypeStruct((b, t, w), F32)
    return pl.pallas_call(
        _rw_scan_kernel,
        out_shape=(out, out),
        grid=(b // nb, n_j),
        in_specs=[fwd] * 6 + [bwd] * 6,
        out_specs=(fwd, bwd),
        scratch_shapes=[pltpu.VMEM((nb * 2 * (w // LANE), LANE, LANE), F32), pltpu.VMEM((nb, 2, tb, w), F32)],
        compiler_params=_cparams("parallel", "arbitrary"),
    )(r, v, kk, w_f, k_f, b_f, r, v, kk, w_b, k_b, b_b)


def _rw_finish_kernel(of_ref, ob_ref, r_ref, kf_ref, kb_ref, v_ref, g_ref, gnw_ref, gnb_ref, rk_ref, o_ref):
    o = of_ref[...] + ob_ref[...]
    cen = o - _head_sum(o, RW_HD) * (1.0 / RW_HD)
    var = _head_sum(cen * cen, RW_HD) * (1.0 / RW_HD)
    o = cen * lax.rsqrt(var + RW_GN_EPS) * gnw_ref[...] + gnb_ref[...]
    dot_rk = _head_sum(r_ref[...] * (kf_ref[...] + kb_ref[...]) * rk_ref[...], RW_HD)
    o_ref[...] = ((o + dot_rk * v_ref[...]) * g_ref[...]).astype(o_ref.dtype)


def _rw_finish(o_f, o_b, r, k_f, k_b, v, g, gn_w, gn_b, r_k, tb):
    b, t, w = o_f.shape
    spec = pl.BlockSpec((None, tb, w), lambda bi, j: (bi, j, 0))
    cspec = pl.BlockSpec((1, w), lambda bi, j: (0, 0))
    return pl.pallas_call(
        _rw_finish_kernel,
        out_shape=jax.ShapeDtypeStruct((b, t, w), BF16),
        grid=(b, t // tb),
        in_specs=[spec] * 7 + [cspec] * 3,
        out_specs=spec,
        compiler_params=_cparams("parallel", "parallel"),
    )(o_f, o_b, r, k_f, k_b, v, g, gn_w, gn_b, r_k)


def _rms(x, gain, n):
    return x * lax.rsqrt(jnp.sum(x * x, axis=-1, keepdims=True) * (1.0 / n) + NORM_EPS) * gain


def _rope128(x, cos, sin):
    half = MLA_ROPE // 2
    lane = lax.broadcasted_iota(jnp.int32, x.shape, 1)
    swapped = jnp.where(lane < half, pltpu.roll(x, LANE - half, 1), pltpu.roll(x, half, 1))
    return x * cos + swapped * sin


def _mla_prep_kernel(p_ref, cos_ref, sin_ref, qn_ref, kvn_ref, wq_ref, wkv_ref, qng_ref, qrg_ref,
                     kng_ref, krg_ref, q_o, k_o, v_o):
    q_rank = qn_ref.shape[-1]
    kv_rank = kvn_ref.shape[-1]
    n_heads = v_o.shape[-1] // MLA_V
    slot = MLA_NOPE + LANE
    cos, sin = cos_ref[...], sin_ref[...]
    p = p_ref[...]
    q = _bdot(_rms(p[:, 0:q_rank], qn_ref[...], q_rank), wq_ref[...])
    kv = _bdot(_rms(p[:, q_rank:q_rank + kv_rank], kvn_ref[...], kv_rank), wkv_ref[...])
    k_rope = _rope128(_rms(p[:, q_rank + kv_rank:], krg_ref[...], MLA_ROPE), cos, sin).astype(k_o.dtype)
    for h in range(n_heads):
        q_nope = _rms(q[:, h * slot:h * slot + MLA_NOPE], qng_ref[...], MLA_NOPE)
        q_rope = _rope128(_rms(q[:, h * slot + MLA_NOPE:(h + 1) * slot], qrg_ref[...], MLA_ROPE), cos, sin)
        q_o[:, h * slot:h * slot + MLA_NOPE] = (q_nope * (MLA_SCALE * LOG2E)).astype(q_o.dtype)
        q_o[:, h * slot + MLA_NOPE:(h + 1) * slot] = (q_rope * (MLA_SCALE * LOG2E)).astype(q_o.dtype)
        k_nope = _rms(kv[:, h * MLA_NOPE:(h + 1) * MLA_NOPE], kng_ref[...], MLA_NOPE)
        k_o[:, h * slot:h * slot + MLA_NOPE] = k_nope.astype(k_o.dtype)
        k_o[:, h * slot + MLA_NOPE:(h + 1) * slot] = k_rope
    v_o[...] = kv[:, n_heads * MLA_NOPE:].astype(v_o.dtype)


def _mla_prep(p_ml, cos, sin, q_norm, kv_norm, w_uq, w_ukv, qn_g, qr_g, kn_g, kr_g, tb):
    b, t, n = p_ml.shape
    n_heads = w_uq.shape[1] // (MLA_NOPE + LANE)

    def const(a):
        return pl.BlockSpec(a.shape, lambda bi, j: (0,) * a.ndim)

    def out(width):
        return (jax.ShapeDtypeStruct((b, t, width), BF16), pl.BlockSpec((None, tb, width), lambda bi, j: (bi, j, 0)))

    outs = [out(n_heads * (MLA_NOPE + LANE)), out(n_heads * (MLA_NOPE + LANE)), out(n_heads * MLA_V)]
    return pl.pallas_call(
        _mla_prep_kernel,
        out_shape=tuple(o[0] for o in outs),
        grid=(b, t // tb),
        in_specs=[
            pl.BlockSpec((None, tb, n), lambda bi, j: (bi, j, 0)),
            pl.BlockSpec((tb, LANE), lambda bi, j: (j, 0)),
            pl.BlockSpec((tb, LANE), lambda bi, j: (j, 0)),
            const(q_norm), const(kv_norm), const(w_uq), const(w_ukv), const(qn_g), const(qr_g),
            const(kn_g), const(kr_g),
        ],
        out_specs=tuple(o[1] for o in outs),
        compiler_params=_cparams("parallel", "parallel"),
    )(p_ml, cos, sin, q_norm, kv_norm, w_uq, w_ukv, qn_g, qr_g, kn_g, kr_g)


def _mla_attn_kernel(q_ref, k_ref, v_ref, o_ref):
    tb = q_ref.shape[0]
    slot = MLA_NOPE + LANE
    n_heads = o_ref.shape[1] // MLA_V

    def head(h, n_keys):
        qs = slice(h * slot, (h + 1) * slot)
        vs = slice(h * MLA_V, (h + 1) * MLA_V)
        s = lax.dot_general(q_ref[:, qs], k_ref[0:n_keys, qs], (((1,), (1,)), ((), ())),
                            preferred_element_type=F32)
        yield
        e = jnp.exp2(s - jnp.max(s, axis=-1, keepdims=True))
        o = jnp.dot(e.astype(BF16), v_ref[0:n_keys, vs], preferred_element_type=F32)
        yield
        o_ref[:, vs] = (o / jnp.sum(e, axis=-1, keepdims=True)).astype(o_ref.dtype)

    def attend(n_keys):
        live = [head(h, n_keys) for h in range(n_heads)]
        while live:
            live = [gen for gen in live if next(gen, True) is None]

    @pl.when(pl.program_id(2) == 0)
    def _():
        attend(tb)

    @pl.when(pl.program_id(2) > 0)
    def _():
        attend(k_ref.shape[0])


def _mla_attn(q, k, v, tb):
    b, t, _ = q.shape
    n_heads = v.shape[-1] // MLA_V
    slot = MLA_NOPE + LANE
    hg = ATTN_HEADS if n_heads % ATTN_HEADS == 0 else 1
    return pl.pallas_call(
        _mla_attn_kernel,
        out_shape=jax.ShapeDtypeStruct((b, t, n_heads * MLA_V), BF16),
        grid=(b, n_heads // hg, t // tb),
        in_specs=[
            pl.BlockSpec((None, tb, hg * slot), lambda bi, h, j: (bi, j, h)),
            pl.BlockSpec((None, t, hg * slot), lambda bi, h, j: (bi, 0, h)),
            pl.BlockSpec((None, t, hg * MLA_V), lambda bi, h, j: (bi, 0, h)),
        ],
        out_specs=pl.BlockSpec((None, tb, hg * MLA_V), lambda bi, h, j: (bi, j, h)),
        compiler_params=_cparams("parallel", "parallel", "arbitrary"),
    )(q, k, v)


def _rope_tables(n_ctx, n_lat):
    rows = n_lat // GRID_W
    row = jnp.repeat(jnp.arange(rows, dtype=F32), GRID_W)
    col = jnp.tile(jnp.arange(GRID_W, dtype=F32), rows)
    axis_dim = MLA_ROPE // 2
    inv_freq = ROPE_THETA ** (-jnp.arange(0, axis_dim, 2, dtype=F32) / axis_dim)
    ang = jnp.concatenate([row[:, None] * inv_freq, col[:, None] * inv_freq], axis=-1)
    ang = jnp.concatenate([jnp.zeros((n_ctx, axis_dim), F32), ang], axis=0)
    cos, sin = jnp.cos(ang), jnp.sin(ang)
    return (_pad_cols(jnp.concatenate([cos, cos], axis=-1), LANE),
            _pad_cols(jnp.concatenate([-sin, sin], axis=-1), LANE))


def _pick(n, prefs):
    for p in prefs:
        if n % p == 0:
            return p
    return n


def kernel(x, c, ctx, c_ctx, norm_g, w_mod, b_mod, w_in, w_out, hg_lb, hg_gn, rw_mu, rw_w0, rw_w2, rw_a0, rw_a2, rw_g2, rw_kk, rw_ka, rw_rk, rw_gn_w, rw_gn_b, mla_q_norm, mla_w_uq, mla_kv_norm, mla_w_ukv, mla_qn_g, mla_qr_g, mla_kn_g, mla_kr_g, ffn_up, ffn_dw, ffn_db, ffn_down):
    bsz, n_lat, d = x.shape
    n_ctx = ctx.shape[1]
    depth = w_in.shape[0]
    hg_w = hg_lb.shape[-1]
    rw_w = rw_w0.shape[-1]
    q_rank = mla_q_norm.shape[-1]
    kv_rank = mla_kv_norm.shape[-1]
    n_heads = mla_w_ukv.shape[-1] // (MLA_NOPE + MLA_V)
    d_ff = ffn_down.shape[1]
    lora = (rw_w2.shape[2], rw_w2.shape[2], rw_a2.shape[2], rw_a2.shape[2], rw_g2.shape[1])
    tb = n_ctx
    nb = 2 if bsz % 2 == 0 else 1
    nb_mm = 4 if bsz % 4 == 0 else nb
    assert n_lat % tb == 0 and tb % HG_CHUNK == 0 and max(lora) <= LANE

    cc = jnp.concatenate([c, c_ctx[None, :]], axis=0)
    cc = jnp.pad(cc, ((0, -(bsz + 1) % SUBLANE), (0, 0)))
    mods = _mods(cc, w_mod, b_mod)
    mod_lat = mods[:, :bsz].reshape(depth, bsz, 6, d).transpose(0, 2, 1, 3)
    mod_ctx = jnp.broadcast_to(mods[:, bsz].reshape(depth, 6, 1, d), (depth, 6, bsz, d))
    modtab = jnp.stack([mod_ctx, mod_lat], axis=2)[:, :, :, :, None, :]

    lb_p = jax.nn.softmax(hg_lb.astype(F32), axis=0)
    lower = jnp.cumsum(lb_p, axis=0) - lb_p[0]

    hg_cols = 5 * hg_w
    rw_cols = 3 * rw_w + sum(lora)
    w_in_b = w_in.astype(BF16)
    w_hg = w_in_b[:, :, :hg_cols]
    w_rw_raw = w_in_b[:, :, hg_cols:hg_cols + rw_cols]
    w_ml = w_in_b[:, :, hg_cols + rw_cols:]
    offs = [3 * rw_w]
    for n in lora:
        offs.append(offs[-1] + n)

    def pad_lora(a):
        parts = [a[..., :3 * rw_w]] + [_pad_cols(a[..., offs[i]:offs[i + 1]], LANE) for i in range(5)]
        return jnp.concatenate(parts, axis=-1)

    w_rw = pad_lora(w_rw_raw)
    mu = pad_lora(rw_mu)[:, None, :]
    w_ml = _pad_cols(w_ml, q_rank + kv_rank + LANE)

    def pad_rows(a):
        return jnp.pad(a, ((0, 0),) * (a.ndim - 2) + ((0, LANE - a.shape[-2]), (0, 0)))

    w2 = pad_rows(rw_w2).astype(BF16)
    a2 = pad_rows(rw_a2).astype(BF16)
    g2 = pad_rows(rw_g2).astype(BF16)

    wq = mla_w_uq.reshape(depth, q_rank, n_heads, MLA_NOPE + MLA_ROPE)
    wq = _pad_cols(wq, MLA_NOPE + LANE).reshape(depth, q_rank, n_heads * (MLA_NOPE + LANE)).astype(BF16)
    wkv = mla_w_ukv.reshape(depth, kv_rank, n_heads, MLA_NOPE + MLA_V)
    wkv = jnp.concatenate([wkv[..., :MLA_NOPE].reshape(depth, kv_rank, -1),
                           wkv[..., MLA_NOPE:].reshape(depth, kv_rank, -1)], axis=-1).astype(BF16)
    cos, sin = _rope_tables(n_ctx, n_lat)

    w_out_b = w_out.astype(BF16)
    tk = _pick(d_ff, (512, 256, 128))
    up_b = ffn_up.astype(BF16)
    down_b = ffn_down.astype(BF16)

    xs = jnp.concatenate([ctx, x], axis=1)
    for l in range(depth):
        gain0, gain1 = norm_g[l, 0:1], norm_g[l, 1:2]
        mt = modtab[l]
        h = _norm(xs, gain0, mt, tb, nb)
        p_hg = _in_proj(h, w_hg[l], tb, nb, hg_cols)
        p_rw = _in_proj(h, w_rw[l], tb, nb, w_rw.shape[-1])
        p_ml = _in_proj(h, w_ml[l], tb, nb_mm, w_ml.shape[-1])

        hg_f, hg_b = _hg_scan(p_hg, lower[l], tb, nb)
        hg_o = _hg_finish(hg_f, hg_b, p_hg, hg_gn[l][None, :], tb)

        r, v, kk, w_f, k_f, b_f, w_b, k_b, b_b, g = _rw_prep(
            p_rw, mu[l], rw_w0[l], w2[l], rw_a0[l], a2[l], g2[l], rw_kk[l][None, :], rw_ka[l][None, :], tb)
        rw_f, rw_b = _rw_scan(r, v, kk, w_f, k_f, b_f, w_b, k_b, b_b, tb, nb)
        rw_o = _rw_finish(rw_f, rw_b, r, k_f, k_b, v, g, rw_gn_w[l][None, :], rw_gn_b[l][None, :],
                          rw_rk[l].reshape(1, rw_w), tb)

        q, k, vv = _mla_prep(p_ml, cos, sin, mla_q_norm[l][None, :], mla_kv_norm[l][None, :], wq[l], wkv[l],
                             mla_qn_g[l][None, :], _pad_cols(mla_qr_g[l][None, :], LANE),
                             mla_kn_g[l][None, :], _pad_cols(mla_kr_g[l][None, :], LANE), tb)
        ml_o = _mla_attn(q, k, vv, tb)

        last = l == depth - 1
        xs = _out_proj(xs, hg_o, rw_o, ml_o, w_out_b[l], mt, tb, nb, d, 1 if last else 0)
        xs = _ffn(xs, gain1, mt, up_b[l], ffn_dw[l], ffn_db[l][None, :], down_b[l], tb, nb, tk, 0 if last else 1)
    return xs
```
